```python
import math
import jax
import jax.numpy as jnp
from jax import lax
import numpy as np

D_MODEL = 2048
BATCH = 1
SEQ = 8192
DEPTH = 4

GRID_W = 64
CTX_LEN = 256
HEAD_DIM = 128
A_Q_HEADS = 8
A_KV_HEADS = 2
B_Q_HEADS = 8
B_KV_HEADS = 2
WINDOW = 128
ATTN_BLOCK = 128
ROPE_THETA = 10000.0
AXIS_DIM = HEAD_DIM // 2
Q_WIDTH = (A_Q_HEADS + B_Q_HEADS) * HEAD_DIM
KV_WIDTH = (A_KV_HEADS + B_KV_HEADS) * HEAD_DIM
ATTN_IN_WIDTH = Q_WIDTH + 2 * KV_WIDTH
HYENA_ORDER = 2
HYENA_IN_WIDTH = (HYENA_ORDER + 1) * D_MODEL
SHORT_CONV = 3
FILTER_EMB = 33
FILTER_HIDDEN = 64
DECAY_TARGET = 1e-2
FAST_DECAY_PCT = 0.3
SLOW_DECAY_PCT = 1.5
MAX_DECAY = math.log(DECAY_TARGET) / FAST_DECAY_PCT
MIN_DECAY = math.log(DECAY_TARGET) / SLOW_DECAY_PCT
D_FF = 4 * D_MODEL
N_MOD = 6
N_ATTN_LAYERS = (DEPTH + 1) // 2
N_HYENA_LAYERS = DEPTH // 2
EPS = 1e-6

kernel_name = 'hybrid_swa_axial_hyena_dit_block'


def rms_norm(x, g):
    x32 = x.astype(jnp.float32)
    y = x32 * lax.rsqrt(jnp.mean(x32 * x32, axis=-1, keepdims=True) + EPS)
    return y.astype(x.dtype) * g


def modulate(h, shift, scale):
    return h * (1 + scale) + shift


def axial_rope_tables(n_tok):
    rows = n_tok // GRID_W
    row = jnp.repeat(jnp.arange(rows, dtype=jnp.float32), GRID_W)
    col = jnp.tile(jnp.arange(GRID_W, dtype=jnp.float32), rows)
    inv = ROPE_THETA ** (-jnp.arange(0, AXIS_DIM, 2, dtype=jnp.float32) / AXIS_DIM)
    ang_r = row[:, None] * inv[None]
    ang_c = col[:, None] * inv[None]
    return (jnp.cos(ang_r), jnp.sin(ang_r), jnp.cos(ang_c), jnp.sin(ang_c))


def _rotate(x, cos, sin):
    x1, x2 = jnp.split(x, 2, axis=-1)
    cos = cos[:, None, :].astype(x.dtype)
    sin = sin[:, None, :].astype(x.dtype)
    return jnp.concatenate([x1 * cos - x2 * sin, x2 * cos + x1 * sin], axis=-1)


def apply_axial_rope(x, rope):
    cos_r, sin_r, cos_c, sin_c = rope
    xr, xc = jnp.split(x, 2, axis=-1)
    return jnp.concatenate([_rotate(xr, cos_r, sin_r), _rotate(xc, cos_c, sin_c)], axis=-1)


def _split_cols(t, sizes):
    idx = [int(v) for v in np.cumsum(sizes)[:-1]]
    return jnp.split(t, idx, axis=-1)


def _heads(t):
    return t.reshape(t.shape[0], t.shape[1], -1, HEAD_DIM)


def _gqa(q, n_kv):
    b, s, hq, dh = q.shape
    return q.reshape(b, s, n_kv, hq // n_kv, dh)


def window_attention(q, k, v, k_ctx, v_ctx, sink):
    b, s, hkv, g, dh = q.shape
    nb = s // ATTN_BLOCK
    nw = -(-WINDOW // ATTN_BLOCK)
    kw_len = (2 * nw + 1) * ATTN_BLOCK
    n_ctx = k_ctx.shape[1]
    qb = q.reshape(b, nb, ATTN_BLOCK, hkv, g, dh)
    pad = ((0, 0), (nw * ATTN_BLOCK, nw * ATTN_BLOCK), (0, 0), (0, 0))
    kp = jnp.pad(k, pad).reshape(b, nb + 2 * nw, ATTN_BLOCK, hkv, dh)
    vp = jnp.pad(v, pad).reshape(b, nb + 2 * nw, ATTN_BLOCK, hkv, dh)
    kw = jnp.concatenate([kp[:, o:o + nb] for o in range(2 * nw + 1)], axis=2)
    vw = jnp.concatenate([vp[:, o:o + nb] for o in range(2 * nw + 1)], axis=2)
    blk = jnp.arange(nb)[:, None] * ATTN_BLOCK
    q_pos = blk + jnp.arange(ATTN_BLOCK)[None]
    k_pos = blk - nw * ATTN_BLOCK + jnp.arange(kw_len)[None]
    valid = (jnp.abs(k_pos[:, None, :] - q_pos[:, :, None]) <= WINDOW) & ((k_pos >= 0) & (k_pos < s))[:, None, :]
    s_loc = jnp.einsum('bnqhgd,bnkhd->bnhgqk', qb, kw).astype(jnp.float32)
    s_loc = jnp.where(valid[None, :, None, None], s_loc, -jnp.inf)
    s_ctx = jnp.einsum('bnqhgd,bchd->bnhgqc', qb, k_ctx).astype(jnp.float32)
    s_sink = jnp.broadcast_to(sink.astype(jnp.float32).reshape(hkv, g)[None, None, :, :, None, None], s_loc.shape[:-1] + (1,))
    p = jax.nn.softmax(jnp.concatenate([s_loc, s_ctx, s_sink], axis=-1), axis=-1).astype(v.dtype)
    o = (jnp.einsum('bnhgqk,bnkhd->bnqhgd', p[..., :kw_len], vw)
         + jnp.einsum('bnhgqc,bchd->bnqhgd', p[..., kw_len:kw_len + n_ctx], v_ctx))
    return o.reshape(b, s, hkv * g, dh)


def global_attention(q, k_all, v_all):
    b, s, hkv, g, dh = q.shape
    nb = s // ATTN_BLOCK
    qb = jnp.moveaxis(q.reshape(b, nb, ATTN_BLOCK, hkv, g, dh), 1, 0)

    def one_block(q_blk):
        sc = jnp.einsum('bqhgd,bkhd->bhgqk', q_blk, k_all).astype(jnp.float32)
        p = jax.nn.softmax(sc, axis=-1).astype(v_all.dtype)
        return jnp.einsum('bhgqk,bkhd->bqhgd', p, v_all)

    o = lax.map(one_block, qb)
    return jnp.moveaxis(o, 0, 1).reshape(b, s, hkv * g, dh)


def context_attention(q, k, v, sink=None):
    b, n, hkv, g, dh = q.shape
    sc = jnp.einsum('bqhgd,bkhd->bhgqk', q, k).astype(jnp.float32)
    if sink is not None:
        s_sink = jnp.broadcast_to(sink.astype(jnp.float32).reshape(hkv, g)[None, :, :, None, None], sc.shape[:-1] + (1,))
        p = jax.nn.softmax(jnp.concatenate([sc, s_sink], axis=-1), axis=-1)[..., :-1]
    else:
        p = jax.nn.softmax(sc, axis=-1)
    o = jnp.einsum('bhgqk,bkhd->bqhgd', p.astype(v.dtype), v)
    return o.reshape(b, n, hkv * g, dh)


def attention_mixer(h_lat, h_ctx, w_in, w_out, sink, q_norm, k_norm, rope, ctx_out):
    b, s, _ = h_lat.shape
    n_ctx = h_ctx.shape[1]
    scale = HEAD_DIM ** -0.5
    kv_sizes = [A_KV_HEADS * HEAD_DIM, B_KV_HEADS * HEAD_DIM, A_KV_HEADS * HEAD_DIM, B_KV_HEADS * HEAD_DIM]
    q_sizes = [A_Q_HEADS * HEAD_DIM, B_Q_HEADS * HEAD_DIM]
    qa, qb, ka, kb, va, vb = [_heads(t) for t in _split_cols(h_lat @ w_in, q_sizes + kv_sizes)]
    qb = rms_norm(qb, q_norm)
    kb = rms_norm(kb, k_norm)
    qa, ka, qb, kb = [apply_axial_rope(t, rope) for t in (qa, ka, qb, kb)]
    ka_c, kb_c, va_c, vb_c = [_heads(t) for t in _split_cols(h_ctx @ w_in[:, Q_WIDTH:], kv_sizes)]
    kb_c = rms_norm(kb_c, k_norm)
    o_a = window_attention(_gqa(qa * scale, A_KV_HEADS), ka, va, ka_c, va_c, sink)
    o_b = global_attention(_gqa(qb * scale, B_KV_HEADS),
                           jnp.concatenate([kb_c, kb], axis=1), jnp.concatenate([vb_c, vb], axis=1))
    y_lat = jnp.concatenate([o_a, o_b], axis=2).reshape(b, s, Q_WIDTH) @ w_out
    if not ctx_out:
        return y_lat, None
    qa_c, qb_c = [_heads(t) for t in _split_cols(h_ctx @ w_in[:, :Q_WIDTH], q_sizes)]
    qb_c = rms_norm(qb_c, q_norm)
    o_ac = context_attention(_gqa(qa_c * scale, A_KV_HEADS), ka_c, va_c, sink)
    o_bc = context_attention(_gqa(qb_c * scale, B_KV_HEADS), kb_c, vb_c)
    y_ctx = jnp.concatenate([o_ac, o_bc], axis=2).reshape(b, n_ctx, Q_WIDTH) @ w_out
    return y_lat, y_ctx


def hyena_filters(n_tok, w1, b1, fr1, w2, b2, fr2, w3):
    f32 = jnp.float32
    bands = (FILTER_EMB - 1) // 2
    t = jnp.linspace(0.0, 1.0, n_tok, dtype=f32)[:, None]
    w = 2.0 * math.pi * jnp.arange(n_tok, dtype=f32)[:, None] / n_tok
    f = jnp.linspace(1e-4, bands - 1, bands, dtype=f32)[None]
    feats = jnp.concatenate([t, jnp.cos(f * w), -jnp.sin(f * w)], axis=-1)
    hid = jnp.sin(fr1.astype(f32) * (feats @ w1.astype(f32) + b1.astype(f32)))
    hid = jnp.sin(fr2.astype(f32) * (hid @ w2.astype(f32) + b2.astype(f32)))
    h = (hid @ w3.astype(f32)).reshape(n_tok, HYENA_ORDER, D_MODEL)
    offs = jnp.abs(jnp.arange(n_tok) - n_tok // 2).astype(f32) * (2.0 / n_tok)
    deltas = jnp.abs(jnp.linspace(MIN_DECAY, MAX_DECAY, D_MODEL, dtype=f32))
    h = h * jnp.exp(-offs[:, None, None] * deltas[None, None, :])
    return h * lax.rsqrt(jnp.sum(h * h, axis=0, keepdims=True) + EPS)


def centred_long_conv(z, h):
    n_tok = z.shape[1]
    n_fft = 2 * n_tok
    zf = jnp.fft.rfft(z.astype(jnp.float32), n=n_fft, axis=1)
    hf = jnp.fft.rfft(h, n=n_fft, axis=0)
    y = jnp.fft.irfft(zf * hf[None], n=n_fft, axis=1)[:, n_tok // 2:n_tok // 2 + n_tok]
    return y.astype(z.dtype)


def centred_short_conv(u, w, b):
    n_tok = u.shape[1]
    half = SHORT_CONV // 2
    up = jnp.pad(u, ((0, 0), (half, half), (0, 0)))
    return sum(up[:, k:k + n_tok] * w[k] for k in range(SHORT_CONV)) + b


def hyena_mixer(h, w_in, b_in, conv_w, conv_b, filt, skip, w_out, b_out):
    u = centred_short_conv(h @ w_in + b_in, conv_w, conv_b)
    x1, x2, v = jnp.split(u, HYENA_ORDER + 1, axis=-1)
    z = x1 * (centred_long_conv(v, filt[:, 0]) + v * skip[0])
    z = x2 * (centred_long_conv(z, filt[:, 1]) + z * skip[1])
    return z @ w_out + b_out


def sq_relu_mlp(h, w1, w2):
    return jnp.square(jax.nn.relu(h @ w1)) @ w2


def setup_inputs(seed: int = 0) -> dict:
    key = jax.random.key(seed)
    ks = iter(jax.random.split(key, 40))

    def nrm(shape, scale):
        return scale * jax.random.normal(next(ks), shape, jnp.float32)

    D = D_MODEL
    NA = N_ATTN_LAYERS
    NH = N_HYENA_LAYERS
    return {
        'x': nrm((BATCH, SEQ, D), 1.0),
        'c': nrm((BATCH, D), 1.0),
        'ctx': nrm((BATCH, CTX_LEN, D), 1.0),
        'c_ctx': nrm((D,), 1.0),
        'mod_w': nrm((DEPTH, D, N_MOD * D), 0.5 * D ** -0.5),
        'mod_b': nrm((DEPTH, N_MOD * D), 0.01),
        'norm_mix_g': 1.0 + nrm((DEPTH, D), 0.02),
        'norm_mlp_g': 1.0 + nrm((DEPTH, D), 0.02),
        'attn_w_in': nrm((NA, D, ATTN_IN_WIDTH), D ** -0.5),
        'attn_w_out': nrm((NA, Q_WIDTH, D), Q_WIDTH ** -0.5),
        'attn_sink': nrm((NA, A_Q_HEADS), 1.0),
        'attn_q_norm': 1.0 + nrm((NA, HEAD_DIM), 0.02),
        'attn_k_norm': 1.0 + nrm((NA, HEAD_DIM), 0.02),
        'hy_w_in': nrm((NH, D, HYENA_IN_WIDTH), D ** -0.5),
        'hy_b_in': nrm((NH, HYENA_IN_WIDTH), 0.01),
        'hy_conv_w': nrm((NH, SHORT_CONV, HYENA_IN_WIDTH), SHORT_CONV ** -0.5),
        'hy_conv_b': nrm((NH, HYENA_IN_WIDTH), 0.01),
        'hy_f_w1': nrm((NH, FILTER_EMB, FILTER_HIDDEN), FILTER_EMB ** -0.5),
        'hy_f_b1': nrm((NH, FILTER_HIDDEN), 0.1),
        'hy_f_freq1': 1.0 + nrm((NH, FILTER_HIDDEN), 0.02),
        'hy_f_w2': nrm((NH, FILTER_HIDDEN, FILTER_HIDDEN), FILTER_HIDDEN ** -0.5),
        'hy_f_b2': nrm((NH, FILTER_HIDDEN), 0.1),
        'hy_f_freq2': 1.0 + nrm((NH, FILTER_HIDDEN), 0.02),
        'hy_f_w3': nrm((NH, FILTER_HIDDEN, HYENA_ORDER * D), FILTER_HIDDEN ** -0.5),
        'hy_skip': nrm((NH, HYENA_ORDER, D), 0.5),
        'hy_w_out': nrm((NH, D, D), D ** -0.5),
        'hy_b_out': nrm((NH, D), 0.01),
        'mlp_w1': nrm((DEPTH, D, D_FF), D ** -0.5),
        'mlp_w2': nrm((DEPTH, D_FF, D), D_FF ** -0.5),
        'final_g': 1.0 + nrm((D,), 0.02),
    }


def reference(x, c, ctx, c_ctx, mod_w, mod_b, norm_mix_g, norm_mlp_g,
              attn_w_in, attn_w_out, attn_sink, attn_q_norm, attn_k_norm,
              hy_w_in, hy_b_in, hy_conv_w, hy_conv_b, hy_f_w1, hy_f_b1, hy_f_freq1,
              hy_f_w2, hy_f_b2, hy_f_freq2, hy_f_w3, hy_skip, hy_w_out, hy_b_out,
              mlp_w1, mlp_w2, final_g):
    n_lat = x.shape[1]
    n_ctx = ctx.shape[1]
    rope = axial_rope_tables(n_lat)
    last_ctx_layer = 2 * ((DEPTH - 1) // 2)
    c_act = jax.nn.silu(c)
    cc_act = jax.nn.silu(c_ctx)
    ctx_s = ctx
    for i in range(DEPTH):
        j = i // 2
        is_attn = (i % 2 == 0)
        ctx_updated = i < last_ctx_layer
        mod = (c_act @ mod_w[i] + mod_b[i])[:, None, :]
        sh1, sc1, g1, sh2, sc2, g2 = jnp.split(mod, N_MOD, axis=-1)
        h_lat = modulate(rms_norm(x, norm_mix_g[i]), sh1, sc1)
        if is_attn or ctx_updated:
            mod_c = cc_act @ mod_w[i] + mod_b[i]
            csh1, csc1, cg1, csh2, csc2, cg2 = jnp.split(mod_c, N_MOD, axis=-1)
            h_ctx = modulate(rms_norm(ctx_s, norm_mix_g[i]), csh1, csc1)
        if is_attn:
            y_lat, y_ctx = attention_mixer(h_lat, h_ctx, attn_w_in[j], attn_w_out[j], attn_sink[j],
                                           attn_q_norm[j], attn_k_norm[j], rope, ctx_updated)
        else:
            fparams = (hy_f_w1[j], hy_f_b1[j], hy_f_freq1[j], hy_f_w2[j], hy_f_b2[j], hy_f_freq2[j], hy_f_w3[j])
            y_lat = hyena_mixer(h_lat, hy_w_in[j], hy_b_in[j], hy_conv_w[j], hy_conv_b[j],
                                hyena_filters(n_lat, *fparams), hy_skip[j], hy_w_out[j], hy_b_out[j])
            if ctx_updated:
                y_ctx = hyena_mixer(h_ctx, hy_w_in[j], hy_b_in[j], hy_conv_w[j], hy_conv_b[j],
                                    hyena_filters(n_ctx, *fparams), hy_skip[j], hy_w_out[j], hy_b_out[j])
        x = x + g1 * y_lat
        x = x + g2 * sq_relu_mlp(modulate(rms_norm(x, norm_mlp_g[i]), sh2, sc2), mlp_w1[i], mlp_w2[i])
        if ctx_updated:
            ctx_s = ctx_s + cg1 * y_ctx
            ctx_s = ctx_s + cg2 * sq_relu_mlp(modulate(rms_norm(ctx_s, norm_mlp_g[i]), csh2, csc2), mlp_w1[i], mlp_w2[i])
    return rms_norm(x, final_g)
```

```python
import functools
import math

import numpy as np
import jax
import jax.numpy as jnp
from jax import lax
from jax.experimental import pallas as pl
from jax.experimental.pallas import tpu as pltpu

F32 = jnp.float32
BF16 = jnp.bfloat16

GRID_W = 64
HEAD_DIM = 128
A_Q_HEADS = 8
A_KV_HEADS = 2
B_Q_HEADS = 8
B_KV_HEADS = 2
GROUP = A_Q_HEADS // A_KV_HEADS
WINDOW = 128
ROPE_THETA = 10000.0
AXIS_DIM = HEAD_DIM // 2
Q_WIDTH = (A_Q_HEADS + B_Q_HEADS) * HEAD_DIM
KV_HEADS = A_KV_HEADS + B_KV_HEADS
KV_WIDTH = KV_HEADS * HEAD_DIM
HYENA_ORDER = 2
SHORT_CONV = 3
FILTER_EMB = 33
DECAY_TARGET = 1e-2
FAST_DECAY_PCT = 0.3
SLOW_DECAY_PCT = 1.5
MAX_DECAY = math.log(DECAY_TARGET) / FAST_DECAY_PCT
MIN_DECAY = math.log(DECAY_TARGET) / SLOW_DECAY_PCT
N_MOD = 6
EPS = 1e-6

VMEM_LIMIT_BYTES = 56 * 1024 * 1024
LANES = 128
SINGLE_STAGE_MAX_LEN = 512


def _params(*sem):
    return pltpu.CompilerParams(dimension_semantics=sem, vmem_limit_bytes=VMEM_LIMIT_BYTES)


def _tile(n, pref, align):
    if n <= pref:
        return n
    t = (pref // align) * align
    while t >= align:
        if n % t == 0:
            return t
        t -= align
    return n


def _mod_body(a_ref, w_ref, b_ref, o_ref):
    a = a_ref[...]
    act = a * (1.0 / (1.0 + jnp.exp(-a)))
    o_ref[...] = jnp.dot(act.astype(BF16), w_ref[...].astype(BF16),
                         preferred_element_type=F32) + b_ref[...]


def _modulation(cond, mod_w, mod_b):
    depth, d, n = mod_w.shape
    r = cond.shape[0]
    tn = _tile(n, 512, LANES)
    return pl.pallas_call(
        _mod_body,
        grid=(depth, n // tn),
        in_specs=[pl.BlockSpec((r, d), lambda l, j: (0, 0)),
                  pl.BlockSpec((None, d, tn), lambda l, j: (l, 0, j)),
                  pl.BlockSpec((None, 1, tn), lambda l, j: (l, 0, j))],
        out_specs=pl.BlockSpec((None, r, tn), lambda l, j: (l, 0, j)),
        out_shape=jax.ShapeDtypeStruct((depth, r, n), F32),
        compiler_params=_params("parallel", "parallel"),
        name="modulation",
    )(cond, mod_w, mod_b.reshape(depth, 1, n))


def _norm_body(x_ref, g_ref, *rest, modulated):
    x = x_ref[...]
    y = x * lax.rsqrt(jnp.mean(x * x, axis=-1, keepdims=True) + EPS) * g_ref[...]
    if modulated:
        sh_ref, sc_ref, o_ref = rest
        y = y * (1.0 + sc_ref[...]) + sh_ref[...]
    else:
        (o_ref,) = rest
    o_ref[...] = y.astype(o_ref.dtype)


def _rms_norm(x, g, shift=None, scale=None, out_dtype=None):
    t, d = x.shape
    out_dtype = BF16 if out_dtype is None else out_dtype
    tm = _tile(t, 512, 16)
    row = pl.BlockSpec((1, d), lambda i: (0, 0))
    vecs = [g.reshape(1, d)]
    if shift is not None:
        vecs += [shift.reshape(1, d), scale.reshape(1, d)]
    return pl.pallas_call(
        functools.partial(_norm_body, modulated=shift is not None),
        grid=(t // tm,),
        in_specs=[pl.BlockSpec((tm, d), lambda i: (i, 0))] + [row] * len(vecs),
        out_specs=pl.BlockSpec((tm, d), lambda i: (i, 0)),
        out_shape=jax.ShapeDtypeStruct((t, d), out_dtype),
        compiler_params=_params("parallel"),
        name="rms_norm",
    )(x, *vecs)


def _ep_none(acc):
    return acc


def _ep_bias(acc, b):
    return acc + b


def _ep_relu2(acc):
    return jnp.square(jnp.maximum(acc, 0.0))


def _ep_resid(acc, x, g):
    return x + g * acc


def _ep_resid_bias(acc, x, g, b):
    return x + g * (acc + b)


def _ep_gate(acc, x, v, skip):
    return x * (acc + v * skip)


def _mm_body(*refs, nk, epilogue, n_extra):
    a_ref, w_ref = refs[0], refs[1]
    extra = refs[2:2 + n_extra]
    o_ref = refs[2 + n_extra]
    a = a_ref[...].astype(BF16)
    w = w_ref[...].astype(BF16)
    part = jnp.dot(a, w, preferred_element_type=F32)
    if nk == 1:
        o_ref[...] = epilogue(part, *[e[...] for e in extra]).astype(o_ref.dtype)
        return
    acc_ref = refs[3 + n_extra]
    k = pl.program_id(2)

    @pl.when(k == 0)
    def _():
        acc_ref[...] = part

    @pl.when(k > 0)
    def _():
        acc_ref[...] += part

    @pl.when(k == nk - 1)
    def _():
        o_ref[...] = epilogue(acc_ref[...], *[e[...] for e in extra]).astype(o_ref.dtype)


def _matmul(a, w, *, out_dtype, epilogue=_ep_none, extras=(), tm=512, tn=1024, tk=2048,
            w_cols=None, name="matmul"):
    m, kdim = a.shape
    n = w.shape[1] if w_cols is None else w_cols[0]
    tm = _tile(m, tm, 16)
    tn = _tile(n, tn, LANES)
    tk = _tile(kdim, tk, LANES)
    nk = kdim // tk
    wmap = (lambda j: j) if w_cols is None else w_cols[1](tn)
    in_specs = [pl.BlockSpec((tm, tk), lambda i, j, k: (i, k)),
                pl.BlockSpec((tk, tn), lambda i, j, k: (k, wmap(j)))]
    in_specs += [fn(tm, tn) for _, fn in extras]
    return pl.pallas_call(
        functools.partial(_mm_body, nk=nk, epilogue=epilogue, n_extra=len(extras)),
        grid=(m // tm, n // tn, nk),
        in_specs=in_specs,
        out_specs=pl.BlockSpec((tm, tn), lambda i, j, k: (i, j)),
        out_shape=jax.ShapeDtypeStruct((m, n), out_dtype),
        scratch_shapes=[pltpu.VMEM((tm, tn), F32)] if nk > 1 else [],
        compiler_params=_params("parallel", "parallel", "arbitrary"),
        name=name,
    )(a, w, *[arr for arr, _ in extras])


def _row_extra(vec, period_blocks=None):
    vec = vec.reshape(1, -1)
    if period_blocks is None:
        return vec, lambda tm, tn: pl.BlockSpec((1, tn), lambda i, j, k: (0, j))
    return vec, lambda tm, tn: pl.BlockSpec((1, tn), lambda i, j, k: (0, j % period_blocks(tn)))


def _tile_extra(arr, colmap=None):
    cm = colmap if colmap is not None else (lambda tn: (lambda j: j))
    return arr, lambda tm, tn: pl.BlockSpec((tm, tn), lambda i, j, k: (i, cm(tn)(j)))


def _rope_tables(n_tok):
    rows = n_tok // GRID_W
    row = jnp.repeat(jnp.arange(rows, dtype=F32), GRID_W)
    col = jnp.tile(jnp.arange(GRID_W, dtype=F32), rows)
    inv = ROPE_THETA ** (-jnp.arange(0, AXIS_DIM, 2, dtype=F32) / AXIS_DIM)
    ang_r = row[:, None] * inv[None]
    ang_c = col[:, None] * inv[None]
    cos = jnp.concatenate([jnp.cos(ang_r)] * 2 + [jnp.cos(ang_c)] * 2, axis=-1)
    sin = jnp.concatenate([-jnp.sin(ang_r), jnp.sin(ang_r), -jnp.sin(ang_c), jnp.sin(ang_c)], axis=-1)
    return cos, sin


def _prep_body(*refs, rope):
    if rope:
        qkv_ref, qn_ref, kn_ref, cos_ref, sin_ref, q_ref, k_ref, v_ref = refs
        cos, sin = cos_ref[...], sin_ref[...]
        lane = lax.broadcasted_iota(jnp.int32, cos.shape, 1)
        first = (lane % (AXIS_DIM)) < (AXIS_DIM // 2)
    else:
        qkv_ref, qn_ref, kn_ref, q_ref, k_ref, v_ref = refs
    scale = HEAD_DIM ** -0.5
    quarter = AXIS_DIM // 2

    def rot(x):
        if not rope:
            return x
        swapped = jnp.where(first, pltpu.roll(x, HEAD_DIM - quarter, 1), pltpu.roll(x, quarter, 1))
        return x * cos + swapped * sin

    def nrm(x, g):
        return x * lax.rsqrt(jnp.mean(x * x, axis=-1, keepdims=True) + EPS) * g

    for h in range(A_Q_HEADS + B_Q_HEADS):
        sl = slice(h * HEAD_DIM, (h + 1) * HEAD_DIM)
        x = qkv_ref[:, sl]
        if h >= A_Q_HEADS:
            x = nrm(x, qn_ref[...])
        q_ref[:, sl] = (rot(x) * scale).astype(BF16)
    for h in range(KV_HEADS):
        x = qkv_ref[:, Q_WIDTH + h * HEAD_DIM:Q_WIDTH + (h + 1) * HEAD_DIM]
        if h >= A_KV_HEADS:
            x = nrm(x, kn_ref[...])
        k_ref[:, h * HEAD_DIM:(h + 1) * HEAD_DIM] = rot(x).astype(BF16)
    v_ref[...] = qkv_ref[:, Q_WIDTH + KV_WIDTH:].astype(BF16)


def _qkv_prep(qkv, q_norm, k_norm, rope):
    t = qkv.shape[0]
    tm = _tile(t, 256, 16)
    vec = pl.BlockSpec((1, HEAD_DIM), lambda i: (0, 0))
    in_specs = [pl.BlockSpec((tm, qkv.shape[1]), lambda i: (i, 0)), vec, vec]
    args = [qkv, q_norm.reshape(1, HEAD_DIM), k_norm.reshape(1, HEAD_DIM)]
    if rope is not None:
        in_specs += [pl.BlockSpec((tm, HEAD_DIM), lambda i: (i, 0))] * 2
        args += list(rope)
    return pl.pallas_call(
        functools.partial(_prep_body, rope=rope is not None),
        grid=(t // tm,),
        in_specs=in_specs,
        out_specs=[pl.BlockSpec((tm, Q_WIDTH), lambda i: (i, 0)),
                   pl.BlockSpec((tm, KV_WIDTH), lambda i: (i, 0)),
                   pl.BlockSpec((tm, KV_WIDTH), lambda i: (i, 0))],
        out_shape=[jax.ShapeDtypeStruct((t, Q_WIDTH), BF16),
                   jax.ShapeDtypeStruct((t, KV_WIDTH), BF16),
                   jax.ShapeDtypeStruct((t, KV_WIDTH), BF16)],
        compiler_params=_params("parallel"),
        name="qkv_prep",
    )(*args)


def _stack_heads(q_ref):
    return jnp.concatenate([q_ref[:, g * HEAD_DIM:(g + 1) * HEAD_DIM] for g in range(GROUP)], axis=0)


def _unstack_heads(o_ref, out, t):
    for g in range(GROUP):
        o_ref[:, g * HEAD_DIM:(g + 1) * HEAD_DIM] = out[g * t:(g + 1) * t].astype(o_ref.dtype)


def _sink_column(sink_ref, first, t):
    head = lax.broadcasted_iota(jnp.int32, (GROUP * t, 1), 0) // t
    col = jnp.full((GROUP * t, 1), sink_ref[first], F32)
    for g in range(1, GROUP):
        col = jnp.where(head == g, sink_ref[first + g], col)
    return col


def _qk(q, k):
    return lax.dot_general(q, k, (((1,), (1,)), ((), ())), preferred_element_type=F32)


def _gattn_body(q_ref, kc_ref, vtc_ref, k_ref, vt_ref, o_ref, m_ref, l_ref, acc_ref, *, tq, ck, n_chunks):
    q = _stack_heads(q_ref)
    m_ref[...] = jnp.full(m_ref.shape, -jnp.inf, F32)
    l_ref[...] = jnp.zeros(l_ref.shape, F32)
    acc_ref[...] = jnp.zeros(acc_ref.shape, F32)

    def step(kb, vtb):
        st = _qk(kb, q)
        m_old = m_ref[...]
        m_new = jnp.maximum(m_old, jnp.max(st, axis=0, keepdims=True))
        alpha = jnp.exp(m_old - m_new)
        p = jnp.exp(st - m_new)
        l_ref[...] = alpha * l_ref[...] + jnp.sum(p, axis=0, keepdims=True)
        acc_ref[...] = alpha * acc_ref[...] + jnp.dot(vtb, p.astype(BF16), preferred_element_type=F32)
        m_ref[...] = m_new

    step(kc_ref[...], vtc_ref[...])

    def body(c, carry):
        off = pl.multiple_of(c * ck, ck)
        step(k_ref[pl.ds(off, ck), :], vt_ref[:, pl.ds(off, ck)])
        return carry

    lax.fori_loop(0, n_chunks, body, 0)
    out_t = acc_ref[...] / l_ref[...]
    _unstack_heads(o_ref, out_t.T, tq)


def _global_attention(q, k, vt, k_ctx, vt_ctx):
    s, n_ctx = q.shape[0], k_ctx.shape[0]
    tq = _tile(s, 256, 16)
    ck = _tile(s, 512, LANES)
    gw = GROUP * HEAD_DIM
    rows = GROUP * tq
    kspec = lambda n: pl.BlockSpec((n, HEAD_DIM), lambda h, i: (0, A_KV_HEADS + h))
    vspec = lambda n: pl.BlockSpec((HEAD_DIM, n), lambda h, i: (h, 0))
    return pl.pallas_call(
        functools.partial(_gattn_body, tq=tq, ck=ck, n_chunks=s // ck),
        grid=(B_KV_HEADS, s // tq),
        in_specs=[pl.BlockSpec((tq, gw), lambda h, i: (i, A_KV_HEADS + h)),
                  kspec(n_ctx), vspec(n_ctx), kspec(s), vspec(s)],
        out_specs=pl.BlockSpec((tq, gw), lambda h, i: (i, h)),
        out_shape=jax.ShapeDtypeStruct((s, B_Q_HEADS * HEAD_DIM), BF16),
        scratch_shapes=[pltpu.VMEM((1, rows), F32), pltpu.VMEM((1, rows), F32),
                        pltpu.VMEM((HEAD_DIM, rows), F32)],
        compiler_params=_params("parallel", "parallel"),
        name="global_attention",
    )(q, k_ctx, vt_ctx, k, vt)


def _project_t_body(w_ref, h_ref, o_ref):
    o_ref[...] = _qk(w_ref[...].astype(BF16), h_ref[...]).astype(o_ref.dtype)


def _project_t(w_t, h):
    c, d = w_t.shape
    t = h.shape[0]
    tt = _tile(t, 1024, LANES)
    return pl.pallas_call(
        _project_t_body,
        grid=(t // tt,),
        in_specs=[pl.BlockSpec((c, d), lambda i: (0, 0)), pl.BlockSpec((tt, d), lambda i: (i, 0))],
        out_specs=pl.BlockSpec((c, tt), lambda i: (0, i)),
        out_shape=jax.ShapeDtypeStruct((c, t), BF16),
        compiler_params=_params("parallel"),
        name="project_t",
    )(w_t, h)


def _wattn_body(sink_ref, q_ref, kc_ref, vc_ref, kp_ref, k0_ref, kn_ref, vp_ref, v0_ref, vn_ref, o_ref,
                *, nb, n_ctx):
    h = pl.program_id(0)
    i = pl.program_id(1)
    w = WINDOW
    q = _stack_heads(q_ref)
    kcat = jnp.concatenate([kc_ref[...], kp_ref[...], k0_ref[...], kn_ref[...]], axis=0)
    vcat = jnp.concatenate([vc_ref[...], vp_ref[...], v0_ref[...], vn_ref[...]], axis=0)
    s = _qk(q, kcat)
    r = lax.broadcasted_iota(jnp.int32, s.shape, 0) % w
    c = lax.broadcasted_iota(jnp.int32, s.shape, 1) - n_ctx
    lo = jnp.maximum(r, jnp.where(i >= 1, 0, w))
    hi = jnp.minimum(r + 2 * w, jnp.where(i + 1 < nb, 3 * w - 1, 2 * w - 1))
    valid = (c < 0) | ((c >= lo) & (c <= hi))
    s = jnp.where(valid, s, -jnp.inf)
    sink = _sink_column(sink_ref, h * GROUP, w)
    m = jnp.maximum(jnp.max(s, axis=-1, keepdims=True), sink)
    p = jnp.exp(s - m)
    denom = jnp.sum(p, axis=-1, keepdims=True) + jnp.exp(sink - m)
    out = jnp.dot(p.astype(BF16), vcat, preferred_element_type=F32) / denom
    _unstack_heads(o_ref, out, w)


def _window_attention(q, k, v, k_ctx, v_ctx, sink):
    s, n_ctx = q.shape[0], k_ctx.shape[0]
    w = WINDOW
    nb = s // w
    gw = GROUP * HEAD_DIM
    blk = lambda fn: pl.BlockSpec((w, HEAD_DIM), fn)
    prev = blk(lambda h, i: (jnp.maximum(i - 1, 0), h))
    cur = blk(lambda h, i: (i, h))
    nxt = blk(lambda h, i: (jnp.minimum(i + 1, nb - 1), h))
    ctx = pl.BlockSpec((n_ctx, HEAD_DIM), lambda h, i: (0, h))
    return pl.pallas_call(
        functools.partial(_wattn_body, nb=nb, n_ctx=n_ctx),
        grid=(A_KV_HEADS, nb),
        in_specs=[pl.BlockSpec(memory_space=pltpu.SMEM),
                  pl.BlockSpec((w, gw), lambda h, i: (i, h)),
                  ctx, ctx, prev, cur, nxt, prev, cur, nxt],
        out_specs=pl.BlockSpec((w, gw), lambda h, i: (i, h)),
        out_shape=jax.ShapeDtypeStruct((s, A_Q_HEADS * HEAD_DIM), BF16),
        compiler_params=_params("parallel", "parallel"),
        name="window_attention",
    )(sink, q, k_ctx, v_ctx, k, k, k, v, v, v)


def _cattn_body(sink_ref, q_ref, k_ref, v_ref, o_ref, *, n_ctx):
    h = pl.program_id(0)
    q = _stack_heads(q_ref)
    s = _qk(q, k_ref[...])
    sink = _sink_column(sink_ref, h * GROUP, n_ctx)
    m = jnp.maximum(jnp.max(s, axis=-1, keepdims=True), sink)
    p = jnp.exp(s - m)
    denom = jnp.sum(p, axis=-1, keepdims=True) + jnp.exp(sink - m)
    out = jnp.dot(p.astype(BF16), v_ref[...], preferred_element_type=F32) / denom
    _unstack_heads(o_ref, out, n_ctx)


def _context_attention(q, k, v, sink):
    n_ctx = q.shape[0]
    gw = GROUP * HEAD_DIM
    sink_all = jnp.concatenate([sink.astype(F32), jnp.full((B_Q_HEADS,), -jnp.inf, F32)])
    kv = pl.BlockSpec((n_ctx, HEAD_DIM), lambda h: (0, h))
    return pl.pallas_call(
        functools.partial(_cattn_body, n_ctx=n_ctx),
        grid=(KV_HEADS,),
        in_specs=[pl.BlockSpec(memory_space=pltpu.SMEM),
                  pl.BlockSpec((n_ctx, gw), lambda h: (0, h)), kv, kv],
        out_specs=pl.BlockSpec((n_ctx, gw), lambda h: (0, h)),
        out_shape=jax.ShapeDtypeStruct((n_ctx, Q_WIDTH), BF16),
        compiler_params=_params("parallel"),
        name="context_attention",
    )(sink_all, q, k, v)


def _short_conv_body(u_ref, up_ref, un_ref, w_ref, b_ref, o_ref, *, n_row_tiles):
    i = pl.program_id(0)
    u = u_ref[...]
    tm = u.shape[0]
    halo = up_ref.shape[0]
    row = lax.broadcasted_iota(jnp.int32, u.shape, 0)
    before = jnp.where(i > 0, up_ref[halo - 1:halo, :], 0.0)
    after = jnp.where(i + 1 < n_row_tiles, un_ref[0:1, :], 0.0)
    left = jnp.where(row == 0, before, pltpu.roll(u, 1, 0))
    right = jnp.where(row == tm - 1, after, pltpu.roll(u, tm - 1, 0))
    o_ref[...] = left * w_ref[0:1, :] + u * w_ref[1:2, :] + right * w_ref[2:3, :] + b_ref[...]


def _short_conv(u, w, b):
    n, c = u.shape
    halo = 8
    tm = _tile(n, 256, halo)
    tc = _tile(c, 2048, LANES)
    hb = tm // halo
    n_row_tiles = n // tm
    return pl.pallas_call(
        functools.partial(_short_conv_body, n_row_tiles=n_row_tiles),
        grid=(n_row_tiles, c // tc),
        in_specs=[pl.BlockSpec((tm, tc), lambda i, j: (i, j)),
                  pl.BlockSpec((halo, tc), lambda i, j: (jnp.maximum(i * hb - 1, 0), j)),
                  pl.BlockSpec((halo, tc), lambda i, j: (jnp.minimum((i + 1) * hb, n // halo - 1), j)),
                  pl.BlockSpec((SHORT_CONV, tc), lambda i, j: (0, j)),
                  pl.BlockSpec((1, tc), lambda i, j: (0, j))],
        out_specs=pl.BlockSpec((tm, tc), lambda i, j: (i, j)),
        out_shape=jax.ShapeDtypeStruct((n, c), F32),
        compiler_params=_params("parallel", "parallel"),
        name="short_conv",
    )(u, u, u, w, b.reshape(1, c))


def _filter_body(feat_ref, w1_ref, b1_ref, f1_ref, w2_ref, b2_ref, f2_ref, w3_ref, dl_ref,
                 h_ref, ss_ref, *, n_tok):
    i = pl.program_id(0)
    tl = feat_ref.shape[0]
    mm = lambda a, b: jnp.dot(a.astype(BF16), b.astype(BF16), preferred_element_type=F32)
    hid = jnp.sin(f1_ref[...] * (mm(feat_ref[...], w1_ref[...]) + b1_ref[...]))
    hid = jnp.sin(f2_ref[...] * (mm(hid, w2_ref[...]) + b2_ref[...]))
    h = mm(hid, w3_ref[...])
    t = i * tl + lax.broadcasted_iota(jnp.int32, (tl, 1), 0)
    offs = jnp.abs(t - n_tok // 2).astype(F32) * (2.0 / n_tok)
    h = h * jnp.exp(-offs * dl_ref[...])
    h_ref[...] = h

    @pl.when(i == 0)
    def _():
        ss_ref[...] = jnp.zeros(ss_ref.shape, F32)

    ss_ref[...] += jnp.sum(h * h, axis=0, keepdims=True)


def _hyena_filters(n_tok, d, w1, b1, fr1, w2, b2, fr2, w3):
    bands = (FILTER_EMB - 1) // 2
    t = jnp.linspace(0.0, 1.0, n_tok, dtype=F32)[:, None]
    wv = 2.0 * math.pi * jnp.arange(n_tok, dtype=F32)[:, None] / n_tok
    f = jnp.linspace(1e-4, bands - 1, bands, dtype=F32)[None]
    feats = jnp.concatenate([t, jnp.cos(f * wv), -jnp.sin(f * wv)], axis=-1)
    deltas = jnp.abs(jnp.linspace(MIN_DECAY, MAX_DECAY, d, dtype=F32))
    hidden = w1.shape[1]
    pad_e, pad_h = LANES - FILTER_EMB, LANES - hidden
    padv = lambda v: jnp.pad(v.reshape(1, hidden), ((0, 0), (0, pad_h)))
    c = HYENA_ORDER * d
    tl = _tile(n_tok, 256, 8)
    full = lambda shape: pl.BlockSpec(shape, lambda i: (0, 0))
    return pl.pallas_call(
        functools.partial(_filter_body, n_tok=n_tok),
        grid=(n_tok // tl,),
        in_specs=[pl.BlockSpec((tl, LANES), lambda i: (i, 0)),
                  full((LANES, LANES)), full((1, LANES)), full((1, LANES)),
                  full((LANES, LANES)), full((1, LANES)), full((1, LANES)),
                  full((LANES, c)), full((1, c))],
        out_specs=[pl.BlockSpec((tl, c), lambda i: (i, 0)), full((1, c))],
        out_shape=[jax.ShapeDtypeStruct((n_tok, c), F32), jax.ShapeDtypeStruct((1, c), F32)],
        compiler_params=_params("arbitrary"),
        name="hyena_filter",
    )(jnp.pad(feats, ((0, 0), (0, pad_e))),
      jnp.pad(w1, ((0, pad_e), (0, pad_h))), padv(b1), padv(fr1),
      jnp.pad(w2, ((0, pad_h), (0, pad_h))), padv(b2), padv(fr2),
      jnp.pad(w3, ((0, pad_h), (0, 0))), jnp.tile(deltas, HYENA_ORDER).reshape(1, c))


def _lane_table(vals):
    return jnp.broadcast_to(jnp.asarray(vals)[..., None], vals.shape + (LANES,))


SUBLANES = 8
DFT_ROW_GROUP = 16


@functools.lru_cache(maxsize=None)
def _dft_plan(n_tok):
    n = 2 * n_tok
    log = n.bit_length() - 1
    assert 1 << log == n, "sequence length must be a power of two"
    n2 = 1 << (log // 2)
    n1 = n // n2
    assert n1 % 4 == 0 and n2 % DFT_ROW_GROUP == 0
    eye = np.eye(SUBLANES)
    a1 = 2.0 * np.pi * np.outer(np.arange(n1), np.arange(n1 // 2)) / n1
    f1 = np.kron(np.concatenate([np.cos(a1), -np.sin(a1)], axis=0), eye)
    a2 = 2.0 * np.pi * np.outer(np.arange(n2), np.arange(n2)) / n2
    c2, s2 = np.cos(a2), np.sin(a2)
    g2 = np.block([[c2, s2], [-s2, c2]])
    g2i = np.block([[c2, -s2], [s2, c2]])
    rows = np.arange(n1 // 4, n1 // 4 + n1 // 2)
    a1i = 2.0 * np.pi * np.outer(rows, np.arange(n1)) / n1
    f1i = np.kron(np.concatenate([np.cos(a1i), -np.sin(a1i)], axis=1) / n, eye)
    tw = 2.0 * np.pi * np.outer(np.arange(n2), np.arange(n1)) / n
    tw1 = tw.reshape(n2 // SUBLANES, SUBLANES, n1).transpose(0, 2, 1).reshape(n2 // SUBLANES, n1 * SUBLANES)
    f32 = lambda m: m.astype(np.float32)
    return dict(n1=n1, n2=n2, f1=f32(f1), g2=f32(g2), g2i=f32(g2i), f1i=f32(f1i),
                tw1_cos=f32(np.cos(tw1)), tw1_sin=f32(np.sin(tw1)),
                tw2_cos=f32(np.cos(tw.T)), tw2_sin=f32(np.sin(tw.T)))


def _first_step():
    return (pl.program_id(0) == 0) & (pl.program_id(1) == 0)


def _stage1_body(f_ref, z_ref, c_ref, s_ref, o_ref, fb_ref):
    n1h, g, tc = z_ref.shape
    n1 = 2 * n1h
    half = n1 * SUBLANES
    reps = tc // LANES

    @pl.when(_first_step())
    def _():
        fb_ref[...] = f_ref[...].astype(BF16)

    z = z_ref[...]
    re_parts, im_parts = [], []
    for a in range(g // SUBLANES):
        zz = z[:, a * SUBLANES:(a + 1) * SUBLANES, :].reshape(n1h * SUBLANES, tc)
        acc = jnp.dot(fb_ref[...], zz.astype(BF16), preferred_element_type=F32)
        re, im = acc[:half], acc[half:]
        c = jnp.tile(c_ref[a], (1, reps))
        s = jnp.tile(s_ref[a], (1, reps))
        re_parts.append((re * c + im * s).reshape(n1, SUBLANES, tc))
        im_parts.append((im * c - re * s).reshape(n1, SUBLANES, tc))
    o_ref[0] = jnp.concatenate(re_parts, axis=1).astype(o_ref.dtype)
    o_ref[1] = jnp.concatenate(im_parts, axis=1).astype(o_ref.dtype)


def _dft_stage1(plan, src, width, part=0):
    n1, n2 = plan["n1"], plan["n2"]
    g = DFT_ROW_GROUP
    tc = _tile(width, 512, LANES)
    nc = width // tc
    sub = g // SUBLANES
    tw = lambda t: _lane_table(t).reshape(n2 // g, sub, n1 * SUBLANES, LANES)
    tw_spec = pl.BlockSpec((None, sub, n1 * SUBLANES, LANES), lambda i, j: (i, 0, 0, 0))
    f = jnp.asarray(plan["f1"])
    return pl.pallas_call(
        _stage1_body,
        grid=(n2 // g, nc),
        in_specs=[pl.BlockSpec(f.shape, lambda i, j: (0, 0)),
                  pl.BlockSpec((n1 // 2, g, tc), lambda i, j: (0, i, part * nc + j)),
                  tw_spec, tw_spec],
        out_specs=pl.BlockSpec((2, n1, g, tc), lambda i, j: (0, 0, i, j)),
        out_shape=jax.ShapeDtypeStruct((2, n1, n2, width), BF16),
        scratch_shapes=[pltpu.VMEM(f.shape, BF16)],
        compiler_params=_params("arbitrary", "arbitrary"),
        name="dft_stage1",
    )(f, src.reshape(n1 // 2, n2, src.shape[1]), tw(plan["tw1_cos"]), tw(plan["tw1_sin"]))


def _istage1_body(f_ref, q_ref, x_ref, v_ref, skip_ref, o_ref, fb_ref):
    _, n1, g, tc = q_ref.shape

    @pl.when(_first_step())
    def _():
        fb_ref[...] = f_ref[...].astype(BF16)

    q = q_ref[...].astype(F32)
    parts = []
    for a in range(g // SUBLANES):
        qq = q[:, :, a * SUBLANES:(a + 1) * SUBLANES, :].reshape(2 * n1 * SUBLANES, tc)
        y = jnp.dot(fb_ref[...], qq.astype(BF16), preferred_element_type=F32)
        parts.append(y.reshape(n1 // 2, SUBLANES, tc))
    y = jnp.concatenate(parts, axis=1)
    o_ref[...] = (x_ref[...] * (y + v_ref[...] * skip_ref[...])).astype(o_ref.dtype)


def _idft_gate(plan, q, x_src, x_part, v_src, v_part, skip, out_dtype):
    n1, n2 = plan["n1"], plan["n2"]
    d = q.shape[3]
    g = DFT_ROW_GROUP
    tc = _tile(d, 512, LANES)
    nc = d // tc
    f = jnp.asarray(plan["f1i"])
    tok = lambda part: pl.BlockSpec((n1 // 2, g, tc), lambda i, j: (0, i, part * nc + j))
    view = lambda arr: arr.reshape(n1 // 2, n2, arr.shape[1])
    out = pl.pallas_call(
        _istage1_body,
        grid=(n2 // g, nc),
        in_specs=[pl.BlockSpec(f.shape, lambda i, j: (0, 0)),
                  pl.BlockSpec((2, n1, g, tc), lambda i, j: (0, 0, i, j)),
                  tok(x_part), tok(v_part), pl.BlockSpec((1, tc), lambda i, j: (0, j))],
        out_specs=tok(0),
        out_shape=jax.ShapeDtypeStruct((n1 // 2, n2, d), out_dtype),
        scratch_shapes=[pltpu.VMEM(f.shape, BF16)],
        compiler_params=_params("arbitrary", "arbitrary"),
        name="idft_gate",
    )(f, q, view(x_src), view(v_src), skip.reshape(1, d))
    return out.reshape(n1 // 2 * n2, d)


def _spectral_body(a_ref, ah_ref, ss_ref, g2_ref, g2i_ref, c_ref, s_ref, o_ref):
    n2 = a_ref.shape[1]
    g2 = g2_ref[...].astype(BF16)
    stack = lambda ref: jnp.concatenate([ref[0], ref[1]], axis=0)
    x = jnp.dot(g2, stack(a_ref), preferred_element_type=F32)
    hf = jnp.dot(g2, stack(ah_ref), preferred_element_type=F32)
    xr, xi, hr, hi = x[:n2], x[n2:], hf[:n2], hf[n2:]
    scale = lax.rsqrt(ss_ref[...] + EPS)
    y = jnp.concatenate([(xr * hr - xi * hi) * scale, (xr * hi + xi * hr) * scale], axis=0)
    p = jnp.dot(g2i_ref[...].astype(BF16), y.astype(BF16), preferred_element_type=F32)
    pr, pi = p[:n2], p[n2:]
    reps = pr.shape[1] // LANES
    c = jnp.tile(c_ref[...], (1, reps))
    s = jnp.tile(s_ref[...], (1, reps))
    o_ref[0] = (pr * c - pi * s).astype(o_ref.dtype)
    o_ref[1] = (pr * s + pi * c).astype(o_ref.dtype)


def _spectral_conv(plan, a, ah, ss, order):
    n1, n2, d = a.shape[1:]
    td = _tile(d, 2048, LANES)
    nd = d // td
    blk = lambda off: pl.BlockSpec((2, None, n2, td), lambda k1, j: (0, k1, 0, off + j))
    const = pl.BlockSpec((2 * n2, 2 * n2), lambda k1, j: (0, 0))
    tw = pl.BlockSpec((None, n2, LANES), lambda k1, j: (k1, 0, 0))
    return pl.pallas_call(
        _spectral_body,
        grid=(n1, nd),
        in_specs=[blk(0), blk(order * nd), pl.BlockSpec((1, td), lambda k1, j: (0, order * nd + j)),
                  const, const, tw, tw],
        out_specs=blk(0),
        out_shape=jax.ShapeDtypeStruct((2, n1, n2, d), BF16),
        compiler_params=_params("parallel", "parallel"),
        name="spectral_conv",
    )(a, ah, ss, jnp.asarray(plan["g2"]), jnp.asarray(plan["g2i"]),
      _lane_table(plan["tw2_cos"]), _lane_table(plan["tw2_sin"]))


@functools.lru_cache(maxsize=None)
def _dft_plan_single(n_tok):
    n = 2 * n_tok
    af = 2.0 * np.pi * np.outer(np.arange(n), np.arange(n_tok)) / n
    fwd = np.concatenate([np.cos(af), -np.sin(af)], axis=0)
    ai = 2.0 * np.pi * np.outer(np.arange(n_tok // 2, n_tok // 2 + n_tok), np.arange(n)) / n
    inv = np.concatenate([np.cos(ai), -np.sin(ai)], axis=1) / n
    return dict(n=n, fwd=fwd.astype(np.float32), inv=inv.astype(np.float32))


def _cmul_body(x_ref, h_ref, ss_ref, o_ref):
    n = x_ref.shape[0] // 2
    xr, xi, hr, hi = x_ref[:n], x_ref[n:], h_ref[:n], h_ref[n:]
    scale = lax.rsqrt(ss_ref[...] + EPS)
    o_ref[:n] = ((xr * hr - xi * hi) * scale).astype(o_ref.dtype)
    o_ref[n:] = ((xr * hi + xi * hr) * scale).astype(o_ref.dtype)


def _cmul(x, h, ss, d, order):
    rows = x.shape[0]
    td = _tile(d, 512, LANES)
    nd = d // td
    return pl.pallas_call(
        _cmul_body,
        grid=(nd,),
        in_specs=[pl.BlockSpec((rows, td), lambda j: (0, j)),
                  pl.BlockSpec((rows, td), lambda j: (0, order * nd + j)),
                  pl.BlockSpec((1, td), lambda j: (0, order * nd + j))],
        out_specs=pl.BlockSpec((rows, td), lambda j: (0, j)),
        out_shape=jax.ShapeDtypeStruct((rows, d), BF16),
        compiler_params=_params("parallel"),
        name="spectrum_product",
    )(x, h, ss)


def _hyena_mixer(h, w_in, b_in, conv_w, conv_b, fparams, skip):
    n_tok, d = h.shape
    u = _matmul(h, w_in, out_dtype=F32, epilogue=_ep_bias, extras=(_row_extra(b_in),), name="hyena_in")
    u = _short_conv(u, conv_w, conv_b)
    filt, ss = _hyena_filters(n_tok, d, *fparams)

    if n_tok <= SINGLE_STAGE_MAX_LEN:
        plan = _dft_plan_single(n_tok)
        fwd, inv = jnp.asarray(plan["fwd"]), jnp.asarray(plan["inv"])
        part_cols = lambda p: (lambda tn: (lambda j: p * (d // tn) + j))
        hf = _matmul(fwd, filt, out_dtype=F32, tm=2 * plan["n"], tn=512, name="dft_filter")
        z_src, z_cols = u, part_cols(HYENA_ORDER)
        for order in range(HYENA_ORDER):
            x = _matmul(fwd, z_src, out_dtype=F32, tm=2 * plan["n"], tn=512, w_cols=(d, z_cols),
                        name="dft_signal")
            y = _cmul(x, hf, ss, d, order)
            last = order == HYENA_ORDER - 1
            z_src = _matmul(inv, y, out_dtype=BF16 if last else F32, epilogue=_ep_gate, tn=512,
                            extras=(_tile_extra(u, part_cols(order)), _tile_extra(z_src, z_cols),
                                    _row_extra(skip[order])), name="idft_gate")
            z_cols = part_cols(0)
        return z_src

    plan = _dft_plan(n_tok)
    ah = _dft_stage1(plan, filt, HYENA_ORDER * d)
    z_src, z_part = u, HYENA_ORDER
    for order in range(HYENA_ORDER):
        a = _dft_stage1(plan, z_src, d, z_part)
        q = _spectral_conv(plan, a, ah, ss, order)
        last = order == HYENA_ORDER - 1
        z_src = _idft_gate(plan, q, u, order, z_src, z_part, skip[order], BF16 if last else F32)
        z_part = 0
    return z_src


def kernel(x, c, ctx, c_ctx, mod_w, mod_b, norm_mix_g, norm_mlp_g, attn_w_in, attn_w_out, attn_sink, attn_q_norm, attn_k_norm, hy_w_in, hy_b_in, hy_conv_w, hy_conv_b, hy_f_w1, hy_f_b1, hy_f_freq1, hy_f_w2, hy_f_b2, hy_f_freq2, hy_f_w3, hy_skip, hy_w_out, hy_b_out, mlp_w1, mlp_w2, final_g):
    batch, n_lat, d = x.shape
    n_ctx = ctx.shape[1]
    depth = mod_w.shape[0]
    assert batch == 1 and n_lat % GRID_W == 0 and n_lat % WINDOW == 0
    last_ctx_layer = 2 * ((depth - 1) // 2)
    rope = _rope_tables(n_lat)
    bf = lambda w: w.astype(BF16)

    cond = jnp.zeros((16, d), F32).at[0].set(c[0]).at[1].set(c_ctx)
    mod = _modulation(cond, mod_w, mod_b)

    xs = x[0]
    cs = ctx[0]
    for i in range(depth):
        j = i // 2
        is_attn = i % 2 == 0
        ctx_updated = i < last_ctx_layer
        sh1, sc1, g1, sh2, sc2, g2 = jnp.split(mod[i, 0:1], N_MOD, axis=-1)
        csh1, csc1, cg1, csh2, csc2, cg2 = jnp.split(mod[i, 1:2], N_MOD, axis=-1)
        h_lat = _rms_norm(xs, norm_mix_g[i], sh1, sc1)
        if is_attn or ctx_updated:
            h_ctx = _rms_norm(cs, norm_mix_g[i], csh1, csc1)
        if is_attn:
            w_in, w_out = bf(attn_w_in[j]), bf(attn_w_out[j])
            qkv = _matmul(h_lat, w_in, out_dtype=F32, name="attn_in")
            q, k, v = _qkv_prep(qkv, attn_q_norm[j], attn_k_norm[j], rope)
            qkv_c = _matmul(h_ctx, w_in, out_dtype=F32, name="attn_in_ctx")
            q_c, k_c, v_c = _qkv_prep(qkv_c, attn_q_norm[j], attn_k_norm[j], None)
            o_a = _window_attention(q, k, v, k_c, v_c, attn_sink[j])
            wv_t = attn_w_in[j][:, Q_WIDTH + KV_WIDTH + A_KV_HEADS * HEAD_DIM:].T
            o_b = _global_attention(q, k, _project_t(wv_t, h_lat), k_c, _project_t(wv_t, h_ctx))
            o = jnp.concatenate([o_a, o_b], axis=1)
            xs = _matmul(o, w_out, out_dtype=F32, epilogue=_ep_resid,
                         extras=(_tile_extra(xs), _row_extra(g1)), name="attn_out")
            if ctx_updated:
                o_c = _context_attention(q_c, k_c, v_c, attn_sink[j])
                cs = _matmul(o_c, w_out, out_dtype=F32, epilogue=_ep_resid,
                             extras=(_tile_extra(cs), _row_extra(cg1)), name="attn_out_ctx")
        else:
            w_in, w_out = bf(hy_w_in[j]), bf(hy_w_out[j])
            fparams = (hy_f_w1[j], hy_f_b1[j], hy_f_freq1[j], hy_f_w2[j], hy_f_b2[j], hy_f_freq2[j], hy_f_w3[j])
            z = _hyena_mixer(h_lat, w_in, hy_b_in[j], hy_conv_w[j], hy_conv_b[j], fparams, hy_skip[j])
            xs = _matmul(z, w_out, out_dtype=F32, epilogue=_ep_resid_bias,
                         extras=(_tile_extra(xs), _row_extra(g1), _row_extra(hy_b_out[j])), name="hyena_out")
            if ctx_updated:
                z_c = _hyena_mixer(h_ctx, w_in, hy_b_in[j], hy_conv_w[j], hy_conv_b[j], fparams, hy_skip[j])
                cs = _matmul(z_c, w_out, out_dtype=F32, epilogue=_ep_resid_bias,
                             extras=(_tile_extra(cs), _row_extra(cg1), _row_extra(hy_b_out[j])),
                             name="hyena_out_ctx")
        w1, w2 = bf(mlp_w1[i]), bf(mlp_w2[i])
        h2 = _rms_norm(xs, norm_mlp_g[i], sh2, sc2)
        a1 = _matmul(h2, w1, out_dtype=BF16, epilogue=_ep_relu2, name="mlp_up")
        xs = _matmul(a1, w2, out_dtype=F32, epilogue=_ep_resid,
                     extras=(_tile_extra(xs), _row_extra(g2)), name="mlp_down")
        if ctx_updated:
            h2c = _rms_norm(cs, norm_mlp_g[i], csh2, csc2)
            a1c = _matmul(h2c, w1, out_dtype=BF16, epilogue=_ep_relu2, name="mlp_up_ctx")
            cs = _matmul(a1c, w2, out_dtype=F32, epilogue=_ep_resid,
                         extras=(_tile_extra(cs), _row_extra(cg2)), name="mlp_down_ctx")
    return _rms_norm(xs, final_g, out_dtype=F32)[None]
```

```python
import functools
import math

import numpy as np
import jax
import jax.numpy as jnp
from jax import lax
from jax.experimental import pallas as pl
from jax.experimental.pallas import tpu as pltpu

F32 = jnp.float32
BF16 = jnp.bfloat16

GRID_W = 64
HEAD_DIM = 128
A_Q_HEADS = 8
A_KV_HEADS = 2
B_Q_HEADS = 8
B_KV_HEADS = 2
GROUP = A_Q_HEADS // A_KV_HEADS
WINDOW = 128
ROPE_THETA = 10000.0
AXIS_DIM = HEAD_DIM // 2
Q_WIDTH = (A_Q_HEADS + B_Q_HEADS) * HEAD_DIM
KV_HEADS = A_KV_HEADS + B_KV_HEADS
KV_WIDTH = KV_HEADS * HEAD_DIM
HYENA_ORDER = 2
SHORT_CONV = 3
FILTER_EMB = 33
DECAY_TARGET = 1e-2
FAST_DECAY_PCT = 0.3
SLOW_DECAY_PCT = 1.5
MAX_DECAY = math.log(DECAY_TARGET) / FAST_DECAY_PCT
MIN_DECAY = math.log(DECAY_TARGET) / SLOW_DECAY_PCT
N_MOD = 6
EPS = 1e-6
LOG2_E = math.log2(math.e)
LN_2 = math.log(2.0)

VMEM_LIMIT_BYTES = 56 * 1024 * 1024
LANES = 128
SINGLE_STAGE_MAX_LEN = 512
BIG_TM = 1024


def _params(*sem):
    return pltpu.CompilerParams(dimension_semantics=sem, vmem_limit_bytes=VMEM_LIMIT_BYTES)


def _tile(n, pref, align):
    if n <= pref:
        return n
    t = (pref // align) * align
    while t >= align:
        if n % t == 0:
            return t
        t -= align
    return n


def _mod_body(a_ref, w_ref, b_ref, o_ref):
    a = a_ref[...]
    act = a * (1.0 / (1.0 + jnp.exp(-a)))
    o_ref[...] = jnp.dot(act.astype(BF16), w_ref[...].astype(BF16),
                         preferred_element_type=F32) + b_ref[...]


def _modulation(cond, mod_w, mod_b):
    depth, d, n = mod_w.shape
    r = cond.shape[0]
    tn = _tile(n, 512, LANES)
    return pl.pallas_call(
        _mod_body,
        grid=(depth, n // tn),
        in_specs=[pl.BlockSpec((r, d), lambda l, j: (0, 0)),
                  pl.BlockSpec((None, d, tn), lambda l, j: (l, 0, j)),
                  pl.BlockSpec((None, 1, tn), lambda l, j: (l, 0, j))],
        out_specs=pl.BlockSpec((None, r, tn), lambda l, j: (l, 0, j)),
        out_shape=jax.ShapeDtypeStruct((depth, r, n), F32),
        compiler_params=_params("parallel", "parallel"),
        name="modulation",
    )(cond, mod_w, mod_b.reshape(depth, 1, n))


def _norm_body(x_ref, g_ref, *rest, modulated):
    x = x_ref[...]
    y = x * lax.rsqrt(jnp.mean(x * x, axis=-1, keepdims=True) + EPS) * g_ref[...]
    if modulated:
        sh_ref, sc_ref, o_ref = rest
        y = y * (1.0 + sc_ref[...]) + sh_ref[...]
    else:
        (o_ref,) = rest
    o_ref[...] = y.astype(o_ref.dtype)


def _rms_norm(x, g, shift=None, scale=None, out_dtype=None):
    t, d = x.shape
    out_dtype = BF16 if out_dtype is None else out_dtype
    tm = _tile(t, 512, 16)
    row = pl.BlockSpec((1, d), lambda i: (0, 0))
    vecs = [g.reshape(1, d)]
    if shift is not None:
        vecs += [shift.reshape(1, d), scale.reshape(1, d)]
    return pl.pallas_call(
        functools.partial(_norm_body, modulated=shift is not None),
        grid=(t // tm,),
        in_specs=[pl.BlockSpec((tm, d), lambda i: (i, 0))] + [row] * len(vecs),
        out_specs=pl.BlockSpec((tm, d), lambda i: (i, 0)),
        out_shape=jax.ShapeDtypeStruct((t, d), out_dtype),
        compiler_params=_params("parallel"),
        name="rms_norm",
    )(x, *vecs)


def _ep_none(acc):
    return acc


def _ep_bias(acc, b):
    return acc + b


def _ep_relu2(acc):
    return jnp.square(jnp.maximum(acc, 0.0))


def _ep_resid(acc, x, g):
    return x + g * acc


def _ep_resid_bias(acc, x, g, b):
    return x + g * (acc + b)


def _ep_gate(acc, x, v, skip):
    return x * (acc + v * skip)


def _mm_body(*refs, nk, epilogue, n_extra):
    a_ref, w_ref = refs[0], refs[1]
    extra = refs[2:2 + n_extra]
    o_ref = refs[2 + n_extra]
    a = a_ref[...].astype(BF16)
    w = w_ref[...].astype(BF16)
    part = jnp.dot(a, w, preferred_element_type=F32)
    if nk == 1:
        o_ref[...] = epilogue(part, *[e[...] for e in extra]).astype(o_ref.dtype)
        return
    acc_ref = refs[3 + n_extra]
    k = pl.program_id(2)

    @pl.when(k == 0)
    def _():
        acc_ref[...] = part

    @pl.when(k > 0)
    def _():
        acc_ref[...] += part

    @pl.when(k == nk - 1)
    def _():
        o_ref[...] = epilogue(acc_ref[...], *[e[...] for e in extra]).astype(o_ref.dtype)


def _matmul(a, w, *, out_dtype, epilogue=_ep_none, extras=(), tm=512, tn=1024, tk=2048,
            w_cols=None, name="matmul"):
    m, kdim = a.shape
    n = w.shape[1] if w_cols is None else w_cols[0]
    tm = _tile(m, tm, 16)
    tn = _tile(n, tn, LANES)
    tk = _tile(kdim, tk, LANES)
    nk = kdim // tk
    wmap = (lambda j: j) if w_cols is None else w_cols[1](tn)
    in_specs = [pl.BlockSpec((tm, tk), lambda i, j, k: (i, k)),
                pl.BlockSpec((tk, tn), lambda i, j, k: (k, wmap(j)))]
    in_specs += [fn(tm, tn) for _, fn in extras]
    return pl.pallas_call(
        functools.partial(_mm_body, nk=nk, epilogue=epilogue, n_extra=len(extras)),
        grid=(m // tm, n // tn, nk),
        in_specs=in_specs,
        out_specs=pl.BlockSpec((tm, tn), lambda i, j, k: (i, j)),
        out_shape=jax.ShapeDtypeStruct((m, n), out_dtype),
        scratch_shapes=[pltpu.VMEM((tm, tn), F32)] if nk > 1 else [],
        compiler_params=_params("parallel", "parallel", "arbitrary"),
        name=name,
    )(a, w, *[arr for arr, _ in extras])


def _nmm_body(x_ref, g_ref, sh_ref, sc_ref, w_ref, *rest, epilogue, n_extra):
    extra = rest[:n_extra]
    o_ref, h_ref = rest[n_extra], rest[n_extra + 1]

    @pl.when(pl.program_id(1) == 0)
    def _():
        x = x_ref[...]
        y = x * lax.rsqrt(jnp.mean(x * x, axis=-1, keepdims=True) + EPS) * g_ref[...]
        h_ref[...] = (y * (1.0 + sc_ref[...]) + sh_ref[...]).astype(h_ref.dtype)

    acc = jnp.dot(h_ref[...], w_ref[...].astype(BF16), preferred_element_type=F32)
    o_ref[...] = epilogue(acc, *[e[...] for e in extra]).astype(o_ref.dtype)


def _norm_matmul(x, g, shift, scale, w, *, out_dtype, epilogue=_ep_none, extras=(), tm=1024, tn=1024,
                 name="norm_matmul"):
    m, d = x.shape
    n = w.shape[1]
    tm = _tile(m, tm, 16)
    tn = _tile(n, tn, LANES)
    vec = pl.BlockSpec((1, d), lambda i, j: (0, 0))
    with_k = lambda spec_fn: (lambda bs: pl.BlockSpec(bs.block_shape, lambda i, j: bs.index_map(i, j, 0)))(
        spec_fn(tm, tn))
    return pl.pallas_call(
        functools.partial(_nmm_body, epilogue=epilogue, n_extra=len(extras)),
        grid=(m // tm, n // tn),
        in_specs=[pl.BlockSpec((tm, d), lambda i, j: (i, 0)), vec, vec, vec,
                  pl.BlockSpec((d, tn), lambda i, j: (0, j))] + [with_k(fn) for _, fn in extras],
        out_specs=pl.BlockSpec((tm, tn), lambda i, j: (i, j)),
        out_shape=jax.ShapeDtypeStruct((m, n), out_dtype),
        scratch_shapes=[pltpu.VMEM((tm, d), BF16)],
        compiler_params=_params("arbitrary", "arbitrary"),
        name=name,
    )(x, g.reshape(1, d), shift.reshape(1, d), scale.reshape(1, d), w, *[arr for arr, _ in extras])


def _row_extra(vec, period_blocks=None):
    vec = vec.reshape(1, -1)
    if period_blocks is None:
        return vec, lambda tm, tn: pl.BlockSpec((1, tn), lambda i, j, k: (0, j))
    return vec, lambda tm, tn: pl.BlockSpec((1, tn), lambda i, j, k: (0, j % period_blocks(tn)))


def _tile_extra(arr, colmap=None):
    cm = colmap if colmap is not None else (lambda tn: (lambda j: j))
    return arr, lambda tm, tn: pl.BlockSpec((tm, tn), lambda i, j, k: (i, cm(tn)(j)))


def _rope_tables(n_tok):
    rows = n_tok // GRID_W
    row = jnp.repeat(jnp.arange(rows, dtype=F32), GRID_W)
    col = jnp.tile(jnp.arange(GRID_W, dtype=F32), rows)
    inv = ROPE_THETA ** (-jnp.arange(0, AXIS_DIM, 2, dtype=F32) / AXIS_DIM)
    ang_r = row[:, None] * inv[None]
    ang_c = col[:, None] * inv[None]
    cos = jnp.concatenate([jnp.cos(ang_r)] * 2 + [jnp.cos(ang_c)] * 2, axis=-1)
    sin = jnp.concatenate([-jnp.sin(ang_r), jnp.sin(ang_r), -jnp.sin(ang_c), jnp.sin(ang_c)], axis=-1)
    return cos, sin


def _prep_body(*refs, rope):
    if rope:
        qkv_ref, qn_ref, kn_ref, cos_ref, sin_ref, q_ref, k_ref, v_ref = refs
        cos, sin = cos_ref[...], sin_ref[...]
        lane = lax.broadcasted_iota(jnp.int32, cos.shape, 1)
        first = (lane % (AXIS_DIM)) < (AXIS_DIM // 2)
    else:
        qkv_ref, qn_ref, kn_ref, q_ref, k_ref, v_ref = refs
    scale = HEAD_DIM ** -0.5
    quarter = AXIS_DIM // 2

    def rot(x):
        if not rope:
            return x
        swapped = jnp.where(first, pltpu.roll(x, HEAD_DIM - quarter, 1), pltpu.roll(x, quarter, 1))
        return x * cos + swapped * sin

    def nrm(x, g):
        return x * lax.rsqrt(jnp.mean(x * x, axis=-1, keepdims=True) + EPS) * g

    for h in range(A_Q_HEADS + B_Q_HEADS):
        sl = slice(h * HEAD_DIM, (h + 1) * HEAD_DIM)
        x = qkv_ref[:, sl]
        if h >= A_Q_HEADS:
            x = nrm(x, qn_ref[...])
        q_ref[:, sl] = (rot(x) * (scale * LOG2_E if h >= A_Q_HEADS else scale)).astype(BF16)
    for h in range(KV_HEADS):
        x = qkv_ref[:, Q_WIDTH + h * HEAD_DIM:Q_WIDTH + (h + 1) * HEAD_DIM]
        if h >= A_KV_HEADS:
            x = nrm(x, kn_ref[...])
        k_ref[:, h * HEAD_DIM:(h + 1) * HEAD_DIM] = rot(x).astype(BF16)
    v_ref[...] = qkv_ref[:, Q_WIDTH + KV_WIDTH:].astype(BF16)


def _qkv_prep(qkv, q_norm, k_norm, rope):
    t = qkv.shape[0]
    tm = _tile(t, 256, 16)
    vec = pl.BlockSpec((1, HEAD_DIM), lambda i: (0, 0))
    in_specs = [pl.BlockSpec((tm, qkv.shape[1]), lambda i: (i, 0)), vec, vec]
    args = [qkv, q_norm.reshape(1, HEAD_DIM), k_norm.reshape(1, HEAD_DIM)]
    if rope is not None:
        in_specs += [pl.BlockSpec((tm, HEAD_DIM), lambda i: (i, 0))] * 2
        args += list(rope)
    return pl.pallas_call(
        functools.partial(_prep_body, rope=rope is not None),
        grid=(t // tm,),
        in_specs=in_specs,
        out_specs=[pl.BlockSpec((tm, Q_WIDTH), lambda i: (i, 0)),
                   pl.BlockSpec((tm, KV_WIDTH), lambda i: (i, 0)),
                   pl.BlockSpec((tm, KV_WIDTH), lambda i: (i, 0))],
        out_shape=[jax.ShapeDtypeStruct((t, Q_WIDTH), BF16),
                   jax.ShapeDtypeStruct((t, KV_WIDTH), BF16),
                   jax.ShapeDtypeStruct((t, KV_WIDTH), BF16)],
        compiler_params=_params("parallel"),
        name="qkv_prep",
    )(*args)


def _stack_heads(q_ref):
    return jnp.concatenate([q_ref[:, g * HEAD_DIM:(g + 1) * HEAD_DIM] for g in range(GROUP)], axis=0)


def _unstack_heads(o_ref, out, t):
    for g in range(GROUP):
        o_ref[:, g * HEAD_DIM:(g + 1) * HEAD_DIM] = out[g * t:(g + 1) * t].astype(o_ref.dtype)


def _sink_column(sink_ref, first, t):
    head = lax.broadcasted_iota(jnp.int32, (GROUP * t, 1), 0) // t
    col = jnp.full((GROUP * t, 1), sink_ref[first], F32)
    for g in range(1, GROUP):
        col = jnp.where(head == g, sink_ref[first + g], col)
    return col


def _qk(q, k):
    return lax.dot_general(q, k, (((1,), (1,)), ((), ())), preferred_element_type=F32)


ONES_ROWS = 16
STREAMS = 1


def _gattn_body(q_ref, kc_ref, vtc_ref, k_ref, vt_ref, o_prev_ref, o_ref, *scratch, tq, ck, n_chunks):
    del o_prev_ref
    m_refs, acc_refs = scratch[:STREAMS], scratch[STREAMS:]
    per = GROUP // STREAMS
    qs = [jnp.concatenate([q_ref[:, g * HEAD_DIM:(g + 1) * HEAD_DIM] for g in range(s * per, (s + 1) * per)],
                          axis=0) for s in range(STREAMS)]
    for s in range(STREAMS):
        m_refs[s][...] = jnp.full(m_refs[s].shape, -jnp.inf, F32)
        acc_refs[s][...] = jnp.zeros(acc_refs[s].shape, F32)

    def step(kb, vtb):
        lhs = jnp.concatenate([vtb, jnp.ones((ONES_ROWS, vtb.shape[1]), BF16)], axis=0)
        for s in range(STREAMS):
            st = _qk(kb, qs[s])
            m_old = m_refs[s][...]
            m_new = jnp.maximum(m_old, jnp.max(st, axis=0, keepdims=True))
            p = jnp.exp2(st - m_new).astype(BF16)
            acc_refs[s][...] = (jnp.exp2(m_old - m_new) * acc_refs[s][...]
                                + jnp.dot(lhs, p, preferred_element_type=F32))
            m_refs[s][...] = m_new

    step(kc_ref[...], vtc_ref[...])

    def body(c, carry):
        off = pl.multiple_of(c * ck, ck)
        step(k_ref[pl.ds(off, ck), :], vt_ref[:, pl.ds(off, ck)])
        return carry

    lax.fori_loop(0, n_chunks, body, 0)
    for s in range(STREAMS):
        out = (acc_refs[s][:HEAD_DIM] / acc_refs[s][HEAD_DIM:HEAD_DIM + 1]).T
        for g in range(per):
            col = (s * per + g) * HEAD_DIM
            o_ref[:, col:col + HEAD_DIM] = out[g * tq:(g + 1) * tq].astype(o_ref.dtype)


def _global_attention(q, k, vt, k_ctx, vt_ctx, o_prev):
    s, n_ctx = q.shape[0], k_ctx.shape[0]
    tq = _tile(s, 512, LANES)
    ck = _tile(s, 2048, LANES)
    gw = GROUP * HEAD_DIM
    kspec = lambda n: pl.BlockSpec((n, HEAD_DIM), lambda h, i: (0, A_KV_HEADS + h))
    vspec = lambda n: pl.BlockSpec((HEAD_DIM, n), lambda h, i: (h, 0))
    return pl.pallas_call(
        functools.partial(_gattn_body, tq=tq, ck=ck, n_chunks=s // ck),
        grid=(B_KV_HEADS, s // tq),
        in_specs=[pl.BlockSpec((tq, gw), lambda h, i: (i, A_KV_HEADS + h)),
                  kspec(n_ctx), vspec(n_ctx), kspec(s), vspec(s), pl.BlockSpec(memory_space=pl.ANY)],
        out_specs=pl.BlockSpec((tq, gw), lambda h, i: (i, A_KV_HEADS + h)),
        out_shape=jax.ShapeDtypeStruct(o_prev.shape, o_prev.dtype),
        input_output_aliases={5: 0},
        scratch_shapes=([pltpu.VMEM((1, GROUP // STREAMS * tq), F32)] * STREAMS
                        + [pltpu.VMEM((HEAD_DIM + ONES_ROWS, GROUP // STREAMS * tq), F32)] * STREAMS),
        compiler_params=_params("parallel", "parallel"),
        name="global_attention",
    )(q, k_ctx, vt_ctx, k, vt, o_prev)


def _project_t_body(w_ref, h_ref, o_ref):
    o_ref[...] = _qk(w_ref[...].astype(BF16), h_ref[...]).astype(o_ref.dtype)


def _project_t(w_t, h):
    c, d = w_t.shape
    t = h.shape[0]
    tt = _tile(t, 1024, LANES)
    return pl.pallas_call(
        _project_t_body,
        grid=(t // tt,),
        in_specs=[pl.BlockSpec((c, d), lambda i: (0, 0)), pl.BlockSpec((tt, d), lambda i: (i, 0))],
        out_specs=pl.BlockSpec((c, tt), lambda i: (0, i)),
        out_shape=jax.ShapeDtypeStruct((c, t), BF16),
        compiler_params=_params("parallel"),
        name="project_t",
    )(w_t, h)


def _wattn_body(sink_ref, q_ref, kc_ref, vc_ref, kp_ref, k0_ref, kn_ref, vp_ref, v0_ref, vn_ref, o_ref,
                *, nb, n_ctx):
    h = pl.program_id(0)
    i = pl.program_id(1)
    w = WINDOW
    q = _stack_heads(q_ref)
    kcat = jnp.concatenate([kc_ref[...], kp_ref[...], k0_ref[...], kn_ref[...]], axis=0)
    vcat = jnp.concatenate([vc_ref[...], vp_ref[...], v0_ref[...], vn_ref[...]], axis=0)
    s = _qk(q, kcat)
    r = lax.broadcasted_iota(jnp.int32, s.shape, 0) % w
    c = lax.broadcasted_iota(jnp.int32, s.shape, 1) - n_ctx
    lo = jnp.maximum(r, jnp.where(i >= 1, 0, w))
    hi = jnp.minimum(r + 2 * w, jnp.where(i + 1 < nb, 3 * w - 1, 2 * w - 1))
    valid = (c < 0) | ((c >= lo) & (c <= hi))
    s = jnp.where(valid, s, -jnp.inf)
    sink = _sink_column(sink_ref, h * GROUP, w)
    m = jnp.maximum(jnp.max(s, axis=-1, keepdims=True), sink)
    p = jnp.exp(s - m)
    denom = jnp.sum(p, axis=-1, keepdims=True) + jnp.exp(sink - m)
    out = jnp.dot(p.astype(BF16), vcat, preferred_element_type=F32) / denom
    _unstack_heads(o_ref, out, w)


def _window_attention(q, k, v, k_ctx, v_ctx, sink):
    s, n_ctx = q.shape[0], k_ctx.shape[0]
    w = WINDOW
    nb = s // w
    gw = GROUP * HEAD_DIM
    blk = lambda fn: pl.BlockSpec((w, HEAD_DIM), fn)
    prev = blk(lambda h, i: (jnp.maximum(i - 1, 0), h))
    cur = blk(lambda h, i: (i, h))
    nxt = blk(lambda h, i: (jnp.minimum(i + 1, nb - 1), h))
    ctx = pl.BlockSpec((n_ctx, HEAD_DIM), lambda h, i: (0, h))
    return pl.pallas_call(
        functools.partial(_wattn_body, nb=nb, n_ctx=n_ctx),
        grid=(A_KV_HEADS, nb),
        in_specs=[pl.BlockSpec(memory_space=pltpu.SMEM),
                  pl.BlockSpec((w, gw), lambda h, i: (i, h)),
                  ctx, ctx, prev, cur, nxt, prev, cur, nxt],
        out_specs=pl.BlockSpec((w, gw), lambda h, i: (i, h)),
        out_shape=jax.ShapeDtypeStruct((s, Q_WIDTH), BF16),
        compiler_params=_params("parallel", "parallel"),
        name="window_attention",
    )(sink, q, k_ctx, v_ctx, k, k, k, v, v, v)


def _cattn_body(sink_ref, q_ref, k_ref, v_ref, o_ref, *, n_ctx):
    h = pl.program_id(0)
    q = _stack_heads(q_ref)
    s = _qk(q, k_ref[...]) * jnp.where(h >= A_KV_HEADS, LN_2, 1.0)
    sink = _sink_column(sink_ref, h * GROUP, n_ctx)
    m = jnp.maximum(jnp.max(s, axis=-1, keepdims=True), sink)
    p = jnp.exp(s - m)
    denom = jnp.sum(p, axis=-1, keepdims=True) + jnp.exp(sink - m)
    out = jnp.dot(p.astype(BF16), v_ref[...], preferred_element_type=F32) / denom
    _unstack_heads(o_ref, out, n_ctx)


def _context_attention(q, k, v, sink):
    n_ctx = q.shape[0]
    gw = GROUP * HEAD_DIM
    sink_all = jnp.concatenate([sink.astype(F32), jnp.full((B_Q_HEADS,), -jnp.inf, F32)])
    kv = pl.BlockSpec((n_ctx, HEAD_DIM), lambda h: (0, h))
    return pl.pallas_call(
        functools.partial(_cattn_body, n_ctx=n_ctx),
        grid=(KV_HEADS,),
        in_specs=[pl.BlockSpec(memory_space=pltpu.SMEM),
                  pl.BlockSpec((n_ctx, gw), lambda h: (0, h)), kv, kv],
        out_specs=pl.BlockSpec((n_ctx, gw), lambda h: (0, h)),
        out_shape=jax.ShapeDtypeStruct((n_ctx, Q_WIDTH), BF16),
        compiler_params=_params("parallel"),
        name="context_attention",
    )(sink_all, q, k, v)


def _short_conv_body(u_ref, up_ref, un_ref, w_ref, b_ref, o_ref, *, n_row_tiles):
    i = pl.program_id(0)
    u = u_ref[...]
    tm = u.shape[0]
    halo = up_ref.shape[0]
    row = lax.broadcasted_iota(jnp.int32, u.shape, 0)
    before = jnp.where(i > 0, up_ref[halo - 1:halo, :], 0.0)
    after = jnp.where(i + 1 < n_row_tiles, un_ref[0:1, :], 0.0)
    left = jnp.where(row == 0, before, pltpu.roll(u, 1, 0))
    right = jnp.where(row == tm - 1, after, pltpu.roll(u, tm - 1, 0))
    o_ref[...] = left * w_ref[0:1, :] + u * w_ref[1:2, :] + right * w_ref[2:3, :] + b_ref[...]


def _short_conv(u, w, b):
    n, c = u.shape
    halo = 8
    tm = _tile(n, 256, halo)
    tc = _tile(c, 2048, LANES)
    hb = tm // halo
    n_row_tiles = n // tm
    return pl.pallas_call(
        functools.partial(_short_conv_body, n_row_tiles=n_row_tiles),
        grid=(n_row_tiles, c // tc),
        in_specs=[pl.BlockSpec((tm, tc), lambda i, j: (i, j)),
                  pl.BlockSpec((halo, tc), lambda i, j: (jnp.maximum(i * hb - 1, 0), j)),
                  pl.BlockSpec((halo, tc), lambda i, j: (jnp.minimum((i + 1) * hb, n // halo - 1), j)),
                  pl.BlockSpec((SHORT_CONV, tc), lambda i, j: (0, j)),
                  pl.BlockSpec((1, tc), lambda i, j: (0, j))],
        out_specs=pl.BlockSpec((tm, tc), lambda i, j: (i, j)),
        out_shape=jax.ShapeDtypeStruct((n, c), F32),
        compiler_params=_params("parallel", "parallel"),
        name="short_conv",
    )(u, u, u, w, b.reshape(1, c))


def _filter_body(feat_ref, w1_ref, b1_ref, f1_ref, w2_ref, b2_ref, f2_ref, w3_ref, dl_ref,
                 h_ref, ss_ref, *, n_tok):
    i = pl.program_id(0)
    tl = feat_ref.shape[0]
    mm = lambda a, b: jnp.dot(a.astype(BF16), b.astype(BF16), preferred_element_type=F32)
    hid = jnp.sin(f1_ref[...] * (mm(feat_ref[...], w1_ref[...]) + b1_ref[...]))
    hid = jnp.sin(f2_ref[...] * (mm(hid, w2_ref[...]) + b2_ref[...]))
    h = mm(hid, w3_ref[...])
    t = i * tl + lax.broadcasted_iota(jnp.int32, (tl, 1), 0)
    offs = jnp.abs(t - n_tok // 2).astype(F32) * (2.0 / n_tok)
    h = h * jnp.exp(-offs * dl_ref[...])
    h_ref[...] = h

    @pl.when(i == 0)
    def _():
        ss_ref[...] = jnp.zeros(ss_ref.shape, F32)

    ss_ref[...] += jnp.sum(h * h, axis=0, keepdims=True)


def _hyena_filters(n_tok, d, w1, b1, fr1, w2, b2, fr2, w3):
    bands = (FILTER_EMB - 1) // 2
    t = jnp.linspace(0.0, 1.0, n_tok, dtype=F32)[:, None]
    wv = 2.0 * math.pi * jnp.arange(n_tok, dtype=F32)[:, None] / n_tok
    f = jnp.linspace(1e-4, bands - 1, bands, dtype=F32)[None]
    feats = jnp.concatenate([t, jnp.cos(f * wv), -jnp.sin(f * wv)], axis=-1)
    deltas = jnp.abs(jnp.linspace(MIN_DECAY, MAX_DECAY, d, dtype=F32))
    hidden = w1.shape[1]
    pad_e, pad_h = LANES - FILTER_EMB, LANES - hidden
    padv = lambda v: jnp.pad(v.reshape(1, hidden), ((0, 0), (0, pad_h)))
    c = HYENA_ORDER * d
    tl = _tile(n_tok, 256, 8)
    full = lambda shape: pl.BlockSpec(shape, lambda i: (0, 0))
    return pl.pallas_call(
        functools.partial(_filter_body, n_tok=n_tok),
        grid=(n_tok // tl,),
        in_specs=[pl.BlockSpec((tl, LANES), lambda i: (i, 0)),
                  full((LANES, LANES)), full((1, LANES)), full((1, LANES)),
                  full((LANES, LANES)), full((1, LANES)), full((1, LANES)),
                  full((LANES, c)), full((1, c))],
        out_specs=[pl.BlockSpec((tl, c), lambda i: (i, 0)), full((1, c))],
        out_shape=[jax.ShapeDtypeStruct((n_tok, c), F32), jax.ShapeDtypeStruct((1, c), F32)],
        compiler_params=_params("arbitrary"),
        name="hyena_filter",
    )(jnp.pad(feats, ((0, 0), (0, pad_e))),
      jnp.pad(w1, ((0, pad_e), (0, pad_h))), padv(b1), padv(fr1),
      jnp.pad(w2, ((0, pad_h), (0, pad_h))), padv(b2), padv(fr2),
      jnp.pad(w3, ((0, pad_h), (0, 0))), jnp.tile(deltas, HYENA_ORDER).reshape(1, c))


def _lane_table(vals):
    return jnp.broadcast_to(jnp.asarray(vals)[..., None], vals.shape + (LANES,))


SUBLANES = 8
DFT_ROW_GROUP = 16


@functools.lru_cache(maxsize=None)
def _dft_plan(n_tok):
    n = 2 * n_tok
    log = n.bit_length() - 1
    assert 1 << log == n, "sequence length must be a power of two"
    n2 = 1 << (log // 2)
    n1 = n // n2
    assert n1 % 4 == 0 and n2 % DFT_ROW_GROUP == 0
    eye = np.eye(SUBLANES)
    nk1 = n1 // 2 + 1
    nk1_pad = -(-nk1 // SUBLANES) * SUBLANES
    a1 = 2.0 * np.pi * np.outer(np.arange(nk1), np.arange(n1 // 2)) / n1
    f1 = np.kron(np.concatenate([np.cos(a1), -np.sin(a1)], axis=0), eye)
    a2 = 2.0 * np.pi * np.outer(np.arange(n2), np.arange(n2)) / n2
    c2, s2 = np.cos(a2), np.sin(a2)
    g2 = np.block([[c2, s2], [-s2, c2]])
    g2i = np.block([[c2, -s2], [s2, c2]])
    rows = np.arange(n1 // 4, n1 // 4 + n1 // 2)
    a1i = 2.0 * np.pi * np.outer(rows, np.arange(nk1_pad)) / n1
    weight = np.where((np.arange(nk1_pad) == 0) | (np.arange(nk1_pad) == n1 // 2), 1.0, 2.0)
    weight = np.where(np.arange(nk1_pad) < nk1, weight, 0.0) / n
    f1i = np.kron(np.concatenate([np.cos(a1i) * weight, -np.sin(a1i) * weight], axis=1), eye)
    tw = 2.0 * np.pi * np.outer(np.arange(n2), np.arange(nk1)) / n
    tw1 = tw.reshape(n2 // SUBLANES, SUBLANES, nk1).transpose(0, 2, 1).reshape(n2 // SUBLANES, nk1 * SUBLANES)
    f32 = lambda m: m.astype(np.float32)
    return dict(n1=n1, n2=n2, nk1=nk1, nk1_pad=nk1_pad, f1=f32(f1), g2=f32(g2), g2i=f32(g2i), f1i=f32(f1i),
                tw1_cos=f32(np.cos(tw1)), tw1_sin=f32(np.sin(tw1)),
                tw2_cos=f32(np.cos(tw.T)), tw2_sin=f32(np.sin(tw.T)))


def _first_step():
    return (pl.program_id(0) == 0) & (pl.program_id(1) == 0)


def _stage1_body(f_ref, z_ref, c_ref, s_ref, o_ref, fb_ref):
    n1h, g, tc = z_ref.shape
    nk1 = o_ref.shape[1]
    half = nk1 * SUBLANES
    reps = tc // LANES

    @pl.when(_first_step())
    def _():
        fb_ref[...] = f_ref[...].astype(BF16)

    z = z_ref[...]
    re_parts, im_parts = [], []
    for a in range(g // SUBLANES):
        zz = z[:, a * SUBLANES:(a + 1) * SUBLANES, :].reshape(n1h * SUBLANES, tc)
        acc = jnp.dot(fb_ref[...], zz.astype(BF16), preferred_element_type=F32)
        re, im = acc[:half], acc[half:]
        c = jnp.tile(c_ref[a], (1, reps))
        s = jnp.tile(s_ref[a], (1, reps))
        re_parts.append((re * c + im * s).reshape(nk1, SUBLANES, tc))
        im_parts.append((im * c - re * s).reshape(nk1, SUBLANES, tc))
    o_ref[0] = jnp.concatenate(re_parts, axis=1).astype(o_ref.dtype)
    o_ref[1] = jnp.concatenate(im_parts, axis=1).astype(o_ref.dtype)


def _dft_stage1(plan, src, width, part=0):
    n1, n2, nk1 = plan["n1"], plan["n2"], plan["nk1"]
    g = DFT_ROW_GROUP
    tc = _tile(width, 512, LANES)
    nc = width // tc
    sub = g // SUBLANES
    tw = lambda t: _lane_table(t).reshape(n2 // g, sub, nk1 * SUBLANES, LANES)
    tw_spec = pl.BlockSpec((None, sub, nk1 * SUBLANES, LANES), lambda i, j: (i, 0, 0, 0))
    f = jnp.asarray(plan["f1"])
    return pl.pallas_call(
        _stage1_body,
        grid=(n2 // g, nc),
        in_specs=[pl.BlockSpec(f.shape, lambda i, j: (0, 0)),
                  pl.BlockSpec((n1 // 2, g, tc), lambda i, j: (0, i, part * nc + j)),
                  tw_spec, tw_spec],
        out_specs=pl.BlockSpec((2, nk1, g, tc), lambda i, j: (0, 0, i, j)),
        out_shape=jax.ShapeDtypeStruct((2, nk1, n2, width), BF16),
        scratch_shapes=[pltpu.VMEM(f.shape, BF16)],
        compiler_params=_params("arbitrary", "arbitrary"),
        name="dft_stage1",
    )(f, src.reshape(n1 // 2, n2, src.shape[1]), tw(plan["tw1_cos"]), tw(plan["tw1_sin"]))


def _istage1_body(f_ref, q_ref, x_ref, v_ref, skip_ref, o_ref, fb_ref):
    _, nk1, g, tc = q_ref.shape
    n1h = o_ref.shape[0]
    nk1_pad = f_ref.shape[1] // (2 * SUBLANES)

    @pl.when(_first_step())
    def _():
        fb_ref[...] = f_ref[...].astype(BF16)

    q = q_ref[...].astype(F32)
    q = jnp.concatenate([q, jnp.zeros((2, nk1_pad - nk1, g, tc), F32)], axis=1)
    parts = []
    for a in range(g // SUBLANES):
        qq = q[:, :, a * SUBLANES:(a + 1) * SUBLANES, :].reshape(2 * nk1_pad * SUBLANES, tc)
        y = jnp.dot(fb_ref[...], qq.astype(BF16), preferred_element_type=F32)
        parts.append(y.reshape(n1h, SUBLANES, tc))
    y = jnp.concatenate(parts, axis=1)
    o_ref[...] = (x_ref[...] * (y + v_ref[...] * skip_ref[...])).astype(o_ref.dtype)


def _idft_gate(plan, q, x_src, x_part, v_src, v_part, skip, out_dtype):
    n1, n2, nk1 = plan["n1"], plan["n2"], plan["nk1"]
    d = q.shape[3]
    g = DFT_ROW_GROUP
    tc = _tile(d, 512, LANES)
    nc = d // tc
    f = jnp.asarray(plan["f1i"])
    tok = lambda part: pl.BlockSpec((n1 // 2, g, tc), lambda i, j: (0, i, part * nc + j))
    view = lambda arr: arr.reshape(n1 // 2, n2, arr.shape[1])
    out = pl.pallas_call(
        _istage1_body,
        grid=(n2 // g, nc),
        in_specs=[pl.BlockSpec(f.shape, lambda i, j: (0, 0)),
                  pl.BlockSpec((2, nk1, g, tc), lambda i, j: (0, 0, i, j)),
                  tok(x_part), tok(v_part), pl.BlockSpec((1, tc), lambda i, j: (0, j))],
        out_specs=tok(0),
        out_shape=jax.ShapeDtypeStruct((n1 // 2, n2, d), out_dtype),
        scratch_shapes=[pltpu.VMEM(f.shape, BF16)],
        compiler_params=_params("arbitrary", "arbitrary"),
        name="idft_gate",
    )(f, q, view(x_src), view(v_src), skip.reshape(1, d))
    return out.reshape(n1 // 2 * n2, d)


def _spectral_body(a_ref, ah_ref, ss_ref, g2_ref, g2i_ref, c_ref, s_ref, o_ref):
    n2 = a_ref.shape[1]
    g2 = g2_ref[...].astype(BF16)
    stack = lambda ref: jnp.concatenate([ref[0], ref[1]], axis=0)
    x = jnp.dot(g2, stack(a_ref), preferred_element_type=F32)
    hf = jnp.dot(g2, stack(ah_ref), preferred_element_type=F32)
    xr, xi, hr, hi = x[:n2], x[n2:], hf[:n2], hf[n2:]
    scale = lax.rsqrt(ss_ref[...] + EPS)
    y = jnp.concatenate([(xr * hr - xi * hi) * scale, (xr * hi + xi * hr) * scale], axis=0)
    p = jnp.dot(g2i_ref[...].astype(BF16), y.astype(BF16), preferred_element_type=F32)
    pr, pi = p[:n2], p[n2:]
    reps = pr.shape[1] // LANES
    c = jnp.tile(c_ref[...], (1, reps))
    s = jnp.tile(s_ref[...], (1, reps))
    o_ref[0] = (pr * c - pi * s).astype(o_ref.dtype)
    o_ref[1] = (pr * s + pi * c).astype(o_ref.dtype)


def _spectral_conv(plan, a, ah, ss, order):
    n1, n2, d = a.shape[1:]
    td = _tile(d, 2048, LANES)
    nd = d // td
    blk = lambda off: pl.BlockSpec((2, None, n2, td), lambda k1, j: (0, k1, 0, off + j))
    const = pl.BlockSpec((2 * n2, 2 * n2), lambda k1, j: (0, 0))
    tw = pl.BlockSpec((None, n2, LANES), lambda k1, j: (k1, 0, 0))
    return pl.pallas_call(
        _spectral_body,
        grid=(n1, nd),
        in_specs=[blk(0), blk(order * nd), pl.BlockSpec((1, td), lambda k1, j: (0, order * nd + j)),
                  const, const, tw, tw],
        out_specs=blk(0),
        out_shape=jax.ShapeDtypeStruct((2, n1, n2, d), BF16),
        compiler_params=_params("parallel", "parallel"),
        name="spectral_conv",
    )(a, ah, ss, jnp.asarray(plan["g2"]), jnp.asarray(plan["g2i"]),
      _lane_table(plan["tw2_cos"]), _lane_table(plan["tw2_sin"]))


@functools.lru_cache(maxsize=None)
def _dft_plan_single(n_tok):
    n = 2 * n_tok
    af = 2.0 * np.pi * np.outer(np.arange(n), np.arange(n_tok)) / n
    fwd = np.concatenate([np.cos(af), -np.sin(af)], axis=0)
    ai = 2.0 * np.pi * np.outer(np.arange(n_tok // 2, n_tok // 2 + n_tok), np.arange(n)) / n
    inv = np.concatenate([np.cos(ai), -np.sin(ai)], axis=1) / n
    return dict(n=n, fwd=fwd.astype(np.float32), inv=inv.astype(np.float32))


def _cmul_body(x_ref, h_ref, ss_ref, o_ref):
    n = x_ref.shape[0] // 2
    xr, xi, hr, hi = x_ref[:n], x_ref[n:], h_ref[:n], h_ref[n:]
    scale = lax.rsqrt(ss_ref[...] + EPS)
    o_ref[:n] = ((xr * hr - xi * hi) * scale).astype(o_ref.dtype)
    o_ref[n:] = ((xr * hi + xi * hr) * scale).astype(o_ref.dtype)


def _cmul(x, h, ss, d, order):
    rows = x.shape[0]
    td = _tile(d, 512, LANES)
    nd = d // td
    return pl.pallas_call(
        _cmul_body,
        grid=(nd,),
        in_specs=[pl.BlockSpec((rows, td), lambda j: (0, j)),
                  pl.BlockSpec((rows, td), lambda j: (0, order * nd + j)),
                  pl.BlockSpec((1, td), lambda j: (0, order * nd + j))],
        out_specs=pl.BlockSpec((rows, td), lambda j: (0, j)),
        out_shape=jax.ShapeDtypeStruct((rows, d), BF16),
        compiler_params=_params("parallel"),
        name="spectrum_product",
    )(x, h, ss)


def _hyena_mixer(x, norm, w_in, b_in, conv_w, conv_b, fparams, skip):
    n_tok, d = x.shape
    u = _norm_matmul(x, *norm, w_in, out_dtype=F32, epilogue=_ep_bias, extras=(_row_extra(b_in),),
                     name="hyena_in")
    u = _short_conv(u, conv_w, conv_b)
    filt, ss = _hyena_filters(n_tok, d, *fparams)

    if n_tok <= SINGLE_STAGE_MAX_LEN:
        plan = _dft_plan_single(n_tok)
        fwd, inv = jnp.asarray(plan["fwd"]), jnp.asarray(plan["inv"])
        part_cols = lambda p: (lambda tn: (lambda j: p * (d // tn) + j))
        hf = _matmul(fwd, filt, out_dtype=F32, tm=2 * plan["n"], tn=512, name="dft_filter")
        z_src, z_cols = u, part_cols(HYENA_ORDER)
        for order in range(HYENA_ORDER):
            x = _matmul(fwd, z_src, out_dtype=F32, tm=2 * plan["n"], tn=512, w_cols=(d, z_cols),
                        name="dft_signal")
            y = _cmul(x, hf, ss, d, order)
            last = order == HYENA_ORDER - 1
            z_src = _matmul(inv, y, out_dtype=BF16 if last else F32, epilogue=_ep_gate, tn=512,
                            extras=(_tile_extra(u, part_cols(order)), _tile_extra(z_src, z_cols),
                                    _row_extra(skip[order])), name="idft_gate")
            z_cols = part_cols(0)
        return z_src

    plan = _dft_plan(n_tok)
    ah = _dft_stage1(plan, filt, HYENA_ORDER * d)
    z_src, z_part = u, HYENA_ORDER
    for order in range(HYENA_ORDER):
        a = _dft_stage1(plan, z_src, d, z_part)
        q = _spectral_conv(plan, a, ah, ss, order)
        last = order == HYENA_ORDER - 1
        z_src = _idft_gate(plan, q, u, order, z_src, z_part, skip[order], BF16 if last else F32)
        z_part = 0
    return z_src


def kernel(x, c, ctx, c_ctx, mod_w, mod_b, norm_mix_g, norm_mlp_g, attn_w_in, attn_w_out, attn_sink, attn_q_norm, attn_k_norm, hy_w_in, hy_b_in, hy_conv_w, hy_conv_b, hy_f_w1, hy_f_b1, hy_f_freq1, hy_f_w2, hy_f_b2, hy_f_freq2, hy_f_w3, hy_skip, hy_w_out, hy_b_out, mlp_w1, mlp_w2, final_g):
    batch, n_lat, d = x.shape
    n_ctx = ctx.shape[1]
    depth = mod_w.shape[0]
    assert batch == 1 and n_lat % GRID_W == 0 and n_lat % WINDOW == 0
    last_ctx_layer = 2 * ((depth - 1) // 2)
    rope = _rope_tables(n_lat)
    bf = lambda w: w.astype(BF16)

    cond = jnp.zeros((16, d), F32).at[0].set(c[0]).at[1].set(c_ctx)
    mod = _modulation(cond, mod_w, mod_b)

    xs = x[0]
    cs = ctx[0]
    for i in range(depth):
        j = i // 2
        is_attn = i % 2 == 0
        ctx_updated = i < last_ctx_layer
        sh1, sc1, g1, sh2, sc2, g2 = jnp.split(mod[i, 0:1], N_MOD, axis=-1)
        csh1, csc1, cg1, csh2, csc2, cg2 = jnp.split(mod[i, 1:2], N_MOD, axis=-1)
        if is_attn:
            w_in, w_out = bf(attn_w_in[j]), bf(attn_w_out[j])
            h_lat = _rms_norm(xs, norm_mix_g[i], sh1, sc1)
            h_ctx = _rms_norm(cs, norm_mix_g[i], csh1, csc1)
            qkv = _matmul(h_lat, w_in, out_dtype=F32, tm=BIG_TM, name="attn_in")
            q, k, v = _qkv_prep(qkv, attn_q_norm[j], attn_k_norm[j], rope)
            qkv_c = _matmul(h_ctx, w_in, out_dtype=F32, name="attn_in_ctx")
            q_c, k_c, v_c = _qkv_prep(qkv_c, attn_q_norm[j], attn_k_norm[j], None)
            wv_t = attn_w_in[j][:, Q_WIDTH + KV_WIDTH + A_KV_HEADS * HEAD_DIM:].T
            o = _window_attention(q, k, v, k_c, v_c, attn_sink[j])
            o = _global_attention(q, k, _project_t(wv_t, h_lat), k_c, _project_t(wv_t, h_ctx), o)
            xs = _matmul(o, w_out, out_dtype=F32, epilogue=_ep_resid, tm=BIG_TM,
                         extras=(_tile_extra(xs), _row_extra(g1)), name="attn_out")
            if ctx_updated:
                o_c = _context_attention(q_c, k_c, v_c, attn_sink[j])
                cs = _matmul(o_c, w_out, out_dtype=F32, epilogue=_ep_resid,
                             extras=(_tile_extra(cs), _row_extra(cg1)), name="attn_out_ctx")
        else:
            w_in, w_out = bf(hy_w_in[j]), bf(hy_w_out[j])
            fparams = (hy_f_w1[j], hy_f_b1[j], hy_f_freq1[j], hy_f_w2[j], hy_f_b2[j], hy_f_freq2[j], hy_f_w3[j])
            z = _hyena_mixer(xs, (norm_mix_g[i], sh1, sc1), w_in, hy_b_in[j], hy_conv_w[j], hy_conv_b[j],
                             fparams, hy_skip[j])
            xs = _matmul(z, w_out, out_dtype=F32, epilogue=_ep_resid_bias, tm=BIG_TM,
                         extras=(_tile_extra(xs), _row_extra(g1), _row_extra(hy_b_out[j])), name="hyena_out")
            if ctx_updated:
                z_c = _hyena_mixer(cs, (norm_mix_g[i], csh1, csc1), w_in, hy_b_in[j], hy_conv_w[j],
                                   hy_conv_b[j], fparams, hy_skip[j])
                cs = _matmul(z_c, w_out, out_dtype=F32, epilogue=_ep_resid_bias,
                             extras=(_tile_extra(cs), _row_extra(cg1), _row_extra(hy_b_out[j])),
                             name="hyena_out_ctx")
        w1, w2 = bf(mlp_w1[i]), bf(mlp_w2[i])
        d_ff = w1.shape[1]
        a1 = _norm_matmul(xs, norm_mlp_g[i], sh2, sc2, w1, out_dtype=BF16, epilogue=_ep_relu2, name="mlp_up")
        xs = _matmul(a1, w2, out_dtype=F32, epilogue=_ep_resid, tn=512, tk=d_ff,
                     extras=(_tile_extra(xs), _row_extra(g2)), name="mlp_down")
        if ctx_updated:
            a1c = _norm_matmul(cs, norm_mlp_g[i], csh2, csc2, w1, out_dtype=BF16, epilogue=_ep_relu2,
                               name="mlp_up_ctx")
            cs = _matmul(a1c, w2, out_dtype=F32, epilogue=_ep_resid, tn=512, tk=d_ff,
                         extras=(_tile_extra(cs), _row_extra(cg2)), name="mlp_down_ctx")
    return _rms_norm(xs, final_g, out_dtype=F32)[None]
```

```python
import functools
import math

import numpy as np
import jax
import jax.numpy as jnp
from jax import lax
from jax.experimental import pallas as pl
from jax.experimental.pallas import tpu as pltpu

F32 = jnp.float32
BF16 = jnp.bfloat16

GRID_W = 64
HEAD_DIM = 128
A_Q_HEADS = 8
A_KV_HEADS = 2
B_Q_HEADS = 8
B_KV_HEADS = 2
GROUP = A_Q_HEADS // A_KV_HEADS
WINDOW = 128
ROPE_THETA = 10000.0
AXIS_DIM = HEAD_DIM // 2
Q_WIDTH = (A_Q_HEADS + B_Q_HEADS) * HEAD_DIM
KV_HEADS = A_KV_HEADS + B_KV_HEADS
KV_WIDTH = KV_HEADS * HEAD_DIM
HYENA_ORDER = 2
SHORT_CONV = 3
FILTER_EMB = 33
DECAY_TARGET = 1e-2
FAST_DECAY_PCT = 0.3
SLOW_DECAY_PCT = 1.5
MAX_DECAY = math.log(DECAY_TARGET) / FAST_DECAY_PCT
MIN_DECAY = math.log(DECAY_TARGET) / SLOW_DECAY_PCT
N_MOD = 6
EPS = 1e-6
LOG2_E = math.log2(math.e)
LN_2 = math.log(2.0)

VMEM_LIMIT_BYTES = 56 * 1024 * 1024
LANES = 128
SINGLE_STAGE_MAX_LEN = 512
BIG_TM = 1024


def _params(*sem):
    return pltpu.CompilerParams(dimension_semantics=sem, vmem_limit_bytes=VMEM_LIMIT_BYTES)


def _tile(n, pref, align):
    if n <= pref:
        return n
    t = (pref // align) * align
    while t >= align:
        if n % t == 0:
            return t
        t -= align
    return n


def _mod_body(a_ref, w_ref, b_ref, o_ref):
    a = a_ref[...]
    act = a * (1.0 / (1.0 + jnp.exp(-a)))
    o_ref[...] = jnp.dot(act.astype(BF16), w_ref[...].astype(BF16),
                         preferred_element_type=F32) + b_ref[...]


def _modulation(cond, mod_w, mod_b):
    depth, d, n = mod_w.shape
    r = cond.shape[0]
    tn = _tile(n, 512, LANES)
    return pl.pallas_call(
        _mod_body,
        grid=(depth, n // tn),
        in_specs=[pl.BlockSpec((r, d), lambda l, j: (0, 0)),
                  pl.BlockSpec((None, d, tn), lambda l, j: (l, 0, j)),
                  pl.BlockSpec((None, 1, tn), lambda l, j: (l, 0, j))],
        out_specs=pl.BlockSpec((None, r, tn), lambda l, j: (l, 0, j)),
        out_shape=jax.ShapeDtypeStruct((depth, r, n), F32),
        compiler_params=_params("parallel", "parallel"),
        name="modulation",
    )(cond, mod_w, mod_b.reshape(depth, 1, n))


def _norm_body(x_ref, g_ref, *rest, modulated):
    x = x_ref[...]
    y = x * lax.rsqrt(jnp.mean(x * x, axis=-1, keepdims=True) + EPS) * g_ref[...]
    if modulated:
        sh_ref, sc_ref, o_ref = rest
        y = y * (1.0 + sc_ref[...]) + sh_ref[...]
    else:
        (o_ref,) = rest
    o_ref[...] = y.astype(o_ref.dtype)


def _rms_norm(x, g, shift=None, scale=None, out_dtype=None):
    t, d = x.shape
    out_dtype = BF16 if out_dtype is None else out_dtype
    tm = _tile(t, 512, 16)
    row = pl.BlockSpec((1, d), lambda i: (0, 0))
    vecs = [g.reshape(1, d)]
    if shift is not None:
        vecs += [shift.reshape(1, d), scale.reshape(1, d)]
    return pl.pallas_call(
        functools.partial(_norm_body, modulated=shift is not None),
        grid=(t // tm,),
        in_specs=[pl.BlockSpec((tm, d), lambda i: (i, 0))] + [row] * len(vecs),
        out_specs=pl.BlockSpec((tm, d), lambda i: (i, 0)),
        out_shape=jax.ShapeDtypeStruct((t, d), out_dtype),
        compiler_params=_params("parallel"),
        name="rms_norm",
    )(x, *vecs)


def _ep_none(acc):
    return acc


def _ep_bias(acc, b):
    return acc + b


def _ep_relu2(acc):
    return jnp.square(jnp.maximum(acc, 0.0))


def _ep_resid(acc, x, g):
    return x + g * acc


def _ep_resid_bias(acc, x, g, b):
    return x + g * (acc + b)


def _ep_gate(acc, x, v, skip):
    return x * (acc + v * skip)


def _mm_body(*refs, nk, epilogue, n_extra):
    a_ref, w_ref = refs[0], refs[1]
    extra = refs[2:2 + n_extra]
    o_ref = refs[2 + n_extra]
    a = a_ref[...].astype(BF16)
    w = w_ref[...].astype(BF16)
    part = jnp.dot(a, w, preferred_element_type=F32)
    if nk == 1:
        o_ref[...] = epilogue(part, *[e[...] for e in extra]).astype(o_ref.dtype)
        return
    acc_ref = refs[3 + n_extra]
    k = pl.program_id(2)

    @pl.when(k == 0)
    def _():
        acc_ref[...] = part

    @pl.when(k > 0)
    def _():
        acc_ref[...] += part

    @pl.when(k == nk - 1)
    def _():
        o_ref[...] = epilogue(acc_ref[...], *[e[...] for e in extra]).astype(o_ref.dtype)


def _weight_spec(w, layer, tk, tn, index):
    if layer is None:
        return pl.BlockSpec((tk, tn), index)
    return pl.BlockSpec((None, tk, tn), lambda *ids: (layer,) + tuple(index(*ids)))


def _matmul(a, w, *, out_dtype, epilogue=_ep_none, extras=(), tm=512, tn=1024, tk=2048,
            w_cols=None, layer=None, name="matmul"):
    m, kdim = a.shape
    n = w.shape[-1] if w_cols is None else w_cols[0]
    tm = _tile(m, tm, 16)
    tn = _tile(n, tn, LANES)
    tk = _tile(kdim, tk, LANES)
    nk = kdim // tk
    wmap = (lambda j: j) if w_cols is None else w_cols[1](tn)
    in_specs = [pl.BlockSpec((tm, tk), lambda i, j, k: (i, k)),
                _weight_spec(w, layer, tk, tn, lambda i, j, k: (k, wmap(j)))]
    in_specs += [fn(tm, tn) for _, fn in extras]
    return pl.pallas_call(
        functools.partial(_mm_body, nk=nk, epilogue=epilogue, n_extra=len(extras)),
        grid=(m // tm, n // tn, nk),
        in_specs=in_specs,
        out_specs=pl.BlockSpec((tm, tn), lambda i, j, k: (i, j)),
        out_shape=jax.ShapeDtypeStruct((m, n), out_dtype),
        scratch_shapes=[pltpu.VMEM((tm, tn), F32)] if nk > 1 else [],
        compiler_params=_params("parallel", "parallel", "arbitrary"),
        name=name,
    )(a, w, *[arr for arr, _ in extras])


def _nmm_body(x_ref, g_ref, sh_ref, sc_ref, w_ref, *rest, epilogue, n_extra):
    extra = rest[:n_extra]
    o_ref, h_ref = rest[n_extra], rest[n_extra + 1]

    @pl.when(pl.program_id(1) == 0)
    def _():
        x = x_ref[...]
        y = x * lax.rsqrt(jnp.mean(x * x, axis=-1, keepdims=True) + EPS) * g_ref[...]
        h_ref[...] = (y * (1.0 + sc_ref[...]) + sh_ref[...]).astype(h_ref.dtype)

    acc = jnp.dot(h_ref[...], w_ref[...].astype(BF16), preferred_element_type=F32)
    o_ref[...] = epilogue(acc, *[e[...] for e in extra]).astype(o_ref.dtype)


def _norm_matmul(x, g, shift, scale, w, *, out_dtype, epilogue=_ep_none, extras=(), tm=1024, tn=1024,
                 layer=None, name="norm_matmul"):
    m, d = x.shape
    n = w.shape[-1]
    tm = _tile(m, tm, 16)
    tn = _tile(n, tn, LANES)
    vec = pl.BlockSpec((1, d), lambda i, j: (0, 0))
    with_k = lambda spec_fn: (lambda bs: pl.BlockSpec(bs.block_shape, lambda i, j: bs.index_map(i, j, 0)))(
        spec_fn(tm, tn))
    return pl.pallas_call(
        functools.partial(_nmm_body, epilogue=epilogue, n_extra=len(extras)),
        grid=(m // tm, n // tn),
        in_specs=[pl.BlockSpec((tm, d), lambda i, j: (i, 0)), vec, vec, vec,
                  _weight_spec(w, layer, d, tn, lambda i, j: (0, j))] + [with_k(fn) for _, fn in extras],
        out_specs=pl.BlockSpec((tm, tn), lambda i, j: (i, j)),
        out_shape=jax.ShapeDtypeStruct((m, n), out_dtype),
        scratch_shapes=[pltpu.VMEM((tm, d), BF16)],
        compiler_params=_params("arbitrary", "arbitrary"),
        name=name,
    )(x, g.reshape(1, d), shift.reshape(1, d), scale.reshape(1, d), w, *[arr for arr, _ in extras])


def _row_extra(vec, period_blocks=None):
    vec = vec.reshape(1, -1)
    if period_blocks is None:
        return vec, lambda tm, tn: pl.BlockSpec((1, tn), lambda i, j, k: (0, j))
    return vec, lambda tm, tn: pl.BlockSpec((1, tn), lambda i, j, k: (0, j % period_blocks(tn)))


def _tile_extra(arr, colmap=None):
    cm = colmap if colmap is not None else (lambda tn: (lambda j: j))
    return arr, lambda tm, tn: pl.BlockSpec((tm, tn), lambda i, j, k: (i, cm(tn)(j)))


def _rope_tables(n_tok):
    rows = n_tok // GRID_W
    row = jnp.repeat(jnp.arange(rows, dtype=F32), GRID_W)
    col = jnp.tile(jnp.arange(GRID_W, dtype=F32), rows)
    inv = ROPE_THETA ** (-jnp.arange(0, AXIS_DIM, 2, dtype=F32) / AXIS_DIM)
    ang_r = row[:, None] * inv[None]
    ang_c = col[:, None] * inv[None]
    cos = jnp.concatenate([jnp.cos(ang_r)] * 2 + [jnp.cos(ang_c)] * 2, axis=-1)
    sin = jnp.concatenate([-jnp.sin(ang_r), jnp.sin(ang_r), -jnp.sin(ang_c), jnp.sin(ang_c)], axis=-1)
    return cos, sin


def _prep_body(*refs, rope):
    if rope:
        qkv_ref, qn_ref, kn_ref, cos_ref, sin_ref, q_ref, k_ref, v_ref = refs
        cos, sin = cos_ref[...], sin_ref[...]
        lane = lax.broadcasted_iota(jnp.int32, cos.shape, 1)
        first = (lane % (AXIS_DIM)) < (AXIS_DIM // 2)
    else:
        qkv_ref, qn_ref, kn_ref, q_ref, k_ref, v_ref = refs
    scale = HEAD_DIM ** -0.5
    quarter = AXIS_DIM // 2

    def rot(x):
        if not rope:
            return x
        swapped = jnp.where(first, pltpu.roll(x, HEAD_DIM - quarter, 1), pltpu.roll(x, quarter, 1))
        return x * cos + swapped * sin

    def nrm(x, g):
        return x * lax.rsqrt(jnp.mean(x * x, axis=-1, keepdims=True) + EPS) * g

    for h in range(A_Q_HEADS + B_Q_HEADS):
        sl = slice(h * HEAD_DIM, (h + 1) * HEAD_DIM)
        x = qkv_ref[:, sl]
        if h >= A_Q_HEADS:
            x = nrm(x, qn_ref[...])
        q_ref[:, sl] = (rot(x) * (scale * LOG2_E if h >= A_Q_HEADS else scale)).astype(BF16)
    for h in range(KV_HEADS):
        x = qkv_ref[:, Q_WIDTH + h * HEAD_DIM:Q_WIDTH + (h + 1) * HEAD_DIM]
        if h >= A_KV_HEADS:
            x = nrm(x, kn_ref[...])
        k_ref[:, h * HEAD_DIM:(h + 1) * HEAD_DIM] = rot(x).astype(BF16)
    v_ref[...] = qkv_ref[:, Q_WIDTH + KV_WIDTH:].astype(BF16)


def _qkv_prep(qkv, q_norm, k_norm, rope):
    t = qkv.shape[0]
    tm = _tile(t, 256, 16)
    vec = pl.BlockSpec((1, HEAD_DIM), lambda i: (0, 0))
    in_specs = [pl.BlockSpec((tm, qkv.shape[1]), lambda i: (i, 0)), vec, vec]
    args = [qkv, q_norm.reshape(1, HEAD_DIM), k_norm.reshape(1, HEAD_DIM)]
    if rope is not None:
        in_specs += [pl.BlockSpec((tm, HEAD_DIM), lambda i: (i, 0))] * 2
        args += list(rope)
    return pl.pallas_call(
        functools.partial(_prep_body, rope=rope is not None),
        grid=(t // tm,),
        in_specs=in_specs,
        out_specs=[pl.BlockSpec((tm, Q_WIDTH), lambda i: (i, 0)),
                   pl.BlockSpec((tm, KV_WIDTH), lambda i: (i, 0)),
                   pl.BlockSpec((tm, KV_WIDTH), lambda i: (i, 0))],
        out_shape=[jax.ShapeDtypeStruct((t, Q_WIDTH), BF16),
                   jax.ShapeDtypeStruct((t, KV_WIDTH), BF16),
                   jax.ShapeDtypeStruct((t, KV_WIDTH), BF16)],
        compiler_params=_params("parallel"),
        name="qkv_prep",
    )(*args)


def _stack_heads(q_ref):
    return jnp.concatenate([q_ref[:, g * HEAD_DIM:(g + 1) * HEAD_DIM] for g in range(GROUP)], axis=0)


def _unstack_heads(o_ref, out, t):
    for g in range(GROUP):
        o_ref[:, g * HEAD_DIM:(g + 1) * HEAD_DIM] = out[g * t:(g + 1) * t].astype(o_ref.dtype)


def _sink_column(sink_ref, first, t):
    head = lax.broadcasted_iota(jnp.int32, (GROUP * t, 1), 0) // t
    col = jnp.full((GROUP * t, 1), sink_ref[first], F32)
    for g in range(1, GROUP):
        col = jnp.where(head == g, sink_ref[first + g], col)
    return col


def _qk(q, k):
    return lax.dot_general(q, k, (((1,), (1,)), ((), ())), preferred_element_type=F32)


ONES_ROWS = 16
STREAMS = 1


def _gattn_body(q_ref, kc_ref, vtc_ref, k_ref, vt_ref, o_prev_ref, o_ref, *scratch, tq, ck, n_chunks):
    del o_prev_ref
    m_refs, acc_refs = scratch[:STREAMS], scratch[STREAMS:]
    per = GROUP // STREAMS
    qs = [jnp.concatenate([q_ref[:, g * HEAD_DIM:(g + 1) * HEAD_DIM] for g in range(s * per, (s + 1) * per)],
                          axis=0) for s in range(STREAMS)]
    for s in range(STREAMS):
        m_refs[s][...] = jnp.full(m_refs[s].shape, -jnp.inf, F32)
        acc_refs[s][...] = jnp.zeros(acc_refs[s].shape, F32)

    def step(kb, vtb):
        lhs = jnp.concatenate([vtb, jnp.ones((ONES_ROWS, vtb.shape[1]), BF16)], axis=0)
        for s in range(STREAMS):
            st = _qk(kb, qs[s])
            m_old = m_refs[s][...]
            m_new = jnp.maximum(m_old, jnp.max(st, axis=0, keepdims=True))
            p = jnp.exp2(st - m_new).astype(BF16)
            acc_refs[s][...] = (jnp.exp2(m_old - m_new) * acc_refs[s][...]
                                + jnp.dot(lhs, p, preferred_element_type=F32))
            m_refs[s][...] = m_new

    step(kc_ref[...], vtc_ref[...])

    def body(c, carry):
        off = pl.multiple_of(c * ck, ck)
        step(k_ref[pl.ds(off, ck), :], vt_ref[:, pl.ds(off, ck)])
        return carry

    lax.fori_loop(0, n_chunks, body, 0)
    for s in range(STREAMS):
        out = (acc_refs[s][:HEAD_DIM] / acc_refs[s][HEAD_DIM:HEAD_DIM + 1]).T
        for g in range(per):
            col = (s * per + g) * HEAD_DIM
            o_ref[:, col:col + HEAD_DIM] = out[g * tq:(g + 1) * tq].astype(o_ref.dtype)


def _global_attention(q, k, vt, k_ctx, vt_ctx, o_prev):
    s, n_ctx = q.shape[0], k_ctx.shape[0]
    tq = _tile(s, 512, LANES)
    ck = _tile(s, 2048, LANES)
    gw = GROUP * HEAD_DIM
    kspec = lambda n: pl.BlockSpec((n, HEAD_DIM), lambda h, i: (0, A_KV_HEADS + h))
    vspec = lambda n: pl.BlockSpec((HEAD_DIM, n), lambda h, i: (A_KV_HEADS + h, 0))
    return pl.pallas_call(
        functools.partial(_gattn_body, tq=tq, ck=ck, n_chunks=s // ck),
        grid=(B_KV_HEADS, s // tq),
        in_specs=[pl.BlockSpec((tq, gw), lambda h, i: (i, A_KV_HEADS + h)),
                  kspec(n_ctx), vspec(n_ctx), kspec(s), vspec(s), pl.BlockSpec(memory_space=pl.ANY)],
        out_specs=pl.BlockSpec((tq, gw), lambda h, i: (i, A_KV_HEADS + h)),
        out_shape=jax.ShapeDtypeStruct(o_prev.shape, o_prev.dtype),
        input_output_aliases={5: 0},
        scratch_shapes=([pltpu.VMEM((1, GROUP // STREAMS * tq), F32)] * STREAMS
                        + [pltpu.VMEM((HEAD_DIM + ONES_ROWS, GROUP // STREAMS * tq), F32)] * STREAMS),
        compiler_params=_params("parallel", "parallel"),
        name="global_attention",
    )(q, k_ctx, vt_ctx, k, vt, o_prev)


def _project_t_body(w_ref, h_ref, o_ref):
    o_ref[...] = _qk(w_ref[...].astype(BF16), h_ref[...]).astype(o_ref.dtype)


def _project_t(w_t, h):
    c, d = w_t.shape
    t = h.shape[0]
    tt = _tile(t, 1024, LANES)
    return pl.pallas_call(
        _project_t_body,
        grid=(t // tt,),
        in_specs=[pl.BlockSpec((c, d), lambda i: (0, 0)), pl.BlockSpec((tt, d), lambda i: (i, 0))],
        out_specs=pl.BlockSpec((c, tt), lambda i: (0, i)),
        out_shape=jax.ShapeDtypeStruct((c, t), BF16),
        compiler_params=_params("parallel"),
        name="project_t",
    )(w_t, h)


def _wattn_body(sink_ref, q_ref, kc_ref, vtc_ref, kp_ref, k0_ref, kn_ref, vtp_ref, vt0_ref, vtn_ref, o_ref,
                *, nb, n_ctx):
    h = pl.program_id(0)
    i = pl.program_id(1)
    w = WINDOW
    q = _stack_heads(q_ref)
    kcat = jnp.concatenate([kc_ref[...], kp_ref[...], k0_ref[...], kn_ref[...]], axis=0)
    vtcat = jnp.concatenate([vtc_ref[...], vtp_ref[...], vt0_ref[...], vtn_ref[...]], axis=1)
    st = _qk(kcat, q)
    c = lax.broadcasted_iota(jnp.int32, st.shape, 0) - n_ctx
    r = lax.broadcasted_iota(jnp.int32, st.shape, 1) % w
    lo = jnp.maximum(r, jnp.where(i >= 1, 0, w))
    hi = jnp.minimum(r + 2 * w, jnp.where(i + 1 < nb, 3 * w - 1, 2 * w - 1))
    valid = (c < 0) | ((c >= lo) & (c <= hi))
    st = jnp.where(valid, st, -jnp.inf)
    head = lax.broadcasted_iota(jnp.int32, (1, GROUP * w), 1) // w
    sink = jnp.full((1, GROUP * w), sink_ref[h * GROUP], F32)
    for g in range(1, GROUP):
        sink = jnp.where(head == g, sink_ref[h * GROUP + g], sink)
    m = jnp.maximum(jnp.max(st, axis=0, keepdims=True), sink)
    p = jnp.exp(st - m).astype(BF16)
    lhs = jnp.concatenate([vtcat, jnp.ones((ONES_ROWS, vtcat.shape[1]), BF16)], axis=0)
    acc = jnp.dot(lhs, p, preferred_element_type=F32)
    denom = acc[HEAD_DIM:HEAD_DIM + 1] + jnp.exp(sink - m)
    _unstack_heads(o_ref, (acc[:HEAD_DIM] / denom).T, w)


def _window_attention(q, k, vt, k_ctx, vt_ctx, sink):
    s, n_ctx = q.shape[0], k_ctx.shape[0]
    w = WINDOW
    nb = s // w
    gw = GROUP * HEAD_DIM
    before = lambda i: jnp.maximum(i - 1, 0)
    after = lambda i: jnp.minimum(i + 1, nb - 1)
    kblk = lambda pos: pl.BlockSpec((w, HEAD_DIM), lambda h, i: (pos(i), h))
    vblk = lambda pos: pl.BlockSpec((HEAD_DIM, w), lambda h, i: (h, pos(i)))
    same = lambda i: i
    return pl.pallas_call(
        functools.partial(_wattn_body, nb=nb, n_ctx=n_ctx),
        grid=(A_KV_HEADS, nb),
        in_specs=[pl.BlockSpec(memory_space=pltpu.SMEM),
                  pl.BlockSpec((w, gw), lambda h, i: (i, h)),
                  pl.BlockSpec((n_ctx, HEAD_DIM), lambda h, i: (0, h)),
                  pl.BlockSpec((HEAD_DIM, n_ctx), lambda h, i: (h, 0)),
                  kblk(before), kblk(same), kblk(after), vblk(before), vblk(same), vblk(after)],
        out_specs=pl.BlockSpec((w, gw), lambda h, i: (i, h)),
        out_shape=jax.ShapeDtypeStruct((s, Q_WIDTH), BF16),
        compiler_params=_params("parallel", "parallel"),
        name="window_attention",
    )(sink, q, k_ctx, vt_ctx, k, k, k, vt, vt, vt)


def _cattn_body(sink_ref, q_ref, k_ref, v_ref, o_ref, *, n_ctx):
    h = pl.program_id(0)
    q = _stack_heads(q_ref)
    s = _qk(q, k_ref[...]) * jnp.where(h >= A_KV_HEADS, LN_2, 1.0)
    sink = _sink_column(sink_ref, h * GROUP, n_ctx)
    m = jnp.maximum(jnp.max(s, axis=-1, keepdims=True), sink)
    p = jnp.exp(s - m)
    denom = jnp.sum(p, axis=-1, keepdims=True) + jnp.exp(sink - m)
    out = jnp.dot(p.astype(BF16), v_ref[...], preferred_element_type=F32) / denom
    _unstack_heads(o_ref, out, n_ctx)


def _context_attention(q, k, v, sink):
    n_ctx = q.shape[0]
    gw = GROUP * HEAD_DIM
    sink_all = jnp.concatenate([sink.astype(F32), jnp.full((B_Q_HEADS,), -jnp.inf, F32)])
    kv = pl.BlockSpec((n_ctx, HEAD_DIM), lambda h: (0, h))
    return pl.pallas_call(
        functools.partial(_cattn_body, n_ctx=n_ctx),
        grid=(KV_HEADS,),
        in_specs=[pl.BlockSpec(memory_space=pltpu.SMEM),
                  pl.BlockSpec((n_ctx, gw), lambda h: (0, h)), kv, kv],
        out_specs=pl.BlockSpec((n_ctx, gw), lambda h: (0, h)),
        out_shape=jax.ShapeDtypeStruct((n_ctx, Q_WIDTH), BF16),
        compiler_params=_params("parallel"),
        name="context_attention",
    )(sink_all, q, k, v)


def _short_conv_body(u_ref, up_ref, un_ref, w_ref, b_ref, o_ref, *, n_row_tiles):
    i = pl.program_id(0)
    u = u_ref[...]
    tm = u.shape[0]
    halo = up_ref.shape[0]
    row = lax.broadcasted_iota(jnp.int32, u.shape, 0)
    before = jnp.where(i > 0, up_ref[halo - 1:halo, :], 0.0)
    after = jnp.where(i + 1 < n_row_tiles, un_ref[0:1, :], 0.0)
    left = jnp.where(row == 0, before, pltpu.roll(u, 1, 0))
    right = jnp.where(row == tm - 1, after, pltpu.roll(u, tm - 1, 0))
    o_ref[...] = left * w_ref[0:1, :] + u * w_ref[1:2, :] + right * w_ref[2:3, :] + b_ref[...]


def _short_conv(u, w, b):
    n, c = u.shape
    halo = 8
    tm = _tile(n, 256, halo)
    tc = _tile(c, 2048, LANES)
    hb = tm // halo
    n_row_tiles = n // tm
    return pl.pallas_call(
        functools.partial(_short_conv_body, n_row_tiles=n_row_tiles),
        grid=(n_row_tiles, c // tc),
        in_specs=[pl.BlockSpec((tm, tc), lambda i, j: (i, j)),
                  pl.BlockSpec((halo, tc), lambda i, j: (jnp.maximum(i * hb - 1, 0), j)),
                  pl.BlockSpec((halo, tc), lambda i, j: (jnp.minimum((i + 1) * hb, n // halo - 1), j)),
                  pl.BlockSpec((SHORT_CONV, tc), lambda i, j: (0, j)),
                  pl.BlockSpec((1, tc), lambda i, j: (0, j))],
        out_specs=pl.BlockSpec((tm, tc), lambda i, j: (i, j)),
        out_shape=jax.ShapeDtypeStruct((n, c), F32),
        compiler_params=_params("parallel", "parallel"),
        name="short_conv",
    )(u, u, u, w, b.reshape(1, c))


def _filter_body(feat_ref, w1_ref, b1_ref, f1_ref, w2_ref, b2_ref, f2_ref, w3_ref, dl_ref,
                 h_ref, ss_ref, *, n_tok):
    i = pl.program_id(0)
    tl = feat_ref.shape[0]
    mm = lambda a, b: jnp.dot(a.astype(BF16), b.astype(BF16), preferred_element_type=F32)
    hid = jnp.sin(f1_ref[...] * (mm(feat_ref[...], w1_ref[...]) + b1_ref[...]))
    hid = jnp.sin(f2_ref[...] * (mm(hid, w2_ref[...]) + b2_ref[...]))
    h = mm(hid, w3_ref[...])
    t = i * tl + lax.broadcasted_iota(jnp.int32, (tl, 1), 0)
    offs = jnp.abs(t - n_tok // 2).astype(F32) * (2.0 / n_tok)
    h = h * jnp.exp(-offs * dl_ref[...])
    h_ref[...] = h

    @pl.when(i == 0)
    def _():
        ss_ref[...] = jnp.zeros(ss_ref.shape, F32)

    ss_ref[...] += jnp.sum(h * h, axis=0, keepdims=True)


def _hyena_filters(n_tok, d, w1, b1, fr1, w2, b2, fr2, w3):
    bands = (FILTER_EMB - 1) // 2
    t = jnp.linspace(0.0, 1.0, n_tok, dtype=F32)[:, None]
    wv = 2.0 * math.pi * jnp.arange(n_tok, dtype=F32)[:, None] / n_tok
    f = jnp.linspace(1e-4, bands - 1, bands, dtype=F32)[None]
    feats = jnp.concatenate([t, jnp.cos(f * wv), -jnp.sin(f * wv)], axis=-1)
    deltas = jnp.abs(jnp.linspace(MIN_DECAY, MAX_DECAY, d, dtype=F32))
    hidden = w1.shape[1]
    pad_e, pad_h = LANES - FILTER_EMB, LANES - hidden
    padv = lambda v: jnp.pad(v.reshape(1, hidden), ((0, 0), (0, pad_h)))
    c = HYENA_ORDER * d
    tl = _tile(n_tok, 256, 8)
    full = lambda shape: pl.BlockSpec(shape, lambda i: (0, 0))
    return pl.pallas_call(
        functools.partial(_filter_body, n_tok=n_tok),
        grid=(n_tok // tl,),
        in_specs=[pl.BlockSpec((tl, LANES), lambda i: (i, 0)),
                  full((LANES, LANES)), full((1, LANES)), full((1, LANES)),
                  full((LANES, LANES)), full((1, LANES)), full((1, LANES)),
                  full((LANES, c)), full((1, c))],
        out_specs=[pl.BlockSpec((tl, c), lambda i: (i, 0)), full((1, c))],
        out_shape=[jax.ShapeDtypeStruct((n_tok, c), F32), jax.ShapeDtypeStruct((1, c), F32)],
        compiler_params=_params("arbitrary"),
        name="hyena_filter",
    )(jnp.pad(feats, ((0, 0), (0, pad_e))),
      jnp.pad(w1, ((0, pad_e), (0, pad_h))), padv(b1), padv(fr1),
      jnp.pad(w2, ((0, pad_h), (0, pad_h))), padv(b2), padv(fr2),
      jnp.pad(w3, ((0, pad_h), (0, 0))), jnp.tile(deltas, HYENA_ORDER).reshape(1, c))


def _lane_table(vals):
    return jnp.broadcast_to(jnp.asarray(vals)[..., None], vals.shape + (LANES,))


SUBLANES = 8
DFT_ROW_GROUP = 16


@functools.lru_cache(maxsize=None)
def _dft_plan(n_tok):
    n = 2 * n_tok
    log = n.bit_length() - 1
    assert 1 << log == n, "sequence length must be a power of two"
    n2 = 1 << (log // 2)
    n1 = n // n2
    assert n1 % 4 == 0 and n2 % DFT_ROW_GROUP == 0
    eye = np.eye(SUBLANES)
    nk1 = n1 // 2 + 1
    nk1_pad = -(-nk1 // SUBLANES) * SUBLANES
    a1 = 2.0 * np.pi * np.outer(np.arange(nk1), np.arange(n1 // 2)) / n1
    f1 = np.kron(np.concatenate([np.cos(a1), -np.sin(a1)], axis=0), eye)
    a2 = 2.0 * np.pi * np.outer(np.arange(n2), np.arange(n2)) / n2
    c2, s2 = np.cos(a2), np.sin(a2)
    g2 = np.block([[c2, s2], [-s2, c2]])
    g2i = np.block([[c2, -s2], [s2, c2]])
    rows = np.arange(n1 // 4, n1 // 4 + n1 // 2)
    a1i = 2.0 * np.pi * np.outer(rows, np.arange(nk1_pad)) / n1
    weight = np.where((np.arange(nk1_pad) == 0) | (np.arange(nk1_pad) == n1 // 2), 1.0, 2.0)
    weight = np.where(np.arange(nk1_pad) < nk1, weight, 0.0) / n
    f1i = np.kron(np.concatenate([np.cos(a1i) * weight, -np.sin(a1i) * weight], axis=1), eye)
    tw = 2.0 * np.pi * np.outer(np.arange(n2), np.arange(nk1)) / n
    tw1 = tw.reshape(n2 // SUBLANES, SUBLANES, nk1).transpose(0, 2, 1).reshape(n2 // SUBLANES, nk1 * SUBLANES)
    f32 = lambda m: m.astype(np.float32)
    return dict(n1=n1, n2=n2, nk1=nk1, nk1_pad=nk1_pad, f1=f32(f1), g2=f32(g2), g2i=f32(g2i), f1i=f32(f1i),
                tw1_cos=f32(np.cos(tw1)), tw1_sin=f32(np.sin(tw1)),
                tw2_cos=f32(np.cos(tw.T)), tw2_sin=f32(np.sin(tw.T)))


def _first_step():
    return (pl.program_id(0) == 0) & (pl.program_id(1) == 0)


def _stage1_body(f_ref, z_ref, c_ref, s_ref, o_ref, fb_ref):
    n1h, g, tc = z_ref.shape
    nk1 = o_ref.shape[1]
    half = nk1 * SUBLANES
    reps = tc // LANES

    @pl.when(_first_step())
    def _():
        fb_ref[...] = f_ref[...].astype(BF16)

    z = z_ref[...]
    re_parts, im_parts = [], []
    for a in range(g // SUBLANES):
        zz = z[:, a * SUBLANES:(a + 1) * SUBLANES, :].reshape(n1h * SUBLANES, tc)
        acc = jnp.dot(fb_ref[...], zz.astype(BF16), preferred_element_type=F32)
        re, im = acc[:half], acc[half:]
        c = jnp.tile(c_ref[a], (1, reps))
        s = jnp.tile(s_ref[a], (1, reps))
        re_parts.append((re * c + im * s).reshape(nk1, SUBLANES, tc))
        im_parts.append((im * c - re * s).reshape(nk1, SUBLANES, tc))
    o_ref[0] = jnp.concatenate(re_parts, axis=1).astype(o_ref.dtype)
    o_ref[1] = jnp.concatenate(im_parts, axis=1).astype(o_ref.dtype)


def _dft_stage1(plan, src, width, part=0):
    n1, n2, nk1 = plan["n1"], plan["n2"], plan["nk1"]
    g = DFT_ROW_GROUP
    tc = _tile(width, 512, LANES)
    nc = width // tc
    sub = g // SUBLANES
    tw = lambda t: _lane_table(t).reshape(n2 // g, sub, nk1 * SUBLANES, LANES)
    tw_spec = pl.BlockSpec((None, sub, nk1 * SUBLANES, LANES), lambda i, j: (i, 0, 0, 0))
    f = jnp.asarray(plan["f1"])
    return pl.pallas_call(
        _stage1_body,
        grid=(n2 // g, nc),
        in_specs=[pl.BlockSpec(f.shape, lambda i, j: (0, 0)),
                  pl.BlockSpec((n1 // 2, g, tc), lambda i, j: (0, i, part * nc + j)),
                  tw_spec, tw_spec],
        out_specs=pl.BlockSpec((2, nk1, g, tc), lambda i, j: (0, 0, i, j)),
        out_shape=jax.ShapeDtypeStruct((2, nk1, n2, width), BF16),
        scratch_shapes=[pltpu.VMEM(f.shape, BF16)],
        compiler_params=_params("arbitrary", "arbitrary"),
        name="dft_stage1",
    )(f, src.reshape(n1 // 2, n2, src.shape[1]), tw(plan["tw1_cos"]), tw(plan["tw1_sin"]))


def _istage1_body(f_ref, q_ref, x_ref, v_ref, skip_ref, o_ref, fb_ref):
    _, nk1, g, tc = q_ref.shape
    n1h = o_ref.shape[0]
    nk1_pad = f_ref.shape[1] // (2 * SUBLANES)

    @pl.when(_first_step())
    def _():
        fb_ref[...] = f_ref[...].astype(BF16)

    q = q_ref[...].astype(F32)
    q = jnp.concatenate([q, jnp.zeros((2, nk1_pad - nk1, g, tc), F32)], axis=1)
    parts = []
    for a in range(g // SUBLANES):
        qq = q[:, :, a * SUBLANES:(a + 1) * SUBLANES, :].reshape(2 * nk1_pad * SUBLANES, tc)
        y = jnp.dot(fb_ref[...], qq.astype(BF16), preferred_element_type=F32)
        parts.append(y.reshape(n1h, SUBLANES, tc))
    y = jnp.concatenate(parts, axis=1)
    o_ref[...] = (x_ref[...] * (y + v_ref[...] * skip_ref[...])).astype(o_ref.dtype)


def _idft_gate(plan, q, x_src, x_part, v_src, v_part, skip, out_dtype):
    n1, n2, nk1 = plan["n1"], plan["n2"], plan["nk1"]
    d = q.shape[3]
    g = DFT_ROW_GROUP
    tc = _tile(d, 512, LANES)
    nc = d // tc
    f = jnp.asarray(plan["f1i"])
    tok = lambda part: pl.BlockSpec((n1 // 2, g, tc), lambda i, j: (0, i, part * nc + j))
    view = lambda arr: arr.reshape(n1 // 2, n2, arr.shape[1])
    out = pl.pallas_call(
        _istage1_body,
        grid=(n2 // g, nc),
        in_specs=[pl.BlockSpec(f.shape, lambda i, j: (0, 0)),
                  pl.BlockSpec((2, nk1, g, tc), lambda i, j: (0, 0, i, j)),
                  tok(x_part), tok(v_part), pl.BlockSpec((1, tc), lambda i, j: (0, j))],
        out_specs=tok(0),
        out_shape=jax.ShapeDtypeStruct((n1 // 2, n2, d), out_dtype),
        scratch_shapes=[pltpu.VMEM(f.shape, BF16)],
        compiler_params=_params("arbitrary", "arbitrary"),
        name="idft_gate",
    )(f, q, view(x_src), view(v_src), skip.reshape(1, d))
    return out.reshape(n1 // 2 * n2, d)


def _spectral_body(a_ref, ah_ref, ss_ref, g2_ref, g2i_ref, c_ref, s_ref, o_ref):
    n2 = a_ref.shape[1]
    g2 = g2_ref[...].astype(BF16)
    stack = lambda ref: jnp.concatenate([ref[0], ref[1]], axis=0)
    x = jnp.dot(g2, stack(a_ref), preferred_element_type=F32)
    hf = jnp.dot(g2, stack(ah_ref), preferred_element_type=F32)
    xr, xi, hr, hi = x[:n2], x[n2:], hf[:n2], hf[n2:]
    scale = lax.rsqrt(ss_ref[...] + EPS)
    y = jnp.concatenate([(xr * hr - xi * hi) * scale, (xr * hi + xi * hr) * scale], axis=0)
    p = jnp.dot(g2i_ref[...].astype(BF16), y.astype(BF16), preferred_element_type=F32)
    pr, pi = p[:n2], p[n2:]
    reps = pr.shape[1] // LANES
    c = jnp.tile(c_ref[...], (1, reps))
    s = jnp.tile(s_ref[...], (1, reps))
    o_ref[0] = (pr * c - pi * s).astype(o_ref.dtype)
    o_ref[1] = (pr * s + pi * c).astype(o_ref.dtype)


def _spectral_conv(plan, a, ah, ss, order):
    n1, n2, d = a.shape[1:]
    td = _tile(d, 2048, LANES)
    nd = d // td
    blk = lambda off: pl.BlockSpec((2, None, n2, td), lambda k1, j: (0, k1, 0, off + j))
    const = pl.BlockSpec((2 * n2, 2 * n2), lambda k1, j: (0, 0))
    tw = pl.BlockSpec((None, n2, LANES), lambda k1, j: (k1, 0, 0))
    return pl.pallas_call(
        _spectral_body,
        grid=(n1, nd),
        in_specs=[blk(0), blk(order * nd), pl.BlockSpec((1, td), lambda k1, j: (0, order * nd + j)),
                  const, const, tw, tw],
        out_specs=blk(0),
        out_shape=jax.ShapeDtypeStruct((2, n1, n2, d), BF16),
        compiler_params=_params("parallel", "parallel"),
        name="spectral_conv",
    )(a, ah, ss, jnp.asarray(plan["g2"]), jnp.asarray(plan["g2i"]),
      _lane_table(plan["tw2_cos"]), _lane_table(plan["tw2_sin"]))


@functools.lru_cache(maxsize=None)
def _dft_plan_single(n_tok):
    n = 2 * n_tok
    af = 2.0 * np.pi * np.outer(np.arange(n), np.arange(n_tok)) / n
    fwd = np.concatenate([np.cos(af), -np.sin(af)], axis=0)
    ai = 2.0 * np.pi * np.outer(np.arange(n_tok // 2, n_tok // 2 + n_tok), np.arange(n)) / n
    inv = np.concatenate([np.cos(ai), -np.sin(ai)], axis=1) / n
    return dict(n=n, fwd=fwd.astype(np.float32), inv=inv.astype(np.float32))


def _cmul_body(x_ref, h_ref, ss_ref, o_ref):
    n = x_ref.shape[0] // 2
    xr, xi, hr, hi = x_ref[:n], x_ref[n:], h_ref[:n], h_ref[n:]
    scale = lax.rsqrt(ss_ref[...] + EPS)
    o_ref[:n] = ((xr * hr - xi * hi) * scale).astype(o_ref.dtype)
    o_ref[n:] = ((xr * hi + xi * hr) * scale).astype(o_ref.dtype)


def _cmul(x, h, ss, d, order):
    rows = x.shape[0]
    td = _tile(d, 512, LANES)
    nd = d // td
    return pl.pallas_call(
        _cmul_body,
        grid=(nd,),
        in_specs=[pl.BlockSpec((rows, td), lambda j: (0, j)),
                  pl.BlockSpec((rows, td), lambda j: (0, order * nd + j)),
                  pl.BlockSpec((1, td), lambda j: (0, order * nd + j))],
        out_specs=pl.BlockSpec((rows, td), lambda j: (0, j)),
        out_shape=jax.ShapeDtypeStruct((rows, d), BF16),
        compiler_params=_params("parallel"),
        name="spectrum_product",
    )(x, h, ss)


def _hyena_mixer(x, norm, w_in, layer, b_in, conv_w, conv_b, fparams, skip):
    n_tok, d = x.shape
    u = _norm_matmul(x, *norm, w_in, layer=layer, out_dtype=F32, epilogue=_ep_bias,
                     extras=(_row_extra(b_in),), name="hyena_in")
    u = _short_conv(u, conv_w, conv_b)
    filt, ss = _hyena_filters(n_tok, d, *fparams)

    if n_tok <= SINGLE_STAGE_MAX_LEN:
        plan = _dft_plan_single(n_tok)
        fwd, inv = jnp.asarray(plan["fwd"]), jnp.asarray(plan["inv"])
        part_cols = lambda p: (lambda tn: (lambda j: p * (d // tn) + j))
        hf = _matmul(fwd, filt, out_dtype=F32, tm=2 * plan["n"], tn=512, name="dft_filter")
        z_src, z_cols = u, part_cols(HYENA_ORDER)
        for order in range(HYENA_ORDER):
            x = _matmul(fwd, z_src, out_dtype=F32, tm=2 * plan["n"], tn=512, w_cols=(d, z_cols),
                        name="dft_signal")
            y = _cmul(x, hf, ss, d, order)
            last = order == HYENA_ORDER - 1
            z_src = _matmul(inv, y, out_dtype=BF16 if last else F32, epilogue=_ep_gate, tn=512,
                            extras=(_tile_extra(u, part_cols(order)), _tile_extra(z_src, z_cols),
                                    _row_extra(skip[order])), name="idft_gate")
            z_cols = part_cols(0)
        return z_src

    plan = _dft_plan(n_tok)
    ah = _dft_stage1(plan, filt, HYENA_ORDER * d)
    z_src, z_part = u, HYENA_ORDER
    for order in range(HYENA_ORDER):
        a = _dft_stage1(plan, z_src, d, z_part)
        q = _spectral_conv(plan, a, ah, ss, order)
        last = order == HYENA_ORDER - 1
        z_src = _idft_gate(plan, q, u, order, z_src, z_part, skip[order], BF16 if last else F32)
        z_part = 0
    return z_src


def kernel(x, c, ctx, c_ctx, mod_w, mod_b, norm_mix_g, norm_mlp_g, attn_w_in, attn_w_out, attn_sink, attn_q_norm, attn_k_norm, hy_w_in, hy_b_in, hy_conv_w, hy_conv_b, hy_f_w1, hy_f_b1, hy_f_freq1, hy_f_w2, hy_f_b2, hy_f_freq2, hy_f_w3, hy_skip, hy_w_out, hy_b_out, mlp_w1, mlp_w2, final_g):
    batch, n_lat, d = x.shape
    n_ctx = ctx.shape[1]
    depth = mod_w.shape[0]
    assert batch == 1 and n_lat % GRID_W == 0 and n_lat % WINDOW == 0
    last_ctx_layer = 2 * ((depth - 1) // 2)
    rope = _rope_tables(n_lat)
    d_ff = mlp_w1.shape[2]
    mlp_w2_bf = mlp_w2.astype(BF16)

    cond = jnp.zeros((16, d), F32).at[0].set(c[0]).at[1].set(c_ctx)
    mod = _modulation(cond, mod_w, mod_b)

    xs = x[0]
    cs = ctx[0]
    for i in range(depth):
        j = i // 2
        is_attn = i % 2 == 0
        ctx_updated = i < last_ctx_layer
        sh1, sc1, g1, sh2, sc2, g2 = jnp.split(mod[i, 0:1], N_MOD, axis=-1)
        csh1, csc1, cg1, csh2, csc2, cg2 = jnp.split(mod[i, 1:2], N_MOD, axis=-1)
        if is_attn:
            h_lat = _rms_norm(xs, norm_mix_g[i], sh1, sc1)
            h_ctx = _rms_norm(cs, norm_mix_g[i], csh1, csc1)
            qkv = _matmul(h_lat, attn_w_in, layer=j, out_dtype=F32, tm=BIG_TM, name="attn_in")
            q, k, _ = _qkv_prep(qkv, attn_q_norm[j], attn_k_norm[j], rope)
            qkv_c = _matmul(h_ctx, attn_w_in, layer=j, out_dtype=F32, name="attn_in_ctx")
            q_c, k_c, v_c = _qkv_prep(qkv_c, attn_q_norm[j], attn_k_norm[j], None)
            wv_t = attn_w_in[j][:, Q_WIDTH + KV_WIDTH:].T
            vt, vt_c = _project_t(wv_t, h_lat), _project_t(wv_t, h_ctx)
            o = _window_attention(q, k, vt, k_c, vt_c, attn_sink[j])
            o = _global_attention(q, k, vt, k_c, vt_c, o)
            xs = _matmul(o, attn_w_out, layer=j, out_dtype=F32, epilogue=_ep_resid, tm=BIG_TM,
                         extras=(_tile_extra(xs), _row_extra(g1)), name="attn_out")
            if ctx_updated:
                o_c = _context_attention(q_c, k_c, v_c, attn_sink[j])
                cs = _matmul(o_c, attn_w_out, layer=j, out_dtype=F32, epilogue=_ep_resid,
                             extras=(_tile_extra(cs), _row_extra(cg1)), name="attn_out_ctx")
        else:
            fparams = (hy_f_w1[j], hy_f_b1[j], hy_f_freq1[j], hy_f_w2[j], hy_f_b2[j], hy_f_freq2[j], hy_f_w3[j])
            z = _hyena_mixer(xs, (norm_mix_g[i], sh1, sc1), hy_w_in, j, hy_b_in[j], hy_conv_w[j], hy_conv_b[j],
                             fparams, hy_skip[j])
            xs = _matmul(z, hy_w_out, layer=j, out_dtype=F32, epilogue=_ep_resid_bias, tm=BIG_TM,
                         extras=(_tile_extra(xs), _row_extra(g1), _row_extra(hy_b_out[j])), name="hyena_out")
            if ctx_updated:
                z_c = _hyena_mixer(cs, (norm_mix_g[i], csh1, csc1), hy_w_in, j, hy_b_in[j], hy_conv_w[j],
                                   hy_conv_b[j], fparams, hy_skip[j])
                cs = _matmul(z_c, hy_w_out, layer=j, out_dtype=F32, epilogue=_ep_resid_bias,
                             extras=(_tile_extra(cs), _row_extra(cg1), _row_extra(hy_b_out[j])),
                             name="hyena_out_ctx")
        a1 = _norm_matmul(xs, norm_mlp_g[i], sh2, sc2, mlp_w1, layer=i, out_dtype=BF16, epilogue=_ep_relu2,
                          name="mlp_up")
        xs = _matmul(a1, mlp_w2_bf, layer=i, out_dtype=F32, epilogue=_ep_resid, tn=512, tk=d_ff,
                     extras=(_tile_extra(xs), _row_extra(g2)), name="mlp_down")
        if ctx_updated:
            a1c = _norm_matmul(cs, norm_mlp_g[i], csh2, csc2, mlp_w1, layer=i, out_dtype=BF16,
                               epilogue=_ep_relu2, name="mlp_up_ctx")
            cs = _matmul(a1c, mlp_w2_bf, layer=i, out_dtype=F32, epilogue=_ep_resid, tn=512, tk=d_ff,
                         extras=(_tile_extra(cs), _row_extra(cg2)), name="mlp_down_ctx")
    return _rms_norm(xs, final_g, out_dtype=F32)[None]
```

```python
import functools
import math

import numpy as np
import jax
import jax.numpy as jnp
from jax import lax
from jax.experimental import pallas as pl
from jax.experimental.pallas import tpu as pltpu

F32 = jnp.float32
BF16 = jnp.bfloat16

GRID_W = 64
HEAD_DIM = 128
A_Q_HEADS = 8
A_KV_HEADS = 2
B_Q_HEADS = 8
B_KV_HEADS = 2
GROUP = A_Q_HEADS // A_KV_HEADS
WINDOW = 128
ROPE_THETA = 10000.0
AXIS_DIM = HEAD_DIM // 2
Q_WIDTH = (A_Q_HEADS + B_Q_HEADS) * HEAD_DIM
KV_HEADS = A_KV_HEADS + B_KV_HEADS
KV_WIDTH = KV_HEADS * HEAD_DIM
HYENA_ORDER = 2
SHORT_CONV = 3
FILTER_EMB = 33
DECAY_TARGET = 1e-2
FAST_DECAY_PCT = 0.3
SLOW_DECAY_PCT = 1.5
MAX_DECAY = math.log(DECAY_TARGET) / FAST_DECAY_PCT
MIN_DECAY = math.log(DECAY_TARGET) / SLOW_DECAY_PCT
N_MOD = 6
EPS = 1e-6
LOG2_E = math.log2(math.e)
LN_2 = math.log(2.0)

VMEM_LIMIT_BYTES = 56 * 1024 * 1024
LANES = 128
SINGLE_STAGE_MAX_LEN = 512
BIG_TM = 1024


def _params(*sem):
    return pltpu.CompilerParams(dimension_semantics=sem, vmem_limit_bytes=VMEM_LIMIT_BYTES)


def _tile(n, pref, align):
    if n <= pref:
        return n
    t = (pref // align) * align
    while t >= align:
        if n % t == 0:
            return t
        t -= align
    return n


def _mod_body(a_ref, w_ref, b_ref, o_ref):
    a = a_ref[...]
    act = a * (1.0 / (1.0 + jnp.exp(-a)))
    o_ref[...] = jnp.dot(act.astype(BF16), w_ref[...].astype(BF16),
                         preferred_element_type=F32) + b_ref[...]


def _modulation(cond, mod_w, mod_b):
    depth, d, n = mod_w.shape
    r = cond.shape[0]
    tn = _tile(n, 1024, LANES)
    return pl.pallas_call(
        _mod_body,
        grid=(depth, n // tn),
        in_specs=[pl.BlockSpec((r, d), lambda l, j: (0, 0)),
                  pl.BlockSpec((None, d, tn), lambda l, j: (l, 0, j)),
                  pl.BlockSpec((None, 1, tn), lambda l, j: (l, 0, j))],
        out_specs=pl.BlockSpec((None, r, tn), lambda l, j: (l, 0, j)),
        out_shape=jax.ShapeDtypeStruct((depth, r, n), F32),
        compiler_params=_params("parallel", "parallel"),
        name="modulation",
    )(cond, mod_w, mod_b.reshape(depth, 1, n))


def _norm_body(x_ref, g_ref, *rest, modulated):
    x = x_ref[...]
    y = x * lax.rsqrt(jnp.mean(x * x, axis=-1, keepdims=True) + EPS) * g_ref[...]
    if modulated:
        sh_ref, sc_ref, o_ref = rest
        y = y * (1.0 + sc_ref[...]) + sh_ref[...]
    else:
        (o_ref,) = rest
    o_ref[...] = y.astype(o_ref.dtype)


def _rms_norm(x, g, shift=None, scale=None, out_dtype=None):
    t, d = x.shape
    out_dtype = BF16 if out_dtype is None else out_dtype
    tm = _tile(t, 512, 16)
    row = pl.BlockSpec((1, d), lambda i: (0, 0))
    vecs = [g.reshape(1, d)]
    if shift is not None:
        vecs += [shift.reshape(1, d), scale.reshape(1, d)]
    return pl.pallas_call(
        functools.partial(_norm_body, modulated=shift is not None),
        grid=(t // tm,),
        in_specs=[pl.BlockSpec((tm, d), lambda i: (i, 0))] + [row] * len(vecs),
        out_specs=pl.BlockSpec((tm, d), lambda i: (i, 0)),
        out_shape=jax.ShapeDtypeStruct((t, d), out_dtype),
        compiler_params=_params("parallel"),
        name="rms_norm",
    )(x, *vecs)


def _ep_none(acc):
    return acc


def _ep_relu2(acc):
    return jnp.square(jnp.maximum(acc, 0.0))


def _ep_resid(acc, x, g):
    return x + g * acc


def _ep_resid_bias(acc, x, g, b):
    return x + g * (acc + b)


def _ep_gate(acc, x, v, skip):
    return x * (acc + v * skip)


def _mm_body(*refs, nk, epilogue, n_extra):
    a_ref, w_ref = refs[0], refs[1]
    extra = refs[2:2 + n_extra]
    o_ref = refs[2 + n_extra]
    a = a_ref[...].astype(BF16)
    w = w_ref[...].astype(BF16)
    part = jnp.dot(a, w, preferred_element_type=F32)
    if nk == 1:
        o_ref[...] = epilogue(part, *[e[...] for e in extra]).astype(o_ref.dtype)
        return
    acc_ref = refs[3 + n_extra]
    k = pl.program_id(2)

    @pl.when(k == 0)
    def _():
        acc_ref[...] = part

    @pl.when(k > 0)
    def _():
        acc_ref[...] += part

    @pl.when(k == nk - 1)
    def _():
        o_ref[...] = epilogue(acc_ref[...], *[e[...] for e in extra]).astype(o_ref.dtype)


def _weight_spec(w, layer, tk, tn, index):
    if layer is None:
        return pl.BlockSpec((tk, tn), index)
    return pl.BlockSpec((None, tk, tn), lambda *ids: (layer,) + tuple(index(*ids)))


def _matmul(a, w, *, out_dtype, epilogue=_ep_none, extras=(), tm=512, tn=1024, tk=2048,
            w_cols=None, layer=None, name="matmul"):
    m, kdim = a.shape
    n = w.shape[-1] if w_cols is None else w_cols[0]
    tm = _tile(m, tm, 16)
    tn = _tile(n, tn, LANES)
    tk = _tile(kdim, tk, LANES)
    nk = kdim // tk
    wmap = (lambda j: j) if w_cols is None else w_cols[1](tn)
    in_specs = [pl.BlockSpec((tm, tk), lambda i, j, k: (i, k)),
                _weight_spec(w, layer, tk, tn, lambda i, j, k: (k, wmap(j)))]
    in_specs += [fn(tm, tn) for _, fn in extras]
    return pl.pallas_call(
        functools.partial(_mm_body, nk=nk, epilogue=epilogue, n_extra=len(extras)),
        grid=(m // tm, n // tn, nk),
        in_specs=in_specs,
        out_specs=pl.BlockSpec((tm, tn), lambda i, j, k: (i, j)),
        out_shape=jax.ShapeDtypeStruct((m, n), out_dtype),
        scratch_shapes=[pltpu.VMEM((tm, tn), F32)] if nk > 1 else [],
        compiler_params=_params("parallel", "parallel", "arbitrary"),
        name=name,
    )(a, w, *[arr for arr, _ in extras])


def _nmm_body(x_ref, g_ref, sh_ref, sc_ref, w_ref, *rest, epilogue, n_extra):
    extra = rest[:n_extra]
    o_ref, h_ref = rest[n_extra], rest[n_extra + 1]

    @pl.when(pl.program_id(1) == 0)
    def _():
        x = x_ref[...]
        y = x * lax.rsqrt(jnp.mean(x * x, axis=-1, keepdims=True) + EPS) * g_ref[...]
        h_ref[...] = (y * (1.0 + sc_ref[...]) + sh_ref[...]).astype(h_ref.dtype)

    acc = jnp.dot(h_ref[...], w_ref[...].astype(BF16), preferred_element_type=F32)
    o_ref[...] = epilogue(acc, *[e[...] for e in extra]).astype(o_ref.dtype)


def _norm_matmul(x, g, shift, scale, w, *, out_dtype, epilogue=_ep_none, extras=(), tm=1024, tn=1024,
                 layer=None, name="norm_matmul"):
    m, d = x.shape
    n = w.shape[-1]
    tm = _tile(m, tm, 16)
    tn = _tile(n, tn, LANES)
    vec = pl.BlockSpec((1, d), lambda i, j: (0, 0))
    with_k = lambda spec_fn: (lambda bs: pl.BlockSpec(bs.block_shape, lambda i, j: bs.index_map(i, j, 0)))(
        spec_fn(tm, tn))
    return pl.pallas_call(
        functools.partial(_nmm_body, epilogue=epilogue, n_extra=len(extras)),
        grid=(m // tm, n // tn),
        in_specs=[pl.BlockSpec((tm, d), lambda i, j: (i, 0)), vec, vec, vec,
                  _weight_spec(w, layer, d, tn, lambda i, j: (0, j))] + [with_k(fn) for _, fn in extras],
        out_specs=pl.BlockSpec((tm, tn), lambda i, j: (i, j)),
        out_shape=jax.ShapeDtypeStruct((m, n), out_dtype),
        scratch_shapes=[pltpu.VMEM((tm, d), BF16)],
        compiler_params=_params("arbitrary", "arbitrary"),
        name=name,
    )(x, g.reshape(1, d), shift.reshape(1, d), scale.reshape(1, d), w, *[arr for arr, _ in extras])


HALO = 16


def _nmm_conv_body(x_ref, xp_ref, xn_ref, g_ref, sh_ref, sc_ref, w_ref, b_ref, cw_ref, cb_ref, o_ref, h_ref,
                   *, n_row_tiles):
    i = pl.program_id(0)
    tm = x_ref.shape[0]

    @pl.when(pl.program_id(1) == 0)
    def _():
        def norm(ref):
            x = ref[...]
            y = x * lax.rsqrt(jnp.mean(x * x, axis=-1, keepdims=True) + EPS) * g_ref[...]
            return (y * (1.0 + sc_ref[...]) + sh_ref[...]).astype(h_ref.dtype)

        h_ref[:HALO] = norm(xp_ref)
        h_ref[HALO:HALO + tm] = norm(x_ref)
        h_ref[HALO + tm:] = norm(xn_ref)

    u = jnp.dot(h_ref[...], w_ref[...].astype(BF16), preferred_element_type=F32) + b_ref[...]
    before = jnp.where(i > 0, u[:HALO], 0.0)
    after = jnp.where(i + 1 < n_row_tiles, u[HALO + tm:], 0.0)
    u = jnp.concatenate([before, u[HALO:HALO + tm], after], axis=0)
    rows = u.shape[0]
    y = (pltpu.roll(u, 1, 0) * cw_ref[0:1, :] + u * cw_ref[1:2, :]
         + pltpu.roll(u, rows - 1, 0) * cw_ref[2:3, :] + cb_ref[...])
    o_ref[...] = y[HALO:HALO + tm]


def _norm_matmul_conv(x, g, shift, scale, w, layer, b, conv_w, conv_b, *, tm=1024, tn=512, name):
    m, d = x.shape
    n = w.shape[-1]
    tm = _tile(m, tm, HALO)
    tn = _tile(n, tn, LANES)
    hb = tm // HALO
    n_row_tiles = m // tm
    vec = pl.BlockSpec((1, d), lambda i, j: (0, 0))
    col = lambda rows: pl.BlockSpec((rows, tn), lambda i, j: (0, j))
    return pl.pallas_call(
        functools.partial(_nmm_conv_body, n_row_tiles=n_row_tiles),
        grid=(n_row_tiles, n // tn),
        in_specs=[pl.BlockSpec((tm, d), lambda i, j: (i, 0)),
                  pl.BlockSpec((HALO, d), lambda i, j: (jnp.maximum(i * hb - 1, 0), 0)),
                  pl.BlockSpec((HALO, d), lambda i, j: (jnp.minimum((i + 1) * hb, m // HALO - 1), 0)),
                  vec, vec, vec, _weight_spec(w, layer, d, tn, lambda i, j: (0, j)),
                  col(1), col(SHORT_CONV), col(1)],
        out_specs=pl.BlockSpec((tm, tn), lambda i, j: (i, j)),
        out_shape=jax.ShapeDtypeStruct((m, n), F32),
        scratch_shapes=[pltpu.VMEM((tm + 2 * HALO, d), BF16)],
        compiler_params=_params("arbitrary", "arbitrary"),
        name=name,
    )(x, x, x, g.reshape(1, d), shift.reshape(1, d), scale.reshape(1, d), w, b.reshape(1, n), conv_w,
      conv_b.reshape(1, n))


def _row_extra(vec, period_blocks=None):
    vec = vec.reshape(1, -1)
    if period_blocks is None:
        return vec, lambda tm, tn: pl.BlockSpec((1, tn), lambda i, j, k: (0, j))
    return vec, lambda tm, tn: pl.BlockSpec((1, tn), lambda i, j, k: (0, j % period_blocks(tn)))


def _tile_extra(arr, colmap=None):
    cm = colmap if colmap is not None else (lambda tn: (lambda j: j))
    return arr, lambda tm, tn: pl.BlockSpec((tm, tn), lambda i, j, k: (i, cm(tn)(j)))


def _rope_tables(n_tok):
    rows = n_tok // GRID_W
    row = jnp.repeat(jnp.arange(rows, dtype=F32), GRID_W)
    col = jnp.tile(jnp.arange(GRID_W, dtype=F32), rows)
    inv = ROPE_THETA ** (-jnp.arange(0, AXIS_DIM, 2, dtype=F32) / AXIS_DIM)
    ang_r = row[:, None] * inv[None]
    ang_c = col[:, None] * inv[None]
    cos = jnp.concatenate([jnp.cos(ang_r)] * 2 + [jnp.cos(ang_c)] * 2, axis=-1)
    sin = jnp.concatenate([-jnp.sin(ang_r), jnp.sin(ang_r), -jnp.sin(ang_c), jnp.sin(ang_c)], axis=-1)
    return cos, sin


def _prep_body(*refs, rope):
    if rope:
        qkv_ref, qn_ref, kn_ref, cos_ref, sin_ref, q_ref, k_ref, v_ref = refs
        cos, sin = cos_ref[...], sin_ref[...]
        lane = lax.broadcasted_iota(jnp.int32, cos.shape, 1)
        first = (lane % (AXIS_DIM)) < (AXIS_DIM // 2)
    else:
        qkv_ref, qn_ref, kn_ref, q_ref, k_ref, v_ref = refs
    scale = HEAD_DIM ** -0.5
    quarter = AXIS_DIM // 2

    def rot(x):
        if not rope:
            return x
        swapped = jnp.where(first, pltpu.roll(x, HEAD_DIM - quarter, 1), pltpu.roll(x, quarter, 1))
        return x * cos + swapped * sin

    def nrm(x, g):
        return x * lax.rsqrt(jnp.mean(x * x, axis=-1, keepdims=True) + EPS) * g

    for h in range(A_Q_HEADS + B_Q_HEADS):
        sl = slice(h * HEAD_DIM, (h + 1) * HEAD_DIM)
        x = qkv_ref[:, sl]
        if h >= A_Q_HEADS:
            x = nrm(x, qn_ref[...])
        q_ref[:, sl] = (rot(x) * (scale * LOG2_E if h >= A_Q_HEADS else scale)).astype(BF16)
    for h in range(KV_HEADS):
        x = qkv_ref[:, Q_WIDTH + h * HEAD_DIM:Q_WIDTH + (h + 1) * HEAD_DIM]
        if h >= A_KV_HEADS:
            x = nrm(x, kn_ref[...])
        k_ref[:, h * HEAD_DIM:(h + 1) * HEAD_DIM] = rot(x).astype(BF16)
    v_ref[...] = qkv_ref[:, Q_WIDTH + KV_WIDTH:].astype(BF16)


def _qkv_prep(qkv, q_norm, k_norm, rope):
    t = qkv.shape[0]
    tm = _tile(t, 256, 16)
    vec = pl.BlockSpec((1, HEAD_DIM), lambda i: (0, 0))
    in_specs = [pl.BlockSpec((tm, qkv.shape[1]), lambda i: (i, 0)), vec, vec]
    args = [qkv, q_norm.reshape(1, HEAD_DIM), k_norm.reshape(1, HEAD_DIM)]
    if rope is not None:
        in_specs += [pl.BlockSpec((tm, HEAD_DIM), lambda i: (i, 0))] * 2
        args += list(rope)
    return pl.pallas_call(
        functools.partial(_prep_body, rope=rope is not None),
        grid=(t // tm,),
        in_specs=in_specs,
        out_specs=[pl.BlockSpec((tm, Q_WIDTH), lambda i: (i, 0)),
                   pl.BlockSpec((tm, KV_WIDTH), lambda i: (i, 0)),
                   pl.BlockSpec((tm, KV_WIDTH), lambda i: (i, 0))],
        out_shape=[jax.ShapeDtypeStruct((t, Q_WIDTH), BF16),
                   jax.ShapeDtypeStruct((t, KV_WIDTH), BF16),
                   jax.ShapeDtypeStruct((t, KV_WIDTH), BF16)],
        compiler_params=_params("parallel"),
        name="qkv_prep",
    )(*args)


def _stack_heads(q_ref):
    return jnp.concatenate([q_ref[:, g * HEAD_DIM:(g + 1) * HEAD_DIM] for g in range(GROUP)], axis=0)


def _unstack_heads(o_ref, out, t):
    for g in range(GROUP):
        o_ref[:, g * HEAD_DIM:(g + 1) * HEAD_DIM] = out[g * t:(g + 1) * t].astype(o_ref.dtype)


def _sink_column(sink_ref, first, t):
    head = lax.broadcasted_iota(jnp.int32, (GROUP * t, 1), 0) // t
    col = jnp.full((GROUP * t, 1), sink_ref[first], F32)
    for g in range(1, GROUP):
        col = jnp.where(head == g, sink_ref[first + g], col)
    return col


def _qk(q, k):
    return lax.dot_general(q, k, (((1,), (1,)), ((), ())), preferred_element_type=F32)


ONES_ROWS = 16


def _gattn_body(q_ref, kc_ref, vtc_ref, k_ref, vt_ref, o_prev_ref, o_ref, m_ref, acc_ref, sa_ref, sb_ref,
                *, tq, ck, n_chunks):
    del o_prev_ref
    q = _stack_heads(q_ref)
    m_ref[...] = jnp.full(m_ref.shape, -jnp.inf, F32)
    acc_ref[...] = jnp.zeros(acc_ref.shape, F32)

    def scores(c):
        return _qk(k_ref[pl.ds(pl.multiple_of(c * ck, ck), ck), :], q)

    def consume(st, vtb):
        m_old = m_ref[...]
        m_new = jnp.maximum(m_old, jnp.max(st, axis=0, keepdims=True))
        p = jnp.exp2(st - m_new).astype(BF16)
        lhs = jnp.concatenate([vtb, jnp.ones((ONES_ROWS, vtb.shape[1]), BF16)], axis=0)
        acc_ref[...] = jnp.exp2(m_old - m_new) * acc_ref[...] + jnp.dot(lhs, p, preferred_element_type=F32)
        m_ref[...] = m_new

    def values(c):
        return vt_ref[:, pl.ds(pl.multiple_of(c * ck, ck), ck)]

    sa_ref[...] = scores(0)
    consume(_qk(kc_ref[...], q), vtc_ref[...])

    def pair(c2, carry):
        c = 2 * c2
        sb_ref[...] = scores(c + 1)
        consume(sa_ref[...], values(c))
        sa_ref[...] = scores(c + 2)
        consume(sb_ref[...], values(c + 1))
        return carry

    lax.fori_loop(0, n_chunks // 2 - 1, pair, 0)
    last = n_chunks - 2
    sb_ref[...] = scores(last + 1)
    consume(sa_ref[...], values(last))
    consume(sb_ref[...], values(last + 1))
    _unstack_heads(o_ref, (acc_ref[:HEAD_DIM] / acc_ref[HEAD_DIM:HEAD_DIM + 1]).T, tq)


def _global_attention(q, k, vt, k_ctx, vt_ctx, o_prev):
    s, n_ctx = q.shape[0], k_ctx.shape[0]
    tq = _tile(s, 512, LANES)
    ck = _tile(s // 2, 1024, LANES)
    n_chunks = s // ck
    assert n_chunks % 2 == 0
    rows = GROUP * tq
    gw = GROUP * HEAD_DIM
    kspec = lambda n: pl.BlockSpec((n, HEAD_DIM), lambda h, i: (0, A_KV_HEADS + h))
    vspec = lambda n: pl.BlockSpec((HEAD_DIM, n), lambda h, i: (A_KV_HEADS + h, 0))
    return pl.pallas_call(
        functools.partial(_gattn_body, tq=tq, ck=ck, n_chunks=n_chunks),
        grid=(B_KV_HEADS, s // tq),
        in_specs=[pl.BlockSpec((tq, gw), lambda h, i: (i, A_KV_HEADS + h)),
                  kspec(n_ctx), vspec(n_ctx), kspec(s), vspec(s), pl.BlockSpec(memory_space=pl.ANY)],
        out_specs=pl.BlockSpec((tq, gw), lambda h, i: (i, A_KV_HEADS + h)),
        out_shape=jax.ShapeDtypeStruct(o_prev.shape, o_prev.dtype),
        input_output_aliases={5: 0},
        scratch_shapes=[pltpu.VMEM((1, rows), F32), pltpu.VMEM((HEAD_DIM + ONES_ROWS, rows), F32),
                        pltpu.VMEM((ck, rows), F32), pltpu.VMEM((ck, rows), F32)],
        compiler_params=_params("parallel", "parallel"),
        name="global_attention",
    )(q, k_ctx, vt_ctx, k, vt, o_prev)


def _project_t_body(w_ref, h_ref, o_ref):
    o_ref[...] = _qk(w_ref[...].astype(BF16), h_ref[...]).astype(o_ref.dtype)


def _project_t(w_t, h):
    c, d = w_t.shape
    t = h.shape[0]
    tt = _tile(t, 1024, LANES)
    return pl.pallas_call(
        _project_t_body,
        grid=(t // tt,),
        in_specs=[pl.BlockSpec((c, d), lambda i: (0, 0)), pl.BlockSpec((tt, d), lambda i: (i, 0))],
        out_specs=pl.BlockSpec((c, tt), lambda i: (0, i)),
        out_shape=jax.ShapeDtypeStruct((c, t), BF16),
        compiler_params=_params("parallel"),
        name="project_t",
    )(w_t, h)


def _wattn_body(sink_ref, q_ref, kc_ref, vtc_ref, kp_ref, k0_ref, kn_ref, vtp_ref, vt0_ref, vtn_ref, o_ref,
                *, nb, n_ctx):
    h = pl.program_id(0)
    i = pl.program_id(1)
    w = WINDOW
    q = _stack_heads(q_ref)
    kcat = jnp.concatenate([kc_ref[...], kp_ref[...], k0_ref[...], kn_ref[...]], axis=0)
    vtcat = jnp.concatenate([vtc_ref[...], vtp_ref[...], vt0_ref[...], vtn_ref[...]], axis=1)
    st = _qk(kcat, q)
    c = lax.broadcasted_iota(jnp.int32, st.shape, 0) - n_ctx
    r = lax.broadcasted_iota(jnp.int32, st.shape, 1) % w
    lo = jnp.maximum(r, jnp.where(i >= 1, 0, w))
    hi = jnp.minimum(r + 2 * w, jnp.where(i + 1 < nb, 3 * w - 1, 2 * w - 1))
    valid = (c < 0) | ((c >= lo) & (c <= hi))
    st = jnp.where(valid, st, -jnp.inf)
    head = lax.broadcasted_iota(jnp.int32, (1, GROUP * w), 1) // w
    sink = jnp.full((1, GROUP * w), sink_ref[h * GROUP], F32)
    for g in range(1, GROUP):
        sink = jnp.where(head == g, sink_ref[h * GROUP + g], sink)
    m = jnp.maximum(jnp.max(st, axis=0, keepdims=True), sink)
    p = jnp.exp(st - m).astype(BF16)
    lhs = jnp.concatenate([vtcat, jnp.ones((ONES_ROWS, vtcat.shape[1]), BF16)], axis=0)
    acc = jnp.dot(lhs, p, preferred_element_type=F32)
    denom = acc[HEAD_DIM:HEAD_DIM + 1] + jnp.exp(sink - m)
    _unstack_heads(o_ref, (acc[:HEAD_DIM] / denom).T, w)


def _window_attention(q, k, vt, k_ctx, vt_ctx, sink):
    s, n_ctx = q.shape[0], k_ctx.shape[0]
    w = WINDOW
    nb = s // w
    gw = GROUP * HEAD_DIM
    before = lambda i: jnp.maximum(i - 1, 0)
    after = lambda i: jnp.minimum(i + 1, nb - 1)
    kblk = lambda pos: pl.BlockSpec((w, HEAD_DIM), lambda h, i: (pos(i), h))
    vblk = lambda pos: pl.BlockSpec((HEAD_DIM, w), lambda h, i: (h, pos(i)))
    same = lambda i: i
    return pl.pallas_call(
        functools.partial(_wattn_body, nb=nb, n_ctx=n_ctx),
        grid=(A_KV_HEADS, nb),
        in_specs=[pl.BlockSpec(memory_space=pltpu.SMEM),
                  pl.BlockSpec((w, gw), lambda h, i: (i, h)),
                  pl.BlockSpec((n_ctx, HEAD_DIM), lambda h, i: (0, h)),
                  pl.BlockSpec((HEAD_DIM, n_ctx), lambda h, i: (h, 0)),
                  kblk(before), kblk(same), kblk(after), vblk(before), vblk(same), vblk(after)],
        out_specs=pl.BlockSpec((w, gw), lambda h, i: (i, h)),
        out_shape=jax.ShapeDtypeStruct((s, Q_WIDTH), BF16),
        compiler_params=_params("parallel", "parallel"),
        name="window_attention",
    )(sink, q, k_ctx, vt_ctx, k, k, k, vt, vt, vt)


def _cattn_body(sink_ref, q_ref, k_ref, v_ref, o_ref, *, n_ctx):
    h = pl.program_id(0)
    q = _stack_heads(q_ref)
    s = _qk(q, k_ref[...]) * jnp.where(h >= A_KV_HEADS, LN_2, 1.0)
    sink = _sink_column(sink_ref, h * GROUP, n_ctx)
    m = jnp.maximum(jnp.max(s, axis=-1, keepdims=True), sink)
    p = jnp.exp(s - m)
    denom = jnp.sum(p, axis=-1, keepdims=True) + jnp.exp(sink - m)
    out = jnp.dot(p.astype(BF16), v_ref[...], preferred_element_type=F32) / denom
    _unstack_heads(o_ref, out, n_ctx)


def _context_attention(q, k, v, sink):
    n_ctx = q.shape[0]
    gw = GROUP * HEAD_DIM
    sink_all = jnp.concatenate([sink.astype(F32), jnp.full((B_Q_HEADS,), -jnp.inf, F32)])
    kv = pl.BlockSpec((n_ctx, HEAD_DIM), lambda h: (0, h))
    return pl.pallas_call(
        functools.partial(_cattn_body, n_ctx=n_ctx),
        grid=(KV_HEADS,),
        in_specs=[pl.BlockSpec(memory_space=pltpu.SMEM),
                  pl.BlockSpec((n_ctx, gw), lambda h: (0, h)), kv, kv],
        out_specs=pl.BlockSpec((n_ctx, gw), lambda h: (0, h)),
        out_shape=jax.ShapeDtypeStruct((n_ctx, Q_WIDTH), BF16),
        compiler_params=_params("parallel"),
        name="context_attention",
    )(sink_all, q, k, v)


def _filter_body(feat_ref, w1_ref, b1_ref, f1_ref, w2_ref, b2_ref, f2_ref, w3_ref, dl_ref,
                 h_ref, ss_ref, *, n_tok):
    i = pl.program_id(0)
    tl = feat_ref.shape[0]
    mm = lambda a, b: jnp.dot(a.astype(BF16), b.astype(BF16), preferred_element_type=F32)
    hid = jnp.sin(f1_ref[...] * (mm(feat_ref[...], w1_ref[...]) + b1_ref[...]))
    hid = jnp.sin(f2_ref[...] * (mm(hid, w2_ref[...]) + b2_ref[...]))
    h = mm(hid, w3_ref[...])
    t = i * tl + lax.broadcasted_iota(jnp.int32, (tl, 1), 0)
    offs = jnp.abs(t - n_tok // 2).astype(F32) * (2.0 / n_tok)
    h = h * jnp.exp(-offs * dl_ref[...])
    h_ref[...] = h.astype(h_ref.dtype)

    @pl.when(i == 0)
    def _():
        ss_ref[...] = jnp.zeros(ss_ref.shape, F32)

    ss_ref[...] += jnp.sum(h * h, axis=0, keepdims=True)


def _hyena_filters(n_tok, d, w1, b1, fr1, w2, b2, fr2, w3):
    bands = (FILTER_EMB - 1) // 2
    t = jnp.linspace(0.0, 1.0, n_tok, dtype=F32)[:, None]
    wv = 2.0 * math.pi * jnp.arange(n_tok, dtype=F32)[:, None] / n_tok
    f = jnp.linspace(1e-4, bands - 1, bands, dtype=F32)[None]
    feats = jnp.concatenate([t, jnp.cos(f * wv), -jnp.sin(f * wv)], axis=-1)
    deltas = jnp.abs(jnp.linspace(MIN_DECAY, MAX_DECAY, d, dtype=F32))
    hidden = w1.shape[1]
    pad_e, pad_h = LANES - FILTER_EMB, LANES - hidden
    padv = lambda v: jnp.pad(v.reshape(1, hidden), ((0, 0), (0, pad_h)))
    c = HYENA_ORDER * d
    tl = _tile(n_tok, 256, 8)
    full = lambda shape: pl.BlockSpec(shape, lambda i: (0, 0))
    return pl.pallas_call(
        functools.partial(_filter_body, n_tok=n_tok),
        grid=(n_tok // tl,),
        in_specs=[pl.BlockSpec((tl, LANES), lambda i: (i, 0)),
                  full((LANES, LANES)), full((1, LANES)), full((1, LANES)),
                  full((LANES, LANES)), full((1, LANES)), full((1, LANES)),
                  full((LANES, c)), full((1, c))],
        out_specs=[pl.BlockSpec((tl, c), lambda i: (i, 0)), full((1, c))],
        out_shape=[jax.ShapeDtypeStruct((n_tok, c), BF16), jax.ShapeDtypeStruct((1, c), F32)],
        compiler_params=_params("arbitrary"),
        name="hyena_filter",
    )(jnp.pad(feats, ((0, 0), (0, pad_e))),
      jnp.pad(w1, ((0, pad_e), (0, pad_h))), padv(b1), padv(fr1),
      jnp.pad(w2, ((0, pad_h), (0, pad_h))), padv(b2), padv(fr2),
      jnp.pad(w3, ((0, pad_h), (0, 0))), jnp.tile(deltas, HYENA_ORDER).reshape(1, c))


def _lane_table(vals):
    return jnp.broadcast_to(jnp.asarray(vals)[..., None], vals.shape + (LANES,))


SUBLANES = 8
DFT_ROW_GROUP = 16


@functools.lru_cache(maxsize=None)
def _dft_plan(n_tok):
    n = 2 * n_tok
    log = n.bit_length() - 1
    assert 1 << log == n, "sequence length must be a power of two"
    n2 = 1 << (log // 2)
    n1 = n // n2
    assert n1 % 4 == 0 and n2 % DFT_ROW_GROUP == 0
    eye = np.eye(SUBLANES)
    nk1 = n1 // 2 + 1
    nk1_pad = -(-nk1 // SUBLANES) * SUBLANES
    a1 = 2.0 * np.pi * np.outer(np.arange(nk1), np.arange(n1 // 2)) / n1
    f1 = np.kron(np.concatenate([np.cos(a1), -np.sin(a1)], axis=0), eye)
    a2 = 2.0 * np.pi * np.outer(np.arange(n2), np.arange(n2)) / n2
    c2, s2 = np.cos(a2), np.sin(a2)
    g2 = np.block([[c2, s2], [-s2, c2]])
    g2i = np.block([[c2, -s2], [s2, c2]])
    rows = np.arange(n1 // 4, n1 // 4 + n1 // 2)
    a1i = 2.0 * np.pi * np.outer(rows, np.arange(nk1_pad)) / n1
    weight = np.where((np.arange(nk1_pad) == 0) | (np.arange(nk1_pad) == n1 // 2), 1.0, 2.0)
    weight = np.where(np.arange(nk1_pad) < nk1, weight, 0.0) / n
    f1i = np.kron(np.concatenate([np.cos(a1i) * weight, -np.sin(a1i) * weight], axis=1), eye)
    tw = 2.0 * np.pi * np.outer(np.arange(n2), np.arange(nk1)) / n
    tw1 = tw.reshape(n2 // SUBLANES, SUBLANES, nk1).transpose(0, 2, 1).reshape(n2 // SUBLANES, nk1 * SUBLANES)
    f32 = lambda m: m.astype(np.float32)
    return dict(n1=n1, n2=n2, nk1=nk1, nk1_pad=nk1_pad, f1=f32(f1), g2=f32(g2), g2i=f32(g2i), f1i=f32(f1i),
                tw1_cos=f32(np.cos(tw1)), tw1_sin=f32(np.sin(tw1)),
                tw2_cos=f32(np.cos(tw.T)), tw2_sin=f32(np.sin(tw.T)))


def _first_step():
    return (pl.program_id(0) == 0) & (pl.program_id(1) == 0)


def _stage1_body(f_ref, z_ref, c_ref, s_ref, o_ref, fb_ref):
    n1h, g, tc = z_ref.shape
    nk1 = o_ref.shape[1]
    half = nk1 * SUBLANES
    reps = tc // LANES

    @pl.when(_first_step())
    def _():
        fb_ref[...] = f_ref[...].astype(BF16)

    z = z_ref[...].astype(F32)
    re_parts, im_parts = [], []
    for a in range(g // SUBLANES):
        zz = z[:, a * SUBLANES:(a + 1) * SUBLANES, :].reshape(n1h * SUBLANES, tc)
        acc = jnp.dot(fb_ref[...], zz.astype(BF16), preferred_element_type=F32)
        re, im = acc[:half], acc[half:]
        c = jnp.tile(c_ref[a], (1, reps))
        s = jnp.tile(s_ref[a], (1, reps))
        re_parts.append((re * c + im * s).reshape(nk1, SUBLANES, tc))
        im_parts.append((im * c - re * s).reshape(nk1, SUBLANES, tc))
    o_ref[0] = jnp.concatenate(re_parts, axis=1).astype(o_ref.dtype)
    o_ref[1] = jnp.concatenate(im_parts, axis=1).astype(o_ref.dtype)


def _dft_stage1(plan, src, width, part=0):
    n1, n2, nk1 = plan["n1"], plan["n2"], plan["nk1"]
    g = DFT_ROW_GROUP
    tc = _tile(width, 1024, LANES)
    nc = width // tc
    sub = g // SUBLANES
    tw = lambda t: _lane_table(t).reshape(n2 // g, sub, nk1 * SUBLANES, LANES)
    tw_spec = pl.BlockSpec((None, sub, nk1 * SUBLANES, LANES), lambda i, j: (i, 0, 0, 0))
    f = jnp.asarray(plan["f1"])
    return pl.pallas_call(
        _stage1_body,
        grid=(n2 // g, nc),
        in_specs=[pl.BlockSpec(f.shape, lambda i, j: (0, 0)),
                  pl.BlockSpec((n1 // 2, g, tc), lambda i, j: (0, i, part * nc + j)),
                  tw_spec, tw_spec],
        out_specs=pl.BlockSpec((2, nk1, g, tc), lambda i, j: (0, 0, i, j)),
        out_shape=jax.ShapeDtypeStruct((2, nk1, n2, width), BF16),
        scratch_shapes=[pltpu.VMEM(f.shape, BF16)],
        compiler_params=_params("arbitrary", "arbitrary"),
        name="dft_stage1",
    )(f, src.reshape(n1 // 2, n2, src.shape[1]), tw(plan["tw1_cos"]), tw(plan["tw1_sin"]))


def _istage1_body(f_ref, q_ref, x_ref, v_ref, skip_ref, o_ref, fb_ref):
    _, nk1, g, tc = q_ref.shape
    n1h = o_ref.shape[0]
    nk1_pad = f_ref.shape[1] // (2 * SUBLANES)

    @pl.when(_first_step())
    def _():
        fb_ref[...] = f_ref[...].astype(BF16)

    q = q_ref[...].astype(F32)
    q = jnp.concatenate([q, jnp.zeros((2, nk1_pad - nk1, g, tc), F32)], axis=1)
    parts = []
    for a in range(g // SUBLANES):
        qq = q[:, :, a * SUBLANES:(a + 1) * SUBLANES, :].reshape(2 * nk1_pad * SUBLANES, tc)
        y = jnp.dot(fb_ref[...], qq.astype(BF16), preferred_element_type=F32)
        parts.append(y.reshape(n1h, SUBLANES, tc))
    y = jnp.concatenate(parts, axis=1)
    o_ref[...] = (x_ref[...] * (y + v_ref[...] * skip_ref[...])).astype(o_ref.dtype)


def _idft_gate(plan, q, x_src, x_part, v_src, v_part, skip, out_dtype):
    n1, n2, nk1 = plan["n1"], plan["n2"], plan["nk1"]
    d = q.shape[3]
    g = DFT_ROW_GROUP
    tc = _tile(d, 512, LANES)
    nc = d // tc
    f = jnp.asarray(plan["f1i"])
    tok = lambda part: pl.BlockSpec((n1 // 2, g, tc), lambda i, j: (0, i, part * nc + j))
    view = lambda arr: arr.reshape(n1 // 2, n2, arr.shape[1])
    out = pl.pallas_call(
        _istage1_body,
        grid=(n2 // g, nc),
        in_specs=[pl.BlockSpec(f.shape, lambda i, j: (0, 0)),
                  pl.BlockSpec((2, nk1, g, tc), lambda i, j: (0, 0, i, j)),
                  tok(x_part), tok(v_part), pl.BlockSpec((1, tc), lambda i, j: (0, j))],
        out_specs=tok(0),
        out_shape=jax.ShapeDtypeStruct((n1 // 2, n2, d), out_dtype),
        scratch_shapes=[pltpu.VMEM(f.shape, BF16)],
        compiler_params=_params("arbitrary", "arbitrary"),
        name="idft_gate",
    )(f, q, view(x_src), view(v_src), skip.reshape(1, d))
    return out.reshape(n1 // 2 * n2, d)


def _spectral_body(a_ref, ah_ref, ss_ref, g2_ref, g2i_ref, c_ref, s_ref, o_ref):
    n2 = a_ref.shape[1]
    g2 = g2_ref[...].astype(BF16)
    stack = lambda ref: jnp.concatenate([ref[0], ref[1]], axis=0)
    x = jnp.dot(g2, stack(a_ref), preferred_element_type=F32)
    hf = jnp.dot(g2, stack(ah_ref), preferred_element_type=F32)
    xr, xi, hr, hi = x[:n2], x[n2:], hf[:n2], hf[n2:]
    scale = lax.rsqrt(ss_ref[...] + EPS)
    y = jnp.concatenate([(xr * hr - xi * hi) * scale, (xr * hi + xi * hr) * scale], axis=0)
    p = jnp.dot(g2i_ref[...].astype(BF16), y.astype(BF16), preferred_element_type=F32)
    pr, pi = p[:n2], p[n2:]
    reps = pr.shape[1] // LANES
    c = jnp.tile(c_ref[...], (1, reps))
    s = jnp.tile(s_ref[...], (1, reps))
    o_ref[0] = (pr * c - pi * s).astype(o_ref.dtype)
    o_ref[1] = (pr * s + pi * c).astype(o_ref.dtype)


def _spectral_conv(plan, a, ah, ss, order):
    n1, n2, d = a.shape[1:]
    td = _tile(d, 2048, LANES)
    nd = d // td
    blk = lambda off: pl.BlockSpec((2, None, n2, td), lambda k1, j: (0, k1, 0, off + j))
    const = pl.BlockSpec((2 * n2, 2 * n2), lambda k1, j: (0, 0))
    tw = pl.BlockSpec((None, n2, LANES), lambda k1, j: (k1, 0, 0))
    return pl.pallas_call(
        _spectral_body,
        grid=(n1, nd),
        in_specs=[blk(0), blk(order * nd), pl.BlockSpec((1, td), lambda k1, j: (0, order * nd + j)),
                  const, const, tw, tw],
        out_specs=blk(0),
        out_shape=jax.ShapeDtypeStruct((2, n1, n2, d), BF16),
        compiler_params=_params("parallel", "parallel"),
        name="spectral_conv",
    )(a, ah, ss, jnp.asarray(plan["g2"]), jnp.asarray(plan["g2i"]),
      _lane_table(plan["tw2_cos"]), _lane_table(plan["tw2_sin"]))


@functools.lru_cache(maxsize=None)
def _dft_plan_single(n_tok):
    n = 2 * n_tok
    af = 2.0 * np.pi * np.outer(np.arange(n), np.arange(n_tok)) / n
    fwd = np.concatenate([np.cos(af), -np.sin(af)], axis=0)
    ai = 2.0 * np.pi * np.outer(np.arange(n_tok // 2, n_tok // 2 + n_tok), np.arange(n)) / n
    inv = np.concatenate([np.cos(ai), -np.sin(ai)], axis=1) / n
    return dict(n=n, fwd=fwd.astype(np.float32), inv=inv.astype(np.float32))


def _cmul_body(x_ref, h_ref, ss_ref, o_ref):
    n = x_ref.shape[0] // 2
    xr, xi, hr, hi = x_ref[:n], x_ref[n:], h_ref[:n], h_ref[n:]
    scale = lax.rsqrt(ss_ref[...] + EPS)
    o_ref[:n] = ((xr * hr - xi * hi) * scale).astype(o_ref.dtype)
    o_ref[n:] = ((xr * hi + xi * hr) * scale).astype(o_ref.dtype)


def _cmul(x, h, ss, d, order):
    rows = x.shape[0]
    td = _tile(d, 512, LANES)
    nd = d // td
    return pl.pallas_call(
        _cmul_body,
        grid=(nd,),
        in_specs=[pl.BlockSpec((rows, td), lambda j: (0, j)),
                  pl.BlockSpec((rows, td), lambda j: (0, order * nd + j)),
                  pl.BlockSpec((1, td), lambda j: (0, order * nd + j))],
        out_specs=pl.BlockSpec((rows, td), lambda j: (0, j)),
        out_shape=jax.ShapeDtypeStruct((rows, d), BF16),
        compiler_params=_params("parallel"),
        name="spectrum_product",
    )(x, h, ss)


def _hyena_mixer(x, norm, w_in, layer, b_in, conv_w, conv_b, fparams, skip):
    n_tok, d = x.shape
    u = _norm_matmul_conv(x, *norm, w_in, layer, b_in, conv_w, conv_b, name="hyena_in")
    filt, ss = _hyena_filters(n_tok, d, *fparams)

    if n_tok <= SINGLE_STAGE_MAX_LEN:
        plan = _dft_plan_single(n_tok)
        fwd, inv = jnp.asarray(plan["fwd"]), jnp.asarray(plan["inv"])
        part_cols = lambda p: (lambda tn: (lambda j: p * (d // tn) + j))
        hf = _matmul(fwd, filt, out_dtype=F32, tm=2 * plan["n"], tn=512, name="dft_filter")
        z_src, z_cols = u, part_cols(HYENA_ORDER)
        for order in range(HYENA_ORDER):
            x = _matmul(fwd, z_src, out_dtype=F32, tm=2 * plan["n"], tn=512, w_cols=(d, z_cols),
                        name="dft_signal")
            y = _cmul(x, hf, ss, d, order)
            last = order == HYENA_ORDER - 1
            z_src = _matmul(inv, y, out_dtype=BF16 if last else F32, epilogue=_ep_gate, tn=512,
                            extras=(_tile_extra(u, part_cols(order)), _tile_extra(z_src, z_cols),
                                    _row_extra(skip[order])), name="idft_gate")
            z_cols = part_cols(0)
        return z_src

    plan = _dft_plan(n_tok)
    ah = _dft_stage1(plan, filt, HYENA_ORDER * d)
    z_src, z_part = u, HYENA_ORDER
    for order in range(HYENA_ORDER):
        a = _dft_stage1(plan, z_src, d, z_part)
        q = _spectral_conv(plan, a, ah, ss, order)
        last = order == HYENA_ORDER - 1
        z_src = _idft_gate(plan, q, u, order, z_src, z_part, skip[order], BF16 if last else F32)
        z_part = 0
    return z_src


def kernel(x, c, ctx, c_ctx, mod_w, mod_b, norm_mix_g, norm_mlp_g, attn_w_in, attn_w_out, attn_sink, attn_q_norm, attn_k_norm, hy_w_in, hy_b_in, hy_conv_w, hy_conv_b, hy_f_w1, hy_f_b1, hy_f_freq1, hy_f_w2, hy_f_b2, hy_f_freq2, hy_f_w3, hy_skip, hy_w_out, hy_b_out, mlp_w1, mlp_w2, final_g):
    batch, n_lat, d = x.shape
    n_ctx = ctx.shape[1]
    depth = mod_w.shape[0]
    assert batch == 1 and n_lat % GRID_W == 0 and n_lat % WINDOW == 0
    last_ctx_layer = 2 * ((depth - 1) // 2)
    rope = _rope_tables(n_lat)
    d_ff = mlp_w1.shape[2]
    mlp_w2_bf = mlp_w2.astype(BF16)

    cond = jnp.zeros((16, d), F32).at[0].set(c[0]).at[1].set(c_ctx)
    mod = _modulation(cond, mod_w, mod_b)

    xs = x[0]
    cs = ctx[0]
    for i in range(depth):
        j = i // 2
        is_attn = i % 2 == 0
        ctx_updated = i < last_ctx_layer
        sh1, sc1, g1, sh2, sc2, g2 = jnp.split(mod[i, 0:1], N_MOD, axis=-1)
        csh1, csc1, cg1, csh2, csc2, cg2 = jnp.split(mod[i, 1:2], N_MOD, axis=-1)
        if is_attn:
            h_lat = _rms_norm(xs, norm_mix_g[i], sh1, sc1)
            h_ctx = _rms_norm(cs, norm_mix_g[i], csh1, csc1)
            qkv = _matmul(h_lat, attn_w_in, layer=j, out_dtype=F32, tm=BIG_TM, name="attn_in")
            q, k, _ = _qkv_prep(qkv, attn_q_norm[j], attn_k_norm[j], rope)
            qkv_c = _matmul(h_ctx, attn_w_in, layer=j, out_dtype=F32, name="attn_in_ctx")
            q_c, k_c, v_c = _qkv_prep(qkv_c, attn_q_norm[j], attn_k_norm[j], None)
            wv_t = attn_w_in[j][:, Q_WIDTH + KV_WIDTH:].T
            vt, vt_c = _project_t(wv_t, h_lat), _project_t(wv_t, h_ctx)
            o = _window_attention(q, k, vt, k_c, vt_c, attn_sink[j])
            o = _global_attention(q, k, vt, k_c, vt_c, o)
            xs = _matmul(o, attn_w_out, layer=j, out_dtype=F32, epilogue=_ep_resid, tm=BIG_TM,
                         extras=(_tile_extra(xs), _row_extra(g1)), name="attn_out")
            if ctx_updated:
                o_c = _context_attention(q_c, k_c, v_c, attn_sink[j])
                cs = _matmul(o_c, attn_w_out, layer=j, out_dtype=F32, epilogue=_ep_resid,
                             extras=(_tile_extra(cs), _row_extra(cg1)), name="attn_out_ctx")
        else:
            fparams = (hy_f_w1[j], hy_f_b1[j], hy_f_freq1[j], hy_f_w2[j], hy_f_b2[j], hy_f_freq2[j], hy_f_w3[j])
            z = _hyena_mixer(xs, (norm_mix_g[i], sh1, sc1), hy_w_in, j, hy_b_in[j], hy_conv_w[j], hy_conv_b[j],
                             fparams, hy_skip[j])
            xs = _matmul(z, hy_w_out, layer=j, out_dtype=F32, epilogue=_ep_resid_bias, tm=BIG_TM,
                         extras=(_tile_extra(xs), _row_extra(g1), _row_extra(hy_b_out[j])), name="hyena_out")
            if ctx_updated:
                z_c = _hyena_mixer(cs, (norm_mix_g[i], csh1, csc1), hy_w_in, j, hy_b_in[j], hy_conv_w[j],
                                   hy_conv_b[j], fparams, hy_skip[j])
                cs = _matmul(z_c, hy_w_out, layer=j, out_dtype=F32, epilogue=_ep_resid_bias,
                             extras=(_tile_extra(cs), _row_extra(cg1), _row_extra(hy_b_out[j])),
                             name="hyena_out_ctx")
        a1 = _norm_matmul(xs, norm_mlp_g[i], sh2, sc2, mlp_w1, layer=i, out_dtype=BF16, epilogue=_ep_relu2,
                          name="mlp_up")
        xs = _matmul(a1, mlp_w2_bf, layer=i, out_dtype=F32, epilogue=_ep_resid, tn=512, tk=d_ff,
                     extras=(_tile_extra(xs), _row_extra(g2)), name="mlp_down")
        if ctx_updated:
            a1c = _norm_matmul(cs, norm_mlp_g[i], csh2, csc2, mlp_w1, layer=i, out_dtype=BF16,
                               epilogue=_ep_relu2, name="mlp_up_ctx")
            cs = _matmul(a1c, mlp_w2_bf, layer=i, out_dtype=F32, epilogue=_ep_resid, tn=512, tk=d_ff,
                         extras=(_tile_extra(cs), _row_extra(cg2)), name="mlp_down_ctx")
    return _rms_norm(xs, final_g, out_dtype=F32)[None]
```

```python
import functools
import math

import numpy as np
import jax
import jax.numpy as jnp
from jax import lax
from jax.experimental import pallas as pl
from jax.experimental.pallas import tpu as pltpu

F32 = jnp.float32
BF16 = jnp.bfloat16

GRID_W = 64
HEAD_DIM = 128
A_Q_HEADS = 8
A_KV_HEADS = 2
B_Q_HEADS = 8
B_KV_HEADS = 2
GROUP = A_Q_HEADS // A_KV_HEADS
WINDOW = 128
ROPE_THETA = 10000.0
AXIS_DIM = HEAD_DIM // 2
Q_WIDTH = (A_Q_HEADS + B_Q_HEADS) * HEAD_DIM
KV_HEADS = A_KV_HEADS + B_KV_HEADS
KV_WIDTH = KV_HEADS * HEAD_DIM
HYENA_ORDER = 2
SHORT_CONV = 3
FILTER_EMB = 33
DECAY_TARGET = 1e-2
FAST_DECAY_PCT = 0.3
SLOW_DECAY_PCT = 1.5
MAX_DECAY = math.log(DECAY_TARGET) / FAST_DECAY_PCT
MIN_DECAY = math.log(DECAY_TARGET) / SLOW_DECAY_PCT
N_MOD = 6
EPS = 1e-6
LOG2_E = math.log2(math.e)
LN_2 = math.log(2.0)

VMEM_LIMIT_BYTES = 56 * 1024 * 1024
LANES = 128
SINGLE_STAGE_MAX_LEN = 512
BIG_TM = 1024


def _params(*sem):
    return pltpu.CompilerParams(dimension_semantics=sem, vmem_limit_bytes=VMEM_LIMIT_BYTES)


def _tile(n, pref, align):
    if n <= pref:
        return n
    t = (pref // align) * align
    while t >= align:
        if n % t == 0:
            return t
        t -= align
    return n


def _mod_body(a_ref, w_ref, b_ref, o_ref):
    a = a_ref[...]
    act = a * (1.0 / (1.0 + jnp.exp(-a)))
    o_ref[...] = jnp.dot(act.astype(BF16), w_ref[...].astype(BF16),
                         preferred_element_type=F32) + b_ref[...]


def _modulation(cond, mod_w, mod_b):
    depth, d, n = mod_w.shape
    r = cond.shape[0]
    tn = _tile(n, 1024, LANES)
    return pl.pallas_call(
        _mod_body,
        grid=(depth, n // tn),
        in_specs=[pl.BlockSpec((r, d), lambda l, j: (0, 0)),
                  pl.BlockSpec((None, d, tn), lambda l, j: (l, 0, j)),
                  pl.BlockSpec((None, 1, tn), lambda l, j: (l, 0, j))],
        out_specs=pl.BlockSpec((None, r, tn), lambda l, j: (l, 0, j)),
        out_shape=jax.ShapeDtypeStruct((depth, r, n), F32),
        compiler_params=_params("parallel", "parallel"),
        name="modulation",
    )(cond, mod_w, mod_b.reshape(depth, 1, n))


def _norm_body(x_ref, g_ref, *rest, modulated):
    x = x_ref[...]
    y = x * lax.rsqrt(jnp.mean(x * x, axis=-1, keepdims=True) + EPS) * g_ref[...]
    if modulated:
        sh_ref, sc_ref, o_ref = rest
        y = y * (1.0 + sc_ref[...]) + sh_ref[...]
    else:
        (o_ref,) = rest
    o_ref[...] = y.astype(o_ref.dtype)


def _rms_norm(x, g, shift=None, scale=None, out_dtype=None):
    t, d = x.shape
    out_dtype = BF16 if out_dtype is None else out_dtype
    tm = _tile(t, 512, 16)
    row = pl.BlockSpec((1, d), lambda i: (0, 0))
    vecs = [g.reshape(1, d)]
    if shift is not None:
        vecs += [shift.reshape(1, d), scale.reshape(1, d)]
    return pl.pallas_call(
        functools.partial(_norm_body, modulated=shift is not None),
        grid=(t // tm,),
        in_specs=[pl.BlockSpec((tm, d), lambda i: (i, 0))] + [row] * len(vecs),
        out_specs=pl.BlockSpec((tm, d), lambda i: (i, 0)),
        out_shape=jax.ShapeDtypeStruct((t, d), out_dtype),
        compiler_params=_params("parallel"),
        name="rms_norm",
    )(x, *vecs)


def _ep_none(acc):
    return acc


def _ep_relu2(acc):
    return jnp.square(jnp.maximum(acc, 0.0))


def _ep_resid(acc, x, g):
    return x + g * acc


def _ep_resid_bias(acc, x, g, b):
    return x + g * (acc + b)


def _ep_gate(acc, x, v, skip):
    return x * (acc + v * skip)


def _mm_body(*refs, nk, epilogue, n_extra):
    a_ref, w_ref = refs[0], refs[1]
    extra = refs[2:2 + n_extra]
    o_ref = refs[2 + n_extra]
    a = a_ref[...].astype(BF16)
    w = w_ref[...].astype(BF16)
    part = jnp.dot(a, w, preferred_element_type=F32)
    if nk == 1:
        o_ref[...] = epilogue(part, *[e[...] for e in extra]).astype(o_ref.dtype)
        return
    acc_ref = refs[3 + n_extra]
    k = pl.program_id(2)

    @pl.when(k == 0)
    def _():
        acc_ref[...] = part

    @pl.when(k > 0)
    def _():
        acc_ref[...] += part

    @pl.when(k == nk - 1)
    def _():
        o_ref[...] = epilogue(acc_ref[...], *[e[...] for e in extra]).astype(o_ref.dtype)


def _weight_spec(w, layer, tk, tn, index):
    if layer is None:
        return pl.BlockSpec((tk, tn), index)
    return pl.BlockSpec((None, tk, tn), lambda *ids: (layer,) + tuple(index(*ids)))


def _matmul(a, w, *, out_dtype, epilogue=_ep_none, extras=(), tm=512, tn=1024, tk=2048,
            w_cols=None, layer=None, name="matmul"):
    m, kdim = a.shape
    n = w.shape[-1] if w_cols is None else w_cols[0]
    tm = _tile(m, tm, 16)
    tn = _tile(n, tn, LANES)
    tk = _tile(kdim, tk, LANES)
    nk = kdim // tk
    wmap = (lambda j: j) if w_cols is None else w_cols[1](tn)
    in_specs = [pl.BlockSpec((tm, tk), lambda i, j, k: (i, k)),
                _weight_spec(w, layer, tk, tn, lambda i, j, k: (k, wmap(j)))]
    in_specs += [fn(tm, tn) for _, fn in extras]
    return pl.pallas_call(
        functools.partial(_mm_body, nk=nk, epilogue=epilogue, n_extra=len(extras)),
        grid=(m // tm, n // tn, nk),
        in_specs=in_specs,
        out_specs=pl.BlockSpec((tm, tn), lambda i, j, k: (i, j)),
        out_shape=jax.ShapeDtypeStruct((m, n), out_dtype),
        scratch_shapes=[pltpu.VMEM((tm, tn), F32)] if nk > 1 else [],
        compiler_params=_params("parallel", "parallel", "arbitrary"),
        name=name,
    )(a, w, *[arr for arr, _ in extras])


def _nmm_body(x_ref, g_ref, sh_ref, sc_ref, w_ref, *rest, epilogue, n_extra):
    extra = rest[:n_extra]
    o_ref, h_ref = rest[n_extra], rest[n_extra + 1]

    @pl.when(pl.program_id(1) == 0)
    def _():
        x = x_ref[...]
        y = x * lax.rsqrt(jnp.mean(x * x, axis=-1, keepdims=True) + EPS) * g_ref[...]
        h_ref[...] = (y * (1.0 + sc_ref[...]) + sh_ref[...]).astype(h_ref.dtype)

    acc = jnp.dot(h_ref[...], w_ref[...].astype(BF16), preferred_element_type=F32)
    o_ref[...] = epilogue(acc, *[e[...] for e in extra]).astype(o_ref.dtype)


def _norm_matmul(x, g, shift, scale, w, *, out_dtype, epilogue=_ep_none, extras=(), tm=1024, tn=1024,
                 layer=None, name="norm_matmul"):
    m, d = x.shape
    n = w.shape[-1]
    tm = _tile(m, tm, 16)
    tn = _tile(n, tn, LANES)
    vec = pl.BlockSpec((1, d), lambda i, j: (0, 0))
    with_k = lambda spec_fn: (lambda bs: pl.BlockSpec(bs.block_shape, lambda i, j: bs.index_map(i, j, 0)))(
        spec_fn(tm, tn))
    return pl.pallas_call(
        functools.partial(_nmm_body, epilogue=epilogue, n_extra=len(extras)),
        grid=(m // tm, n // tn),
        in_specs=[pl.BlockSpec((tm, d), lambda i, j: (i, 0)), vec, vec, vec,
                  _weight_spec(w, layer, d, tn, lambda i, j: (0, j))] + [with_k(fn) for _, fn in extras],
        out_specs=pl.BlockSpec((tm, tn), lambda i, j: (i, j)),
        out_shape=jax.ShapeDtypeStruct((m, n), out_dtype),
        scratch_shapes=[pltpu.VMEM((tm, d), BF16)],
        compiler_params=_params("arbitrary", "arbitrary"),
        name=name,
    )(x, g.reshape(1, d), shift.reshape(1, d), scale.reshape(1, d), w, *[arr for arr, _ in extras])


HALO = 16


def _nmm_conv_body(x_ref, xp_ref, xn_ref, g_ref, sh_ref, sc_ref, w_ref, b_ref, cw_ref, cb_ref, o_ref, h_ref,
                   *, n_row_tiles):
    i = pl.program_id(0)
    tm = x_ref.shape[0]

    @pl.when(pl.program_id(1) == 0)
    def _():
        def norm(ref):
            x = ref[...]
            y = x * lax.rsqrt(jnp.mean(x * x, axis=-1, keepdims=True) + EPS) * g_ref[...]
            return (y * (1.0 + sc_ref[...]) + sh_ref[...]).astype(h_ref.dtype)

        h_ref[:HALO] = norm(xp_ref)
        h_ref[HALO:HALO + tm] = norm(x_ref)
        h_ref[HALO + tm:] = norm(xn_ref)

    u = jnp.dot(h_ref[...], w_ref[...].astype(BF16), preferred_element_type=F32) + b_ref[...]
    before = jnp.where(i > 0, u[:HALO], 0.0)
    after = jnp.where(i + 1 < n_row_tiles, u[HALO + tm:], 0.0)
    u = jnp.concatenate([before, u[HALO:HALO + tm], after], axis=0)
    rows = u.shape[0]
    y = (pltpu.roll(u, 1, 0) * cw_ref[0:1, :] + u * cw_ref[1:2, :]
         + pltpu.roll(u, rows - 1, 0) * cw_ref[2:3, :] + cb_ref[...])
    o_ref[...] = y[HALO:HALO + tm]


def _norm_matmul_conv(x, g, shift, scale, w, layer, b, conv_w, conv_b, *, tm=1024, tn=512, name):
    m, d = x.shape
    n = w.shape[-1]
    tm = _tile(m, tm, HALO)
    tn = _tile(n, tn, LANES)
    hb = tm // HALO
    n_row_tiles = m // tm
    vec = pl.BlockSpec((1, d), lambda i, j: (0, 0))
    col = lambda rows: pl.BlockSpec((rows, tn), lambda i, j: (0, j))
    return pl.pallas_call(
        functools.partial(_nmm_conv_body, n_row_tiles=n_row_tiles),
        grid=(n_row_tiles, n // tn),
        in_specs=[pl.BlockSpec((tm, d), lambda i, j: (i, 0)),
                  pl.BlockSpec((HALO, d), lambda i, j: (jnp.maximum(i * hb - 1, 0), 0)),
                  pl.BlockSpec((HALO, d), lambda i, j: (jnp.minimum((i + 1) * hb, m // HALO - 1), 0)),
                  vec, vec, vec, _weight_spec(w, layer, d, tn, lambda i, j: (0, j)),
                  col(1), col(SHORT_CONV), col(1)],
        out_specs=pl.BlockSpec((tm, tn), lambda i, j: (i, j)),
        out_shape=jax.ShapeDtypeStruct((m, n), F32),
        scratch_shapes=[pltpu.VMEM((tm + 2 * HALO, d), BF16)],
        compiler_params=_params("arbitrary", "arbitrary"),
        name=name,
    )(x, x, x, g.reshape(1, d), shift.reshape(1, d), scale.reshape(1, d), w, b.reshape(1, n), conv_w,
      conv_b.reshape(1, n))


def _row_extra(vec, period_blocks=None):
    vec = vec.reshape(1, -1)
    if period_blocks is None:
        return vec, lambda tm, tn: pl.BlockSpec((1, tn), lambda i, j, k: (0, j))
    return vec, lambda tm, tn: pl.BlockSpec((1, tn), lambda i, j, k: (0, j % period_blocks(tn)))


def _tile_extra(arr, colmap=None):
    cm = colmap if colmap is not None else (lambda tn: (lambda j: j))
    return arr, lambda tm, tn: pl.BlockSpec((tm, tn), lambda i, j, k: (i, cm(tn)(j)))


def _rope_tables(n_tok):
    rows = n_tok // GRID_W
    row = jnp.repeat(jnp.arange(rows, dtype=F32), GRID_W)
    col = jnp.tile(jnp.arange(GRID_W, dtype=F32), rows)
    inv = ROPE_THETA ** (-jnp.arange(0, AXIS_DIM, 2, dtype=F32) / AXIS_DIM)
    ang_r = row[:, None] * inv[None]
    ang_c = col[:, None] * inv[None]
    cos = jnp.concatenate([jnp.cos(ang_r)] * 2 + [jnp.cos(ang_c)] * 2, axis=-1)
    sin = jnp.concatenate([-jnp.sin(ang_r), jnp.sin(ang_r), -jnp.sin(ang_c), jnp.sin(ang_c)], axis=-1)
    return cos, sin


HEADS_PER_TILE = 8
ATTN_IN_WIDTH = Q_WIDTH + 2 * KV_WIDTH


def _attn_in_body(*refs, rope):
    if rope:
        x_ref, g_ref, sh_ref, sc_ref, w_ref, qn_ref, kn_ref, cos_ref, sin_ref, o_ref, h_ref = refs
        cos, sin = cos_ref[...], sin_ref[...]
        lane = lax.broadcasted_iota(jnp.int32, cos.shape, 1)
        first = (lane % (AXIS_DIM)) < (AXIS_DIM // 2)
    else:
        x_ref, g_ref, sh_ref, sc_ref, w_ref, qn_ref, kn_ref, o_ref, h_ref = refs
    j = pl.program_id(1)
    scale = HEAD_DIM ** -0.5
    quarter = AXIS_DIM // 2

    @pl.when(j == 0)
    def _():
        x = x_ref[...]
        y = x * lax.rsqrt(jnp.mean(x * x, axis=-1, keepdims=True) + EPS) * g_ref[...]
        h_ref[...] = (y * (1.0 + sc_ref[...]) + sh_ref[...]).astype(h_ref.dtype)

    acc = jnp.dot(h_ref[...], w_ref[...].astype(BF16), preferred_element_type=F32)

    def rot(x):
        if not rope:
            return x
        swapped = jnp.where(first, pltpu.roll(x, HEAD_DIM - quarter, 1), pltpu.roll(x, quarter, 1))
        return x * cos + swapped * sin

    def nrm(x, g_ref_):
        return x * lax.rsqrt(jnp.mean(x * x, axis=-1, keepdims=True) + EPS) * g_ref_[...]

    q_a = lambda x: rot(x) * scale
    q_b = lambda x: rot(nrm(x, qn_ref)) * (scale * LOG2_E)
    k_b = lambda x: rot(nrm(x, kn_ref))
    kinds = ([q_a] * A_Q_HEADS + [q_b] * B_Q_HEADS + [rot] * A_KV_HEADS + [k_b] * B_KV_HEADS
             + [lambda x: x] * KV_HEADS)

    def emit(tile_kinds):
        for hh, kind in enumerate(tile_kinds):
            sl = slice(hh * HEAD_DIM, (hh + 1) * HEAD_DIM)
            o_ref[:, sl] = kind(acc[:, sl]).astype(o_ref.dtype)

    for t in range(len(kinds) // HEADS_PER_TILE):
        pl.when(j == t)(functools.partial(emit, kinds[t * HEADS_PER_TILE:(t + 1) * HEADS_PER_TILE]))


def _attn_in(x, g, shift, scale, w, layer, q_norm, k_norm, rope):
    t, d = x.shape
    tm = _tile(t, 512, 16)
    tn = HEADS_PER_TILE * HEAD_DIM
    vec = lambda n: pl.BlockSpec((1, n), lambda i, j: (0, 0))
    in_specs = [pl.BlockSpec((tm, d), lambda i, j: (i, 0)), vec(d), vec(d), vec(d),
                _weight_spec(w, layer, d, tn, lambda i, j: (0, j)), vec(HEAD_DIM), vec(HEAD_DIM)]
    args = [x, g.reshape(1, d), shift.reshape(1, d), scale.reshape(1, d), w,
            q_norm.reshape(1, HEAD_DIM), k_norm.reshape(1, HEAD_DIM)]
    if rope is not None:
        in_specs += [pl.BlockSpec((tm, HEAD_DIM), lambda i, j: (i, 0))] * 2
        args += list(rope)
    return pl.pallas_call(
        functools.partial(_attn_in_body, rope=rope is not None),
        grid=(t // tm, ATTN_IN_WIDTH // tn),
        in_specs=in_specs,
        out_specs=[pl.BlockSpec((tm, tn), lambda i, j: (i, j)), pl.BlockSpec((tm, d), lambda i, j: (i, 0))],
        out_shape=[jax.ShapeDtypeStruct((t, ATTN_IN_WIDTH), BF16), jax.ShapeDtypeStruct((t, d), BF16)],
        compiler_params=_params("arbitrary", "arbitrary"),
        name="attn_in",
    )(*args)


def _stack_heads(q_ref):
    return jnp.concatenate([q_ref[:, g * HEAD_DIM:(g + 1) * HEAD_DIM] for g in range(GROUP)], axis=0)


def _unstack_heads(o_ref, out, t):
    for g in range(GROUP):
        o_ref[:, g * HEAD_DIM:(g + 1) * HEAD_DIM] = out[g * t:(g + 1) * t].astype(o_ref.dtype)


def _sink_column(sink_ref, first, t):
    head = lax.broadcasted_iota(jnp.int32, (GROUP * t, 1), 0) // t
    col = jnp.full((GROUP * t, 1), sink_ref[first], F32)
    for g in range(1, GROUP):
        col = jnp.where(head == g, sink_ref[first + g], col)
    return col


def _qk(q, k):
    return lax.dot_general(q, k, (((1,), (1,)), ((), ())), preferred_element_type=F32)


ONES_ROWS = 16


def _gattn_body(q_ref, kc_ref, vtc_ref, k_ref, vt_ref, o_ref, m_ref, acc_ref, sa_ref, sb_ref,
                *, tq, ck, n_chunks):
    q = _stack_heads(q_ref)
    m_ref[...] = jnp.full(m_ref.shape, -jnp.inf, F32)
    acc_ref[...] = jnp.zeros(acc_ref.shape, F32)

    def scores(c):
        return _qk(k_ref[pl.ds(pl.multiple_of(c * ck, ck), ck), :], q)

    def consume(st, vtb):
        m_old = m_ref[...]
        m_new = jnp.maximum(m_old, jnp.max(st, axis=0, keepdims=True))
        p = jnp.exp2(st - m_new).astype(BF16)
        lhs = jnp.concatenate([vtb, jnp.ones((ONES_ROWS, vtb.shape[1]), BF16)], axis=0)
        acc_ref[...] = jnp.exp2(m_old - m_new) * acc_ref[...] + jnp.dot(lhs, p, preferred_element_type=F32)
        m_ref[...] = m_new

    def values(c):
        return vt_ref[:, pl.ds(pl.multiple_of(c * ck, ck), ck)]

    sa_ref[...] = scores(0)
    consume(_qk(kc_ref[...], q), vtc_ref[...])

    def pair(c2, carry):
        c = 2 * c2
        sb_ref[...] = scores(c + 1)
        consume(sa_ref[...], values(c))
        sa_ref[...] = scores(c + 2)
        consume(sb_ref[...], values(c + 1))
        return carry

    lax.fori_loop(0, n_chunks // 2 - 1, pair, 0)
    last = n_chunks - 2
    sb_ref[...] = scores(last + 1)
    consume(sa_ref[...], values(last))
    consume(sb_ref[...], values(last + 1))
    _unstack_heads(o_ref, (acc_ref[:HEAD_DIM] / acc_ref[HEAD_DIM:HEAD_DIM + 1]).T, tq)


def _attn_out_body(oa_ref, ob_ref, w_ref, x_ref, g_ref, o_ref):
    ka = oa_ref.shape[1]
    w = w_ref[...].astype(BF16)
    acc = (jnp.dot(oa_ref[...], w[:ka], preferred_element_type=F32)
           + jnp.dot(ob_ref[...], w[ka:], preferred_element_type=F32))
    o_ref[...] = x_ref[...] + g_ref[...] * acc


def _attn_out(o_a, o_b, w, layer, x, gate):
    m, ka = o_a.shape
    kb = o_b.shape[1]
    n = w.shape[-1]
    tm = _tile(m, BIG_TM, 16)
    tn = _tile(n, 1024, LANES)
    return pl.pallas_call(
        _attn_out_body,
        grid=(m // tm, n // tn),
        in_specs=[pl.BlockSpec((tm, ka), lambda i, j: (i, 0)), pl.BlockSpec((tm, kb), lambda i, j: (i, 0)),
                  _weight_spec(w, layer, ka + kb, tn, lambda i, j: (0, j)),
                  pl.BlockSpec((tm, tn), lambda i, j: (i, j)), pl.BlockSpec((1, tn), lambda i, j: (0, j))],
        out_specs=pl.BlockSpec((tm, tn), lambda i, j: (i, j)),
        out_shape=jax.ShapeDtypeStruct((m, n), F32),
        compiler_params=_params("parallel", "parallel"),
        name="attn_out",
    )(o_a, o_b, w, x, gate.reshape(1, n))


K_COL0 = Q_WIDTH // HEAD_DIM


def _global_attention(qkv, vt, qkv_ctx, vt_ctx):
    s, n_ctx = qkv.shape[0], qkv_ctx.shape[0]
    tq = _tile(s, 512, LANES)
    ck = _tile(s // 2, 1024, LANES)
    n_chunks = s // ck
    assert n_chunks % 2 == 0
    rows = GROUP * tq
    gw = GROUP * HEAD_DIM
    kspec = lambda n: pl.BlockSpec((n, HEAD_DIM), lambda h, i: (0, K_COL0 + A_KV_HEADS + h))
    vspec = lambda n: pl.BlockSpec((HEAD_DIM, n), lambda h, i: (A_KV_HEADS + h, 0))
    return pl.pallas_call(
        functools.partial(_gattn_body, tq=tq, ck=ck, n_chunks=n_chunks),
        grid=(B_KV_HEADS, s // tq),
        in_specs=[pl.BlockSpec((tq, gw), lambda h, i: (i, A_KV_HEADS + h)),
                  kspec(n_ctx), vspec(n_ctx), kspec(s), vspec(s)],
        out_specs=pl.BlockSpec((tq, gw), lambda h, i: (i, h)),
        out_shape=jax.ShapeDtypeStruct((s, B_Q_HEADS * HEAD_DIM), BF16),
        scratch_shapes=[pltpu.VMEM((1, rows), F32), pltpu.VMEM((HEAD_DIM + ONES_ROWS, rows), F32),
                        pltpu.VMEM((ck, rows), F32), pltpu.VMEM((ck, rows), F32)],
        compiler_params=_params("parallel", "parallel"),
        name="global_attention",
    )(qkv, qkv_ctx, vt_ctx, qkv, vt)


def _project_t_body(w_ref, h_ref, o_ref):
    o_ref[...] = _qk(w_ref[...].astype(BF16), h_ref[...]).astype(o_ref.dtype)


def _project_t(w_t, h):
    c, d = w_t.shape
    t = h.shape[0]
    tt = _tile(t, 1024, LANES)
    return pl.pallas_call(
        _project_t_body,
        grid=(t // tt,),
        in_specs=[pl.BlockSpec((c, d), lambda i: (0, 0)), pl.BlockSpec((tt, d), lambda i: (i, 0))],
        out_specs=pl.BlockSpec((c, tt), lambda i: (0, i)),
        out_shape=jax.ShapeDtypeStruct((c, t), BF16),
        compiler_params=_params("parallel"),
        name="project_t",
    )(w_t, h)


def _wattn_body(sink_ref, q_ref, kc_ref, vtc_ref, kp_ref, k0_ref, kn_ref, vtp_ref, vt0_ref, vtn_ref, o_ref,
                *, nb, n_ctx):
    h = pl.program_id(0)
    i = pl.program_id(1)
    w = WINDOW
    q = _stack_heads(q_ref)
    kcat = jnp.concatenate([kc_ref[...], kp_ref[...], k0_ref[...], kn_ref[...]], axis=0)
    vtcat = jnp.concatenate([vtc_ref[...], vtp_ref[...], vt0_ref[...], vtn_ref[...]], axis=1)
    st = _qk(kcat, q)
    c = lax.broadcasted_iota(jnp.int32, st.shape, 0) - n_ctx
    r = lax.broadcasted_iota(jnp.int32, st.shape, 1) % w
    lo = jnp.maximum(r, jnp.where(i >= 1, 0, w))
    hi = jnp.minimum(r + 2 * w, jnp.where(i + 1 < nb, 3 * w - 1, 2 * w - 1))
    valid = (c < 0) | ((c >= lo) & (c <= hi))
    st = jnp.where(valid, st, -jnp.inf)
    head = lax.broadcasted_iota(jnp.int32, (1, GROUP * w), 1) // w
    sink = jnp.full((1, GROUP * w), sink_ref[h * GROUP], F32)
    for g in range(1, GROUP):
        sink = jnp.where(head == g, sink_ref[h * GROUP + g], sink)
    m = jnp.maximum(jnp.max(st, axis=0, keepdims=True), sink)
    p = jnp.exp(st - m).astype(BF16)
    lhs = jnp.concatenate([vtcat, jnp.ones((ONES_ROWS, vtcat.shape[1]), BF16)], axis=0)
    acc = jnp.dot(lhs, p, preferred_element_type=F32)
    denom = acc[HEAD_DIM:HEAD_DIM + 1] + jnp.exp(sink - m)
    _unstack_heads(o_ref, (acc[:HEAD_DIM] / denom).T, w)


def _window_attention(qkv, vt, qkv_ctx, vt_ctx, sink):
    s, n_ctx = qkv.shape[0], qkv_ctx.shape[0]
    w = WINDOW
    nb = s // w
    gw = GROUP * HEAD_DIM
    before = lambda i: jnp.maximum(i - 1, 0)
    after = lambda i: jnp.minimum(i + 1, nb - 1)
    kblk = lambda pos: pl.BlockSpec((w, HEAD_DIM), lambda h, i: (pos(i), K_COL0 + h))
    vblk = lambda pos: pl.BlockSpec((HEAD_DIM, w), lambda h, i: (h, pos(i)))
    same = lambda i: i
    return pl.pallas_call(
        functools.partial(_wattn_body, nb=nb, n_ctx=n_ctx),
        grid=(A_KV_HEADS, nb),
        in_specs=[pl.BlockSpec(memory_space=pltpu.SMEM),
                  pl.BlockSpec((w, gw), lambda h, i: (i, h)),
                  pl.BlockSpec((n_ctx, HEAD_DIM), lambda h, i: (0, K_COL0 + h)),
                  pl.BlockSpec((HEAD_DIM, n_ctx), lambda h, i: (h, 0)),
                  kblk(before), kblk(same), kblk(after), vblk(before), vblk(same), vblk(after)],
        out_specs=pl.BlockSpec((w, gw), lambda h, i: (i, h)),
        out_shape=jax.ShapeDtypeStruct((s, A_Q_HEADS * HEAD_DIM), BF16),
        compiler_params=_params("parallel", "parallel"),
        name="window_attention",
    )(sink, qkv, qkv_ctx, vt_ctx, qkv, qkv, qkv, vt, vt, vt)


def _cattn_body(sink_ref, q_ref, k_ref, v_ref, o_ref, *, n_ctx):
    h = pl.program_id(0)
    q = _stack_heads(q_ref)
    s = _qk(q, k_ref[...]) * jnp.where(h >= A_KV_HEADS, LN_2, 1.0)
    sink = _sink_column(sink_ref, h * GROUP, n_ctx)
    m = jnp.maximum(jnp.max(s, axis=-1, keepdims=True), sink)
    p = jnp.exp(s - m)
    denom = jnp.sum(p, axis=-1, keepdims=True) + jnp.exp(sink - m)
    out = jnp.dot(p.astype(BF16), v_ref[...], preferred_element_type=F32) / denom
    _unstack_heads(o_ref, out, n_ctx)


def _context_attention(qkv, sink):
    n_ctx = qkv.shape[0]
    gw = GROUP * HEAD_DIM
    sink_all = jnp.concatenate([sink.astype(F32), jnp.full((B_Q_HEADS,), -jnp.inf, F32)])
    kv = lambda col0: pl.BlockSpec((n_ctx, HEAD_DIM), lambda h: (0, col0 + h))
    return pl.pallas_call(
        functools.partial(_cattn_body, n_ctx=n_ctx),
        grid=(KV_HEADS,),
        in_specs=[pl.BlockSpec(memory_space=pltpu.SMEM),
                  pl.BlockSpec((n_ctx, gw), lambda h: (0, h)), kv(K_COL0), kv(K_COL0 + KV_HEADS)],
        out_specs=pl.BlockSpec((n_ctx, gw), lambda h: (0, h)),
        out_shape=jax.ShapeDtypeStruct((n_ctx, Q_WIDTH), BF16),
        compiler_params=_params("parallel"),
        name="context_attention",
    )(sink_all, qkv, qkv, qkv)


def _filter_body(feat_ref, w1_ref, b1_ref, f1_ref, w2_ref, b2_ref, f2_ref, w3_ref, dl_ref,
                 h_ref, ss_ref, *, n_tok):
    i = pl.program_id(0)
    tl = feat_ref.shape[0]
    mm = lambda a, b: jnp.dot(a.astype(BF16), b.astype(BF16), preferred_element_type=F32)
    hid = jnp.sin(f1_ref[...] * (mm(feat_ref[...], w1_ref[...]) + b1_ref[...]))
    hid = jnp.sin(f2_ref[...] * (mm(hid, w2_ref[...]) + b2_ref[...]))
    h = mm(hid, w3_ref[...])
    t = i * tl + lax.broadcasted_iota(jnp.int32, (tl, 1), 0)
    offs = jnp.abs(t - n_tok // 2).astype(F32) * (2.0 / n_tok)
    h = h * jnp.exp(-offs * dl_ref[...])
    h_ref[...] = h.astype(h_ref.dtype)

    @pl.when(i == 0)
    def _():
        ss_ref[...] = jnp.zeros(ss_ref.shape, F32)

    ss_ref[...] += jnp.sum(h * h, axis=0, keepdims=True)


def _hyena_filters(n_tok, d, w1, b1, fr1, w2, b2, fr2, w3):
    bands = (FILTER_EMB - 1) // 2
    t = jnp.linspace(0.0, 1.0, n_tok, dtype=F32)[:, None]
    wv = 2.0 * math.pi * jnp.arange(n_tok, dtype=F32)[:, None] / n_tok
    f = jnp.linspace(1e-4, bands - 1, bands, dtype=F32)[None]
    feats = jnp.concatenate([t, jnp.cos(f * wv), -jnp.sin(f * wv)], axis=-1)
    deltas = jnp.abs(jnp.linspace(MIN_DECAY, MAX_DECAY, d, dtype=F32))
    hidden = w1.shape[1]
    pad_e, pad_h = LANES - FILTER_EMB, LANES - hidden
    padv = lambda v: jnp.pad(v.reshape(1, hidden), ((0, 0), (0, pad_h)))
    c = HYENA_ORDER * d
    tl = _tile(n_tok, 256, 8)
    full = lambda shape: pl.BlockSpec(shape, lambda i: (0, 0))
    return pl.pallas_call(
        functools.partial(_filter_body, n_tok=n_tok),
        grid=(n_tok // tl,),
        in_specs=[pl.BlockSpec((tl, LANES), lambda i: (i, 0)),
                  full((LANES, LANES)), full((1, LANES)), full((1, LANES)),
                  full((LANES, LANES)), full((1, LANES)), full((1, LANES)),
                  full((LANES, c)), full((1, c))],
        out_specs=[pl.BlockSpec((tl, c), lambda i: (i, 0)), full((1, c))],
        out_shape=[jax.ShapeDtypeStruct((n_tok, c), BF16), jax.ShapeDtypeStruct((1, c), F32)],
        compiler_params=_params("arbitrary"),
        name="hyena_filter",
    )(jnp.pad(feats, ((0, 0), (0, pad_e))),
      jnp.pad(w1, ((0, pad_e), (0, pad_h))), padv(b1), padv(fr1),
      jnp.pad(w2, ((0, pad_h), (0, pad_h))), padv(b2), padv(fr2),
      jnp.pad(w3, ((0, pad_h), (0, 0))), jnp.tile(deltas, HYENA_ORDER).reshape(1, c))


def _lane_table(vals):
    return jnp.broadcast_to(jnp.asarray(vals)[..., None], vals.shape + (LANES,))


SUBLANES = 8
DFT_ROW_GROUP = 16


@functools.lru_cache(maxsize=None)
def _dft_plan(n_tok):
    n = 2 * n_tok
    log = n.bit_length() - 1
    assert 1 << log == n, "sequence length must be a power of two"
    n2 = 1 << (log // 2)
    n1 = n // n2
    assert n1 % 4 == 0 and n2 % DFT_ROW_GROUP == 0
    eye = np.eye(SUBLANES)
    nk1 = n1 // 2 + 1
    nk1_pad = -(-nk1 // SUBLANES) * SUBLANES
    a1 = 2.0 * np.pi * np.outer(np.arange(nk1), np.arange(n1 // 2)) / n1
    f1 = np.kron(np.concatenate([np.cos(a1), -np.sin(a1)], axis=0), eye)
    a2 = 2.0 * np.pi * np.outer(np.arange(n2), np.arange(n2)) / n2
    c2, s2 = np.cos(a2), np.sin(a2)
    g2 = np.block([[c2, s2], [-s2, c2]])
    g2i = np.block([[c2, -s2], [s2, c2]])
    rows = np.arange(n1 // 4, n1 // 4 + n1 // 2)
    a1i = 2.0 * np.pi * np.outer(rows, np.arange(nk1_pad)) / n1
    weight = np.where((np.arange(nk1_pad) == 0) | (np.arange(nk1_pad) == n1 // 2), 1.0, 2.0)
    weight = np.where(np.arange(nk1_pad) < nk1, weight, 0.0) / n
    f1i = np.kron(np.concatenate([np.cos(a1i) * weight, -np.sin(a1i) * weight], axis=1), eye)
    tw = 2.0 * np.pi * np.outer(np.arange(n2), np.arange(nk1)) / n
    tw1 = tw.reshape(n2 // SUBLANES, SUBLANES, nk1).transpose(0, 2, 1).reshape(n2 // SUBLANES, nk1 * SUBLANES)
    f32 = lambda m: m.astype(np.float32)
    return dict(n1=n1, n2=n2, nk1=nk1, nk1_pad=nk1_pad, f1=f32(f1), g2=f32(g2), g2i=f32(g2i), f1i=f32(f1i),
                tw1_cos=f32(np.cos(tw1)), tw1_sin=f32(np.sin(tw1)),
                tw2_cos=f32(np.cos(tw.T)), tw2_sin=f32(np.sin(tw.T)))


def _first_step():
    return (pl.program_id(0) == 0) & (pl.program_id(1) == 0)


def _stage1_body(f_ref, z_ref, c_ref, s_ref, o_ref, fb_ref):
    n1h, g, tc = z_ref.shape
    nk1 = o_ref.shape[1]
    half = nk1 * SUBLANES
    reps = tc // LANES

    @pl.when(_first_step())
    def _():
        fb_ref[...] = f_ref[...].astype(BF16)

    z = z_ref[...].astype(F32)
    re_parts, im_parts = [], []
    for a in range(g // SUBLANES):
        zz = z[:, a * SUBLANES:(a + 1) * SUBLANES, :].reshape(n1h * SUBLANES, tc)
        acc = jnp.dot(fb_ref[...], zz.astype(BF16), preferred_element_type=F32)
        re, im = acc[:half], acc[half:]
        c = jnp.tile(c_ref[a], (1, reps))
        s = jnp.tile(s_ref[a], (1, reps))
        re_parts.append((re * c + im * s).reshape(nk1, SUBLANES, tc))
        im_parts.append((im * c - re * s).reshape(nk1, SUBLANES, tc))
    o_ref[0] = jnp.concatenate(re_parts, axis=1).astype(o_ref.dtype)
    o_ref[1] = jnp.concatenate(im_parts, axis=1).astype(o_ref.dtype)


def _dft_stage1(plan, src, width, part=0):
    n1, n2, nk1 = plan["n1"], plan["n2"], plan["nk1"]
    g = DFT_ROW_GROUP
    tc = _tile(width, 1024, LANES)
    nc = width // tc
    sub = g // SUBLANES
    tw = lambda t: _lane_table(t).reshape(n2 // g, sub, nk1 * SUBLANES, LANES)
    tw_spec = pl.BlockSpec((None, sub, nk1 * SUBLANES, LANES), lambda i, j: (i, 0, 0, 0))
    f = jnp.asarray(plan["f1"])
    return pl.pallas_call(
        _stage1_body,
        grid=(n2 // g, nc),
        in_specs=[pl.BlockSpec(f.shape, lambda i, j: (0, 0)),
                  pl.BlockSpec((n1 // 2, g, tc), lambda i, j: (0, i, part * nc + j)),
                  tw_spec, tw_spec],
        out_specs=pl.BlockSpec((2, nk1, g, tc), lambda i, j: (0, 0, i, j)),
        out_shape=jax.ShapeDtypeStruct((2, nk1, n2, width), BF16),
        scratch_shapes=[pltpu.VMEM(f.shape, BF16)],
        compiler_params=_params("arbitrary", "arbitrary"),
        name="dft_stage1",
    )(f, src.reshape(n1 // 2, n2, src.shape[1]), tw(plan["tw1_cos"]), tw(plan["tw1_sin"]))


def _istage1_body(f_ref, q_ref, x_ref, v_ref, skip_ref, o_ref, fb_ref):
    _, nk1, g, tc = q_ref.shape
    n1h = o_ref.shape[0]
    nk1_pad = f_ref.shape[1] // (2 * SUBLANES)

    @pl.when(_first_step())
    def _():
        fb_ref[...] = f_ref[...].astype(BF16)

    q = q_ref[...].astype(F32)
    q = jnp.concatenate([q, jnp.zeros((2, nk1_pad - nk1, g, tc), F32)], axis=1)
    parts = []
    for a in range(g // SUBLANES):
        qq = q[:, :, a * SUBLANES:(a + 1) * SUBLANES, :].reshape(2 * nk1_pad * SUBLANES, tc)
        y = jnp.dot(fb_ref[...], qq.astype(BF16), preferred_element_type=F32)
        parts.append(y.reshape(n1h, SUBLANES, tc))
    y = jnp.concatenate(parts, axis=1)
    o_ref[...] = (x_ref[...] * (y + v_ref[...] * skip_ref[...])).astype(o_ref.dtype)


def _idft_gate(plan, q, x_src, x_part, v_src, v_part, skip, out_dtype):
    n1, n2, nk1 = plan["n1"], plan["n2"], plan["nk1"]
    d = q.shape[3]
    g = DFT_ROW_GROUP
    tc = _tile(d, 512, LANES)
    nc = d // tc
    f = jnp.asarray(plan["f1i"])
    tok = lambda part: pl.BlockSpec((n1 // 2, g, tc), lambda i, j: (0, i, part * nc + j))
    view = lambda arr: arr.reshape(n1 // 2, n2, arr.shape[1])
    out = pl.pallas_call(
        _istage1_body,
        grid=(n2 // g, nc),
        in_specs=[pl.BlockSpec(f.shape, lambda i, j: (0, 0)),
                  pl.BlockSpec((2, nk1, g, tc), lambda i, j: (0, 0, i, j)),
                  tok(x_part), tok(v_part), pl.BlockSpec((1, tc), lambda i, j: (0, j))],
        out_specs=tok(0),
        out_shape=jax.ShapeDtypeStruct((n1 // 2, n2, d), out_dtype),
        scratch_shapes=[pltpu.VMEM(f.shape, BF16)],
        compiler_params=_params("arbitrary", "arbitrary"),
        name="idft_gate",
    )(f, q, view(x_src), view(v_src), skip.reshape(1, d))
    return out.reshape(n1 // 2 * n2, d)


def _spectral_body(a_ref, ah_ref, ss_ref, g2_ref, g2i_ref, c_ref, s_ref, o_ref):
    n2 = a_ref.shape[1]
    g2 = g2_ref[...].astype(BF16)
    stack = lambda ref: jnp.concatenate([ref[0], ref[1]], axis=0)
    x = jnp.dot(g2, stack(a_ref), preferred_element_type=F32)
    hf = jnp.dot(g2, stack(ah_ref), preferred_element_type=F32)
    xr, xi, hr, hi = x[:n2], x[n2:], hf[:n2], hf[n2:]
    scale = lax.rsqrt(ss_ref[...] + EPS)
    y = jnp.concatenate([(xr * hr - xi * hi) * scale, (xr * hi + xi * hr) * scale], axis=0)
    p = jnp.dot(g2i_ref[...].astype(BF16), y.astype(BF16), preferred_element_type=F32)
    pr, pi = p[:n2], p[n2:]
    reps = pr.shape[1] // LANES
    c = jnp.tile(c_ref[...], (1, reps))
    s = jnp.tile(s_ref[...], (1, reps))
    o_ref[0] = (pr * c - pi * s).astype(o_ref.dtype)
    o_ref[1] = (pr * s + pi * c).astype(o_ref.dtype)


def _spectral_conv(plan, a, ah, ss, order):
    n1, n2, d = a.shape[1:]
    td = _tile(d, 2048, LANES)
    nd = d // td
    blk = lambda off: pl.BlockSpec((2, None, n2, td), lambda k1, j: (0, k1, 0, off + j))
    const = pl.BlockSpec((2 * n2, 2 * n2), lambda k1, j: (0, 0))
    tw = pl.BlockSpec((None, n2, LANES), lambda k1, j: (k1, 0, 0))
    return pl.pallas_call(
        _spectral_body,
        grid=(n1, nd),
        in_specs=[blk(0), blk(order * nd), pl.BlockSpec((1, td), lambda k1, j: (0, order * nd + j)),
                  const, const, tw, tw],
        out_specs=blk(0),
        out_shape=jax.ShapeDtypeStruct((2, n1, n2, d), BF16),
        compiler_params=_params("parallel", "parallel"),
        name="spectral_conv",
    )(a, ah, ss, jnp.asarray(plan["g2"]), jnp.asarray(plan["g2i"]),
      _lane_table(plan["tw2_cos"]), _lane_table(plan["tw2_sin"]))


@functools.lru_cache(maxsize=None)
def _dft_plan_single(n_tok):
    n = 2 * n_tok
    af = 2.0 * np.pi * np.outer(np.arange(n), np.arange(n_tok)) / n
    fwd = np.concatenate([np.cos(af), -np.sin(af)], axis=0)
    ai = 2.0 * np.pi * np.outer(np.arange(n_tok // 2, n_tok // 2 + n_tok), np.arange(n)) / n
    inv = np.concatenate([np.cos(ai), -np.sin(ai)], axis=1) / n
    return dict(n=n, fwd=fwd.astype(np.float32), inv=inv.astype(np.float32))


def _cmul_body(x_ref, h_ref, ss_ref, o_ref):
    n = x_ref.shape[0] // 2
    xr, xi, hr, hi = x_ref[:n], x_ref[n:], h_ref[:n], h_ref[n:]
    scale = lax.rsqrt(ss_ref[...] + EPS)
    o_ref[:n] = ((xr * hr - xi * hi) * scale).astype(o_ref.dtype)
    o_ref[n:] = ((xr * hi + xi * hr) * scale).astype(o_ref.dtype)


def _cmul(x, h, ss, d, order):
    rows = x.shape[0]
    td = _tile(d, 512, LANES)
    nd = d // td
    return pl.pallas_call(
        _cmul_body,
        grid=(nd,),
        in_specs=[pl.BlockSpec((rows, td), lambda j: (0, j)),
                  pl.BlockSpec((rows, td), lambda j: (0, order * nd + j)),
                  pl.BlockSpec((1, td), lambda j: (0, order * nd + j))],
        out_specs=pl.BlockSpec((rows, td), lambda j: (0, j)),
        out_shape=jax.ShapeDtypeStruct((rows, d), BF16),
        compiler_params=_params("parallel"),
        name="spectrum_product",
    )(x, h, ss)


def _hyena_mixer(x, norm, w_in, layer, b_in, conv_w, conv_b, fparams, skip):
    n_tok, d = x.shape
    u = _norm_matmul_conv(x, *norm, w_in, layer, b_in, conv_w, conv_b, name="hyena_in")
    filt, ss = _hyena_filters(n_tok, d, *fparams)

    if n_tok <= SINGLE_STAGE_MAX_LEN:
        plan = _dft_plan_single(n_tok)
        fwd, inv = jnp.asarray(plan["fwd"]), jnp.asarray(plan["inv"])
        part_cols = lambda p: (lambda tn: (lambda j: p * (d // tn) + j))
        hf = _matmul(fwd, filt, out_dtype=F32, tm=2 * plan["n"], tn=512, name="dft_filter")
        z_src, z_cols = u, part_cols(HYENA_ORDER)
        for order in range(HYENA_ORDER):
            x = _matmul(fwd, z_src, out_dtype=F32, tm=2 * plan["n"], tn=512, w_cols=(d, z_cols),
                        name="dft_signal")
            y = _cmul(x, hf, ss, d, order)
            last = order == HYENA_ORDER - 1
            z_src = _matmul(inv, y, out_dtype=BF16 if last else F32, epilogue=_ep_gate, tn=512,
                            extras=(_tile_extra(u, part_cols(order)), _tile_extra(z_src, z_cols),
                                    _row_extra(skip[order])), name="idft_gate")
            z_cols = part_cols(0)
        return z_src

    plan = _dft_plan(n_tok)
    ah = _dft_stage1(plan, filt, HYENA_ORDER * d)
    z_src, z_part = u, HYENA_ORDER
    for order in range(HYENA_ORDER):
        a = _dft_stage1(plan, z_src, d, z_part)
        q = _spectral_conv(plan, a, ah, ss, order)
        last = order == HYENA_ORDER - 1
        z_src = _idft_gate(plan, q, u, order, z_src, z_part, skip[order], BF16 if last else F32)
        z_part = 0
    return z_src


def kernel(x, c, ctx, c_ctx, mod_w, mod_b, norm_mix_g, norm_mlp_g, attn_w_in, attn_w_out, attn_sink, attn_q_norm, attn_k_norm, hy_w_in, hy_b_in, hy_conv_w, hy_conv_b, hy_f_w1, hy_f_b1, hy_f_freq1, hy_f_w2, hy_f_b2, hy_f_freq2, hy_f_w3, hy_skip, hy_w_out, hy_b_out, mlp_w1, mlp_w2, final_g):
    batch, n_lat, d = x.shape
    n_ctx = ctx.shape[1]
    depth = mod_w.shape[0]
    assert batch == 1 and n_lat % GRID_W == 0 and n_lat % WINDOW == 0
    last_ctx_layer = 2 * ((depth - 1) // 2)
    rope = _rope_tables(n_lat)
    d_ff = mlp_w1.shape[2]
    mlp_w2_bf = mlp_w2.astype(BF16)

    cond = jnp.zeros((16, d), F32).at[0].set(c[0]).at[1].set(c_ctx)
    mod = _modulation(cond, mod_w, mod_b)

    xs = x[0]
    cs = ctx[0]
    for i in range(depth):
        j = i // 2
        is_attn = i % 2 == 0
        ctx_updated = i < last_ctx_layer
        sh1, sc1, g1, sh2, sc2, g2 = jnp.split(mod[i, 0:1], N_MOD, axis=-1)
        csh1, csc1, cg1, csh2, csc2, cg2 = jnp.split(mod[i, 1:2], N_MOD, axis=-1)
        if is_attn:
            qkv, h_lat = _attn_in(xs, norm_mix_g[i], sh1, sc1, attn_w_in, j, attn_q_norm[j], attn_k_norm[j], rope)
            qkv_c, h_ctx = _attn_in(cs, norm_mix_g[i], csh1, csc1, attn_w_in, j, attn_q_norm[j], attn_k_norm[j],
                                    None)
            wv_t = attn_w_in[j][:, Q_WIDTH + KV_WIDTH:].T
            vt, vt_c = _project_t(wv_t, h_lat), _project_t(wv_t, h_ctx)
            o_a = _window_attention(qkv, vt, qkv_c, vt_c, attn_sink[j])
            o_b = _global_attention(qkv, vt, qkv_c, vt_c)
            xs = _attn_out(o_a, o_b, attn_w_out, j, xs, g1)
            if ctx_updated:
                o_c = _context_attention(qkv_c, attn_sink[j])
                cs = _matmul(o_c, attn_w_out, layer=j, out_dtype=F32, epilogue=_ep_resid,
                             extras=(_tile_extra(cs), _row_extra(cg1)), name="attn_out_ctx")
        else:
            fparams = (hy_f_w1[j], hy_f_b1[j], hy_f_freq1[j], hy_f_w2[j], hy_f_b2[j], hy_f_freq2[j], hy_f_w3[j])
            z = _hyena_mixer(xs, (norm_mix_g[i], sh1, sc1), hy_w_in, j, hy_b_in[j], hy_conv_w[j], hy_conv_b[j],
                             fparams, hy_skip[j])
            xs = _matmul(z, hy_w_out, layer=j, out_dtype=F32, epilogue=_ep_resid_bias, tm=BIG_TM,
                         extras=(_tile_extra(xs), _row_extra(g1), _row_extra(hy_b_out[j])), name="hyena_out")
            if ctx_updated:
                z_c = _hyena_mixer(cs, (norm_mix_g[i], csh1, csc1), hy_w_in, j, hy_b_in[j], hy_conv_w[j],
                                   hy_conv_b[j], fparams, hy_skip[j])
                cs = _matmul(z_c, hy_w_out, layer=j, out_dtype=F32, epilogue=_ep_resid_bias,
                             extras=(_tile_extra(cs), _row_extra(cg1), _row_extra(hy_b_out[j])),
                             name="hyena_out_ctx")
        a1 = _norm_matmul(xs, norm_mlp_g[i], sh2, sc2, mlp_w1, layer=i, out_dtype=BF16, epilogue=_ep_relu2,
                          name="mlp_up")
        xs = _matmul(a1, mlp_w2_bf, layer=i, out_dtype=F32, epilogue=_ep_resid, tn=512, tk=d_ff,
                     extras=(_tile_extra(xs), _row_extra(g2)), name="mlp_down")
        if ctx_updated:
            a1c = _norm_matmul(cs, norm_mlp_g[i], csh2, csc2, mlp_w1, layer=i, out_dtype=BF16,
                               epilogue=_ep_relu2, name="mlp_up_ctx")
            cs = _matmul(a1c, mlp_w2_bf, layer=i, out_dtype=F32, epilogue=_ep_resid, tn=512, tk=d_ff,
                         extras=(_tile_extra(cs), _row_extra(cg2)), name="mlp_down_ctx")
    return _rms_norm(xs, final_g, out_dtype=F32)[None]
```

```python
import functools
import math

import numpy as np
import jax
import jax.numpy as jnp
from jax import lax
from jax.experimental import pallas as pl
from jax.experimental.pallas import tpu as pltpu

F32 = jnp.float32
BF16 = jnp.bfloat16

GRID_W = 64
HEAD_DIM = 128
A_Q_HEADS = 8
A_KV_HEADS = 2
B_Q_HEADS = 8
B_KV_HEADS = 2
GROUP = A_Q_HEADS // A_KV_HEADS
WINDOW = 128
ROPE_THETA = 10000.0
AXIS_DIM = HEAD_DIM // 2
Q_WIDTH = (A_Q_HEADS + B_Q_HEADS) * HEAD_DIM
KV_HEADS = A_KV_HEADS + B_KV_HEADS
KV_WIDTH = KV_HEADS * HEAD_DIM
HYENA_ORDER = 2
SHORT_CONV = 3
FILTER_EMB = 33
DECAY_TARGET = 1e-2
FAST_DECAY_PCT = 0.3
SLOW_DECAY_PCT = 1.5
MAX_DECAY = math.log(DECAY_TARGET) / FAST_DECAY_PCT
MIN_DECAY = math.log(DECAY_TARGET) / SLOW_DECAY_PCT
N_MOD = 6
EPS = 1e-6
LOG2_E = math.log2(math.e)
LN_2 = math.log(2.0)

VMEM_LIMIT_BYTES = 56 * 1024 * 1024
LANES = 128
SINGLE_STAGE_MAX_LEN = 512
BIG_TM = 1024


def _params(*sem):
    return pltpu.CompilerParams(dimension_semantics=sem, vmem_limit_bytes=VMEM_LIMIT_BYTES)


def _tile(n, pref, align):
    if n <= pref:
        return n
    t = (pref // align) * align
    while t >= align:
        if n % t == 0:
            return t
        t -= align
    return n


def _mod_body(a_ref, w_ref, b_ref, o_ref):
    a = a_ref[...]
    act = a * (1.0 / (1.0 + jnp.exp(-a)))
    o_ref[...] = jnp.dot(act.astype(BF16), w_ref[...].astype(BF16),
                         preferred_element_type=F32) + b_ref[...]


def _modulation(cond, mod_w, mod_b):
    depth, d, n = mod_w.shape
    r = cond.shape[0]
    tn = _tile(n, 1024, LANES)
    return pl.pallas_call(
        _mod_body,
        grid=(depth, n // tn),
        in_specs=[pl.BlockSpec((r, d), lambda l, j: (0, 0)),
                  pl.BlockSpec((None, d, tn), lambda l, j: (l, 0, j)),
                  pl.BlockSpec((None, 1, tn), lambda l, j: (l, 0, j))],
        out_specs=pl.BlockSpec((None, r, tn), lambda l, j: (l, 0, j)),
        out_shape=jax.ShapeDtypeStruct((depth, r, n), F32),
        compiler_params=_params("parallel", "parallel"),
        name="modulation",
    )(cond, mod_w, mod_b.reshape(depth, 1, n))


def _norm_body(x_ref, g_ref, *rest, modulated):
    x = x_ref[...]
    y = x * lax.rsqrt(jnp.mean(x * x, axis=-1, keepdims=True) + EPS) * g_ref[...]
    if modulated:
        sh_ref, sc_ref, o_ref = rest
        y = y * (1.0 + sc_ref[...]) + sh_ref[...]
    else:
        (o_ref,) = rest
    o_ref[...] = y.astype(o_ref.dtype)


def _rms_norm(x, g, shift=None, scale=None, out_dtype=None):
    t, d = x.shape
    out_dtype = BF16 if out_dtype is None else out_dtype
    tm = _tile(t, 512, 16)
    row = pl.BlockSpec((1, d), lambda i: (0, 0))
    vecs = [g.reshape(1, d)]
    if shift is not None:
        vecs += [shift.reshape(1, d), scale.reshape(1, d)]
    return pl.pallas_call(
        functools.partial(_norm_body, modulated=shift is not None),
        grid=(t // tm,),
        in_specs=[pl.BlockSpec((tm, d), lambda i: (i, 0))] + [row] * len(vecs),
        out_specs=pl.BlockSpec((tm, d), lambda i: (i, 0)),
        out_shape=jax.ShapeDtypeStruct((t, d), out_dtype),
        compiler_params=_params("parallel"),
        name="rms_norm",
    )(x, *vecs)


def _ep_none(acc):
    return acc


def _ep_relu2(acc):
    return jnp.square(jnp.maximum(acc, 0.0))


def _ep_resid(acc, x, g):
    return x + g * acc


def _ep_resid_bias(acc, x, g, b):
    return x + g * (acc + b)


def _ep_gate(acc, x, v, skip):
    return x * (acc + v * skip)


def _mm_body(*refs, nk, epilogue, n_extra):
    a_ref, w_ref = refs[0], refs[1]
    extra = refs[2:2 + n_extra]
    o_ref = refs[2 + n_extra]
    a = a_ref[...].astype(BF16)
    w = w_ref[...].astype(BF16)
    part = jnp.dot(a, w, preferred_element_type=F32)
    if nk == 1:
        o_ref[...] = epilogue(part, *[e[...] for e in extra]).astype(o_ref.dtype)
        return
    acc_ref = refs[3 + n_extra]
    k = pl.program_id(2)

    @pl.when(k == 0)
    def _():
        acc_ref[...] = part

    @pl.when(k > 0)
    def _():
        acc_ref[...] += part

    @pl.when(k == nk - 1)
    def _():
        o_ref[...] = epilogue(acc_ref[...], *[e[...] for e in extra]).astype(o_ref.dtype)


def _weight_spec(w, layer, tk, tn, index):
    if layer is None:
        return pl.BlockSpec((tk, tn), index)
    return pl.BlockSpec((None, tk, tn), lambda *ids: (layer,) + tuple(index(*ids)))


def _matmul(a, w, *, out_dtype, epilogue=_ep_none, extras=(), tm=512, tn=1024, tk=2048,
            w_cols=None, layer=None, name="matmul"):
    m, kdim = a.shape
    n = w.shape[-1] if w_cols is None else w_cols[0]
    tm = _tile(m, tm, 16)
    tn = _tile(n, tn, LANES)
    tk = _tile(kdim, tk, LANES)
    nk = kdim // tk
    wmap = (lambda j: j) if w_cols is None else w_cols[1](tn)
    in_specs = [pl.BlockSpec((tm, tk), lambda i, j, k: (i, k)),
                _weight_spec(w, layer, tk, tn, lambda i, j, k: (k, wmap(j)))]
    in_specs += [fn(tm, tn) for _, fn in extras]
    return pl.pallas_call(
        functools.partial(_mm_body, nk=nk, epilogue=epilogue, n_extra=len(extras)),
        grid=(m // tm, n // tn, nk),
        in_specs=in_specs,
        out_specs=pl.BlockSpec((tm, tn), lambda i, j, k: (i, j)),
        out_shape=jax.ShapeDtypeStruct((m, n), out_dtype),
        scratch_shapes=[pltpu.VMEM((tm, tn), F32)] if nk > 1 else [],
        compiler_params=_params("parallel", "parallel", "arbitrary"),
        name=name,
    )(a, w, *[arr for arr, _ in extras])


def _nmm_body(x_ref, g_ref, sh_ref, sc_ref, w_ref, *rest, epilogue, n_extra):
    extra = rest[:n_extra]
    o_ref, h_ref = rest[n_extra], rest[n_extra + 1]

    @pl.when(pl.program_id(1) == 0)
    def _():
        x = x_ref[...]
        y = x * lax.rsqrt(jnp.mean(x * x, axis=-1, keepdims=True) + EPS) * g_ref[...]
        h_ref[...] = (y * (1.0 + sc_ref[...]) + sh_ref[...]).astype(h_ref.dtype)

    acc = jnp.dot(h_ref[...], w_ref[...].astype(BF16), preferred_element_type=F32)
    o_ref[...] = epilogue(acc, *[e[...] for e in extra]).astype(o_ref.dtype)


def _norm_matmul(x, g, shift, scale, w, *, out_dtype, epilogue=_ep_none, extras=(), tm=1024, tn=1024,
                 layer=None, name="norm_matmul"):
    m, d = x.shape
    n = w.shape[-1]
    tm = _tile(m, tm, 16)
    tn = _tile(n, tn, LANES)
    vec = pl.BlockSpec((1, d), lambda i, j: (0, 0))
    with_k = lambda spec_fn: (lambda bs: pl.BlockSpec(bs.block_shape, lambda i, j: bs.index_map(i, j, 0)))(
        spec_fn(tm, tn))
    return pl.pallas_call(
        functools.partial(_nmm_body, epilogue=epilogue, n_extra=len(extras)),
        grid=(m // tm, n // tn),
        in_specs=[pl.BlockSpec((tm, d), lambda i, j: (i, 0)), vec, vec, vec,
                  _weight_spec(w, layer, d, tn, lambda i, j: (0, j))] + [with_k(fn) for _, fn in extras],
        out_specs=pl.BlockSpec((tm, tn), lambda i, j: (i, j)),
        out_shape=jax.ShapeDtypeStruct((m, n), out_dtype),
        scratch_shapes=[pltpu.VMEM((tm, d), BF16)],
        compiler_params=_params("arbitrary", "arbitrary"),
        name=name,
    )(x, g.reshape(1, d), shift.reshape(1, d), scale.reshape(1, d), w, *[arr for arr, _ in extras])


HALO = 16


def _nmm_conv_body(x_ref, xp_ref, xn_ref, g_ref, sh_ref, sc_ref, w_ref, b_ref, cw_ref, cb_ref, o_ref, h_ref,
                   *, n_row_tiles):
    i = pl.program_id(0)
    tm = x_ref.shape[0]

    @pl.when(pl.program_id(1) == 0)
    def _():
        def norm(ref):
            x = ref[...]
            y = x * lax.rsqrt(jnp.mean(x * x, axis=-1, keepdims=True) + EPS) * g_ref[...]
            return (y * (1.0 + sc_ref[...]) + sh_ref[...]).astype(h_ref.dtype)

        h_ref[:HALO] = norm(xp_ref)
        h_ref[HALO:HALO + tm] = norm(x_ref)
        h_ref[HALO + tm:] = norm(xn_ref)

    u = jnp.dot(h_ref[...], w_ref[...].astype(BF16), preferred_element_type=F32) + b_ref[...]
    before = jnp.where(i > 0, u[:HALO], 0.0)
    after = jnp.where(i + 1 < n_row_tiles, u[HALO + tm:], 0.0)
    u = jnp.concatenate([before, u[HALO:HALO + tm], after], axis=0)
    rows = u.shape[0]
    y = (pltpu.roll(u, 1, 0) * cw_ref[0:1, :] + u * cw_ref[1:2, :]
         + pltpu.roll(u, rows - 1, 0) * cw_ref[2:3, :] + cb_ref[...])
    o_ref[...] = y[HALO:HALO + tm]


def _norm_matmul_conv(x, g, shift, scale, w, layer, b, conv_w, conv_b, *, tm=1024, tn=512, name):
    m, d = x.shape
    n = w.shape[-1]
    tm = _tile(m, tm, HALO)
    tn = _tile(n, tn, LANES)
    hb = tm // HALO
    n_row_tiles = m // tm
    vec = pl.BlockSpec((1, d), lambda i, j: (0, 0))
    col = lambda rows: pl.BlockSpec((rows, tn), lambda i, j: (0, j))
    return pl.pallas_call(
        functools.partial(_nmm_conv_body, n_row_tiles=n_row_tiles),
        grid=(n_row_tiles, n // tn),
        in_specs=[pl.BlockSpec((tm, d), lambda i, j: (i, 0)),
                  pl.BlockSpec((HALO, d), lambda i, j: (jnp.maximum(i * hb - 1, 0), 0)),
                  pl.BlockSpec((HALO, d), lambda i, j: (jnp.minimum((i + 1) * hb, m // HALO - 1), 0)),
                  vec, vec, vec, _weight_spec(w, layer, d, tn, lambda i, j: (0, j)),
                  col(1), col(SHORT_CONV), col(1)],
        out_specs=pl.BlockSpec((tm, tn), lambda i, j: (i, j)),
        out_shape=jax.ShapeDtypeStruct((m, n), F32),
        scratch_shapes=[pltpu.VMEM((tm + 2 * HALO, d), BF16)],
        compiler_params=_params("arbitrary", "arbitrary"),
        name=name,
    )(x, x, x, g.reshape(1, d), shift.reshape(1, d), scale.reshape(1, d), w, b.reshape(1, n), conv_w,
      conv_b.reshape(1, n))


def _row_extra(vec, period_blocks=None):
    vec = vec.reshape(1, -1)
    if period_blocks is None:
        return vec, lambda tm, tn: pl.BlockSpec((1, tn), lambda i, j, k: (0, j))
    return vec, lambda tm, tn: pl.BlockSpec((1, tn), lambda i, j, k: (0, j % period_blocks(tn)))


def _tile_extra(arr, colmap=None):
    cm = colmap if colmap is not None else (lambda tn: (lambda j: j))
    return arr, lambda tm, tn: pl.BlockSpec((tm, tn), lambda i, j, k: (i, cm(tn)(j)))


def _rope_tables(n_tok):
    rows = n_tok // GRID_W
    row = jnp.repeat(jnp.arange(rows, dtype=F32), GRID_W)
    col = jnp.tile(jnp.arange(GRID_W, dtype=F32), rows)
    inv = ROPE_THETA ** (-jnp.arange(0, AXIS_DIM, 2, dtype=F32) / AXIS_DIM)
    ang_r = row[:, None] * inv[None]
    ang_c = col[:, None] * inv[None]
    cos = jnp.concatenate([jnp.cos(ang_r)] * 2 + [jnp.cos(ang_c)] * 2, axis=-1)
    sin = jnp.concatenate([-jnp.sin(ang_r), jnp.sin(ang_r), -jnp.sin(ang_c), jnp.sin(ang_c)], axis=-1)
    return cos, sin


HEADS_PER_TILE = 8
ATTN_IN_WIDTH = Q_WIDTH + 2 * KV_WIDTH


def _attn_in_body(*refs, rope):
    if rope:
        x_ref, g_ref, sh_ref, sc_ref, w_ref, qn_ref, kn_ref, cos_ref, sin_ref, o_ref, h_ref = refs
        cos, sin = cos_ref[...], sin_ref[...]
        lane = lax.broadcasted_iota(jnp.int32, cos.shape, 1)
        first = (lane % (AXIS_DIM)) < (AXIS_DIM // 2)
    else:
        x_ref, g_ref, sh_ref, sc_ref, w_ref, qn_ref, kn_ref, o_ref, h_ref = refs
    j = pl.program_id(1)
    scale = HEAD_DIM ** -0.5
    quarter = AXIS_DIM // 2

    @pl.when(j == 0)
    def _():
        x = x_ref[...]
        y = x * lax.rsqrt(jnp.mean(x * x, axis=-1, keepdims=True) + EPS) * g_ref[...]
        h_ref[...] = (y * (1.0 + sc_ref[...]) + sh_ref[...]).astype(h_ref.dtype)

    acc = jnp.dot(h_ref[...], w_ref[...].astype(BF16), preferred_element_type=F32)

    def rot(x):
        if not rope:
            return x
        swapped = jnp.where(first, pltpu.roll(x, HEAD_DIM - quarter, 1), pltpu.roll(x, quarter, 1))
        return x * cos + swapped * sin

    def nrm(x, g_ref_):
        return x * lax.rsqrt(jnp.mean(x * x, axis=-1, keepdims=True) + EPS) * g_ref_[...]

    q_a = lambda x: rot(x) * scale
    q_b = lambda x: rot(nrm(x, qn_ref)) * (scale * LOG2_E)
    k_b = lambda x: rot(nrm(x, kn_ref))
    kinds = ([q_a] * A_Q_HEADS + [q_b] * B_Q_HEADS + [rot] * A_KV_HEADS + [k_b] * B_KV_HEADS
             + [lambda x: x] * KV_HEADS)

    def emit(tile_kinds):
        for hh, kind in enumerate(tile_kinds):
            sl = slice(hh * HEAD_DIM, (hh + 1) * HEAD_DIM)
            o_ref[:, sl] = kind(acc[:, sl]).astype(o_ref.dtype)

    for t in range(len(kinds) // HEADS_PER_TILE):
        pl.when(j == t)(functools.partial(emit, kinds[t * HEADS_PER_TILE:(t + 1) * HEADS_PER_TILE]))


def _attn_in(x, g, shift, scale, w, layer, q_norm, k_norm, rope):
    t, d = x.shape
    tm = _tile(t, 512, 16)
    tn = HEADS_PER_TILE * HEAD_DIM
    vec = lambda n: pl.BlockSpec((1, n), lambda i, j: (0, 0))
    in_specs = [pl.BlockSpec((tm, d), lambda i, j: (i, 0)), vec(d), vec(d), vec(d),
                _weight_spec(w, layer, d, tn, lambda i, j: (0, j)), vec(HEAD_DIM), vec(HEAD_DIM)]
    args = [x, g.reshape(1, d), shift.reshape(1, d), scale.reshape(1, d), w,
            q_norm.reshape(1, HEAD_DIM), k_norm.reshape(1, HEAD_DIM)]
    if rope is not None:
        in_specs += [pl.BlockSpec((tm, HEAD_DIM), lambda i, j: (i, 0))] * 2
        args += list(rope)
    return pl.pallas_call(
        functools.partial(_attn_in_body, rope=rope is not None),
        grid=(t // tm, ATTN_IN_WIDTH // tn),
        in_specs=in_specs,
        out_specs=[pl.BlockSpec((tm, tn), lambda i, j: (i, j)), pl.BlockSpec((tm, d), lambda i, j: (i, 0))],
        out_shape=[jax.ShapeDtypeStruct((t, ATTN_IN_WIDTH), BF16), jax.ShapeDtypeStruct((t, d), BF16)],
        compiler_params=_params("arbitrary", "arbitrary"),
        name="attn_in",
    )(*args)


def _stack_heads(q_ref):
    return jnp.concatenate([q_ref[:, g * HEAD_DIM:(g + 1) * HEAD_DIM] for g in range(GROUP)], axis=0)


def _unstack_heads(o_ref, out, t):
    for g in range(GROUP):
        o_ref[:, g * HEAD_DIM:(g + 1) * HEAD_DIM] = out[g * t:(g + 1) * t].astype(o_ref.dtype)


def _sink_column(sink_ref, first, t):
    head = lax.broadcasted_iota(jnp.int32, (GROUP * t, 1), 0) // t
    col = jnp.full((GROUP * t, 1), sink_ref[first], F32)
    for g in range(1, GROUP):
        col = jnp.where(head == g, sink_ref[first + g], col)
    return col


def _qk(q, k):
    return lax.dot_general(q, k, (((1,), (1,)), ((), ())), preferred_element_type=F32)


ONES_ROWS = 16


def _gattn_body(q_ref, kc_ref, vtc_ref, k_ref, vt_ref, o_ref, m_ref, acc_ref, sa_ref, sb_ref,
                *, tq, ck, n_chunks):
    q = _stack_heads(q_ref)
    m_ref[...] = jnp.full(m_ref.shape, -jnp.inf, F32)
    acc_ref[...] = jnp.zeros(acc_ref.shape, F32)

    def scores(c):
        return _qk(k_ref[pl.ds(pl.multiple_of(c * ck, ck), ck), :], q)

    def consume(st, vtb):
        m_old = m_ref[...]
        m_new = jnp.maximum(m_old, jnp.max(st, axis=0, keepdims=True))
        p = jnp.exp2(st - m_new).astype(BF16)
        lhs = jnp.concatenate([vtb, jnp.ones((ONES_ROWS, vtb.shape[1]), BF16)], axis=0)
        acc_ref[...] = jnp.exp2(m_old - m_new) * acc_ref[...] + jnp.dot(lhs, p, preferred_element_type=F32)
        m_ref[...] = m_new

    def values(c):
        return vt_ref[:, pl.ds(pl.multiple_of(c * ck, ck), ck)]

    sa_ref[...] = scores(0)
    consume(_qk(kc_ref[...], q), vtc_ref[...])

    def pair(c2, carry):
        c = 2 * c2
        sb_ref[...] = scores(c + 1)
        consume(sa_ref[...], values(c))
        sa_ref[...] = scores(c + 2)
        consume(sb_ref[...], values(c + 1))
        return carry

    lax.fori_loop(0, n_chunks // 2 - 1, pair, 0)
    last = n_chunks - 2
    sb_ref[...] = scores(last + 1)
    consume(sa_ref[...], values(last))
    consume(sb_ref[...], values(last + 1))
    _unstack_heads(o_ref, (acc_ref[:HEAD_DIM] / acc_ref[HEAD_DIM:HEAD_DIM + 1]).T, tq)


def _attn_out_body(oa_ref, ob_ref, w_ref, x_ref, g_ref, o_ref):
    ka = oa_ref.shape[1]
    w = w_ref[...].astype(BF16)
    acc = (jnp.dot(oa_ref[...], w[:ka], preferred_element_type=F32)
           + jnp.dot(ob_ref[...], w[ka:], preferred_element_type=F32))
    o_ref[...] = x_ref[...] + g_ref[...] * acc


def _attn_out(o_a, o_b, w, layer, x, gate):
    m, ka = o_a.shape
    kb = o_b.shape[1]
    n = w.shape[-1]
    tm = _tile(m, 512, 16)
    tn = _tile(n, 2048, LANES)
    return pl.pallas_call(
        _attn_out_body,
        grid=(m // tm, n // tn),
        in_specs=[pl.BlockSpec((tm, ka), lambda i, j: (i, 0)), pl.BlockSpec((tm, kb), lambda i, j: (i, 0)),
                  _weight_spec(w, layer, ka + kb, tn, lambda i, j: (0, j)),
                  pl.BlockSpec((tm, tn), lambda i, j: (i, j)), pl.BlockSpec((1, tn), lambda i, j: (0, j))],
        out_specs=pl.BlockSpec((tm, tn), lambda i, j: (i, j)),
        out_shape=jax.ShapeDtypeStruct((m, n), F32),
        compiler_params=_params("parallel", "parallel"),
        name="attn_out",
    )(o_a, o_b, w, x, gate.reshape(1, n))


K_COL0 = Q_WIDTH // HEAD_DIM


def _global_attention(qkv, vt, qkv_ctx, vt_ctx):
    s, n_ctx = qkv.shape[0], qkv_ctx.shape[0]
    tq = _tile(s, 512, LANES)
    ck = _tile(s // 2, 1024, LANES)
    n_chunks = s // ck
    assert n_chunks % 2 == 0
    rows = GROUP * tq
    gw = GROUP * HEAD_DIM
    kspec = lambda n: pl.BlockSpec((n, HEAD_DIM), lambda h, i: (0, K_COL0 + A_KV_HEADS + h))
    vspec = lambda n: pl.BlockSpec((HEAD_DIM, n), lambda h, i: (A_KV_HEADS + h, 0))
    return pl.pallas_call(
        functools.partial(_gattn_body, tq=tq, ck=ck, n_chunks=n_chunks),
        grid=(B_KV_HEADS, s // tq),
        in_specs=[pl.BlockSpec((tq, gw), lambda h, i: (i, A_KV_HEADS + h)),
                  kspec(n_ctx), vspec(n_ctx), kspec(s), vspec(s)],
        out_specs=pl.BlockSpec((tq, gw), lambda h, i: (i, h)),
        out_shape=jax.ShapeDtypeStruct((s, B_Q_HEADS * HEAD_DIM), BF16),
        scratch_shapes=[pltpu.VMEM((1, rows), F32), pltpu.VMEM((HEAD_DIM + ONES_ROWS, rows), F32),
                        pltpu.VMEM((ck, rows), F32), pltpu.VMEM((ck, rows), F32)],
        compiler_params=_params("parallel", "parallel"),
        name="global_attention",
    )(qkv, qkv_ctx, vt_ctx, qkv, vt)


def _project_t_body(w_ref, h_ref, o_ref):
    o_ref[...] = _qk(w_ref[...].astype(BF16), h_ref[...]).astype(o_ref.dtype)


def _project_t(w_t, h):
    c, d = w_t.shape
    t = h.shape[0]
    tt = _tile(t, 1024, LANES)
    return pl.pallas_call(
        _project_t_body,
        grid=(t // tt,),
        in_specs=[pl.BlockSpec((c, d), lambda i: (0, 0)), pl.BlockSpec((tt, d), lambda i: (i, 0))],
        out_specs=pl.BlockSpec((c, tt), lambda i: (0, i)),
        out_shape=jax.ShapeDtypeStruct((c, t), BF16),
        compiler_params=_params("parallel"),
        name="project_t",
    )(w_t, h)


WINDOW_BLOCKS = 2


def _wattn_body(sink_ref, q_ref, kc_ref, vtc_ref, kp_ref, k0_ref, kn_ref, vtp_ref, vt0_ref, vtn_ref, o_ref,
                *, n_tiles, n_ctx):
    h = pl.program_id(0)
    i = pl.program_id(1)
    w = WINDOW
    tq = q_ref.shape[0]
    q = _stack_heads(q_ref)
    kcat = jnp.concatenate([kc_ref[...], kp_ref[...], k0_ref[...], kn_ref[...]], axis=0)
    vtcat = jnp.concatenate([vtc_ref[...], vtp_ref[...], vt0_ref[...], vtn_ref[...]], axis=1)
    st = _qk(kcat, q)
    c = lax.broadcasted_iota(jnp.int32, st.shape, 0) - n_ctx
    r = lax.broadcasted_iota(jnp.int32, st.shape, 1) % tq
    lo = jnp.maximum(r, jnp.where(i >= 1, 0, w))
    hi = jnp.minimum(r + 2 * w, jnp.where(i + 1 < n_tiles, tq + 2 * w - 1, tq + w - 1))
    valid = (c < 0) | ((c >= lo) & (c <= hi))
    st = jnp.where(valid, st, -jnp.inf)
    head = lax.broadcasted_iota(jnp.int32, (1, GROUP * tq), 1) // tq
    sink = jnp.full((1, GROUP * tq), sink_ref[h * GROUP], F32)
    for g in range(1, GROUP):
        sink = jnp.where(head == g, sink_ref[h * GROUP + g], sink)
    m = jnp.maximum(jnp.max(st, axis=0, keepdims=True), sink)
    p = jnp.exp(st - m).astype(BF16)
    lhs = jnp.concatenate([vtcat, jnp.ones((ONES_ROWS, vtcat.shape[1]), BF16)], axis=0)
    acc = jnp.dot(lhs, p, preferred_element_type=F32)
    denom = acc[HEAD_DIM:HEAD_DIM + 1] + jnp.exp(sink - m)
    _unstack_heads(o_ref, (acc[:HEAD_DIM] / denom).T, tq)


def _window_attention(qkv, vt, qkv_ctx, vt_ctx, sink):
    s, n_ctx = qkv.shape[0], qkv_ctx.shape[0]
    w = WINDOW
    nb = s // w
    tb = WINDOW_BLOCKS if nb % WINDOW_BLOCKS == 0 else 1
    tq = tb * w
    n_tiles = nb // tb
    gw = GROUP * HEAD_DIM
    before = lambda i: jnp.maximum(i * tb - 1, 0)
    after = lambda i: jnp.minimum((i + 1) * tb, nb - 1)
    kblk = lambda pos: pl.BlockSpec((w, HEAD_DIM), lambda h, i: (pos(i), K_COL0 + h))
    vblk = lambda pos: pl.BlockSpec((HEAD_DIM, w), lambda h, i: (h, pos(i)))
    return pl.pallas_call(
        functools.partial(_wattn_body, n_tiles=n_tiles, n_ctx=n_ctx),
        grid=(A_KV_HEADS, n_tiles),
        in_specs=[pl.BlockSpec(memory_space=pltpu.SMEM),
                  pl.BlockSpec((tq, gw), lambda h, i: (i, h)),
                  pl.BlockSpec((n_ctx, HEAD_DIM), lambda h, i: (0, K_COL0 + h)),
                  pl.BlockSpec((HEAD_DIM, n_ctx), lambda h, i: (h, 0)),
                  kblk(before), pl.BlockSpec((tq, HEAD_DIM), lambda h, i: (i, K_COL0 + h)), kblk(after),
                  vblk(before), pl.BlockSpec((HEAD_DIM, tq), lambda h, i: (h, i)), vblk(after)],
        out_specs=pl.BlockSpec((tq, gw), lambda h, i: (i, h)),
        out_shape=jax.ShapeDtypeStruct((s, A_Q_HEADS * HEAD_DIM), BF16),
        compiler_params=_params("parallel", "parallel"),
        name="window_attention",
    )(sink, qkv, qkv_ctx, vt_ctx, qkv, qkv, qkv, vt, vt, vt)


def _cattn_body(sink_ref, q_ref, k_ref, v_ref, o_ref, *, n_ctx):
    h = pl.program_id(0)
    q = _stack_heads(q_ref)
    s = _qk(q, k_ref[...]) * jnp.where(h >= A_KV_HEADS, LN_2, 1.0)
    sink = _sink_column(sink_ref, h * GROUP, n_ctx)
    m = jnp.maximum(jnp.max(s, axis=-1, keepdims=True), sink)
    p = jnp.exp(s - m)
    denom = jnp.sum(p, axis=-1, keepdims=True) + jnp.exp(sink - m)
    out = jnp.dot(p.astype(BF16), v_ref[...], preferred_element_type=F32) / denom
    _unstack_heads(o_ref, out, n_ctx)


def _context_attention(qkv, sink):
    n_ctx = qkv.shape[0]
    gw = GROUP * HEAD_DIM
    sink_all = jnp.concatenate([sink.astype(F32), jnp.full((B_Q_HEADS,), -jnp.inf, F32)])
    kv = lambda col0: pl.BlockSpec((n_ctx, HEAD_DIM), lambda h: (0, col0 + h))
    return pl.pallas_call(
        functools.partial(_cattn_body, n_ctx=n_ctx),
        grid=(KV_HEADS,),
        in_specs=[pl.BlockSpec(memory_space=pltpu.SMEM),
                  pl.BlockSpec((n_ctx, gw), lambda h: (0, h)), kv(K_COL0), kv(K_COL0 + KV_HEADS)],
        out_specs=pl.BlockSpec((n_ctx, gw), lambda h: (0, h)),
        out_shape=jax.ShapeDtypeStruct((n_ctx, Q_WIDTH), BF16),
        compiler_params=_params("parallel"),
        name="context_attention",
    )(sink_all, qkv, qkv, qkv)


def _filter_body(feat_ref, w1_ref, b1_ref, f1_ref, w2_ref, b2_ref, f2_ref, w3_ref, dl_ref,
                 h_ref, ss_ref, *, n_tok):
    i = pl.program_id(0)
    tl = feat_ref.shape[0]
    mm = lambda a, b: jnp.dot(a.astype(BF16), b.astype(BF16), preferred_element_type=F32)
    hid = jnp.sin(f1_ref[...] * (mm(feat_ref[...], w1_ref[...]) + b1_ref[...]))
    hid = jnp.sin(f2_ref[...] * (mm(hid, w2_ref[...]) + b2_ref[...]))
    h = mm(hid, w3_ref[...])
    t = i * tl + lax.broadcasted_iota(jnp.int32, (tl, 1), 0)
    offs = jnp.abs(t - n_tok // 2).astype(F32) * (2.0 / n_tok)
    h = h * jnp.exp(-offs * dl_ref[...])
    h_ref[...] = h.astype(h_ref.dtype)

    @pl.when(i == 0)
    def _():
        ss_ref[...] = jnp.zeros(ss_ref.shape, F32)

    ss_ref[...] += jnp.sum(h * h, axis=0, keepdims=True)


def _hyena_filters(n_tok, d, w1, b1, fr1, w2, b2, fr2, w3):
    bands = (FILTER_EMB - 1) // 2
    t = jnp.linspace(0.0, 1.0, n_tok, dtype=F32)[:, None]
    wv = 2.0 * math.pi * jnp.arange(n_tok, dtype=F32)[:, None] / n_tok
    f = jnp.linspace(1e-4, bands - 1, bands, dtype=F32)[None]
    feats = jnp.concatenate([t, jnp.cos(f * wv), -jnp.sin(f * wv)], axis=-1)
    deltas = jnp.abs(jnp.linspace(MIN_DECAY, MAX_DECAY, d, dtype=F32))
    hidden = w1.shape[1]
    pad_e, pad_h = LANES - FILTER_EMB, LANES - hidden
    padv = lambda v: jnp.pad(v.reshape(1, hidden), ((0, 0), (0, pad_h)))
    c = HYENA_ORDER * d
    tl = _tile(n_tok, 256, 8)
    full = lambda shape: pl.BlockSpec(shape, lambda i: (0, 0))
    return pl.pallas_call(
        functools.partial(_filter_body, n_tok=n_tok),
        grid=(n_tok // tl,),
        in_specs=[pl.BlockSpec((tl, LANES), lambda i: (i, 0)),
                  full((LANES, LANES)), full((1, LANES)), full((1, LANES)),
                  full((LANES, LANES)), full((1, LANES)), full((1, LANES)),
                  full((LANES, c)), full((1, c))],
        out_specs=[pl.BlockSpec((tl, c), lambda i: (i, 0)), full((1, c))],
        out_shape=[jax.ShapeDtypeStruct((n_tok, c), BF16), jax.ShapeDtypeStruct((1, c), F32)],
        compiler_params=_params("arbitrary"),
        name="hyena_filter",
    )(jnp.pad(feats, ((0, 0), (0, pad_e))),
      jnp.pad(w1, ((0, pad_e), (0, pad_h))), padv(b1), padv(fr1),
      jnp.pad(w2, ((0, pad_h), (0, pad_h))), padv(b2), padv(fr2),
      jnp.pad(w3, ((0, pad_h), (0, 0))), jnp.tile(deltas, HYENA_ORDER).reshape(1, c))


def _lane_table(vals):
    return jnp.broadcast_to(jnp.asarray(vals)[..., None], vals.shape + (LANES,))


SUBLANES = 8
DFT_ROW_GROUP = 16


@functools.lru_cache(maxsize=None)
def _dft_plan(n_tok):
    n = 2 * n_tok
    log = n.bit_length() - 1
    assert 1 << log == n, "sequence length must be a power of two"
    n2 = 1 << (log // 2)
    n1 = n // n2
    assert n1 % 4 == 0 and n2 % DFT_ROW_GROUP == 0
    eye = np.eye(SUBLANES)
    nk1 = n1 // 2 + 1
    nk1_pad = -(-nk1 // SUBLANES) * SUBLANES
    a1 = 2.0 * np.pi * np.outer(np.arange(nk1), np.arange(n1 // 2)) / n1
    f1 = np.kron(np.concatenate([np.cos(a1), -np.sin(a1)], axis=0), eye)
    a2 = 2.0 * np.pi * np.outer(np.arange(n2), np.arange(n2)) / n2
    c2, s2 = np.cos(a2), np.sin(a2)
    g2 = np.block([[c2, s2], [-s2, c2]])
    g2i = np.block([[c2, -s2], [s2, c2]])
    rows = np.arange(n1 // 4, n1 // 4 + n1 // 2)
    a1i = 2.0 * np.pi * np.outer(rows, np.arange(nk1_pad)) / n1
    weight = np.where((np.arange(nk1_pad) == 0) | (np.arange(nk1_pad) == n1 // 2), 1.0, 2.0)
    weight = np.where(np.arange(nk1_pad) < nk1, weight, 0.0) / n
    f1i = np.kron(np.concatenate([np.cos(a1i) * weight, -np.sin(a1i) * weight], axis=1), eye)
    tw = 2.0 * np.pi * np.outer(np.arange(n2), np.arange(nk1)) / n
    tw1 = tw.reshape(n2 // SUBLANES, SUBLANES, nk1).transpose(0, 2, 1).reshape(n2 // SUBLANES, nk1 * SUBLANES)
    f32 = lambda m: m.astype(np.float32)
    return dict(n1=n1, n2=n2, nk1=nk1, nk1_pad=nk1_pad, f1=f32(f1), g2=f32(g2), g2i=f32(g2i), f1i=f32(f1i),
                tw1_cos=f32(np.cos(tw1)), tw1_sin=f32(np.sin(tw1)),
                tw2_cos=f32(np.cos(tw.T)), tw2_sin=f32(np.sin(tw.T)))


def _first_step():
    return (pl.program_id(0) == 0) & (pl.program_id(1) == 0)


def _stage1_body(f_ref, z_ref, c_ref, s_ref, o_ref, fb_ref):
    n1h, g, tc = z_ref.shape
    nk1 = o_ref.shape[1]
    half = nk1 * SUBLANES
    reps = tc // LANES

    @pl.when(_first_step())
    def _():
        fb_ref[...] = f_ref[...].astype(BF16)

    z = z_ref[...].astype(F32)
    re_parts, im_parts = [], []
    for a in range(g // SUBLANES):
        zz = z[:, a * SUBLANES:(a + 1) * SUBLANES, :].reshape(n1h * SUBLANES, tc)
        acc = jnp.dot(fb_ref[...], zz.astype(BF16), preferred_element_type=F32)
        re, im = acc[:half], acc[half:]
        c = jnp.tile(c_ref[a], (1, reps))
        s = jnp.tile(s_ref[a], (1, reps))
        re_parts.append((re * c + im * s).reshape(nk1, SUBLANES, tc))
        im_parts.append((im * c - re * s).reshape(nk1, SUBLANES, tc))
    o_ref[0] = jnp.concatenate(re_parts, axis=1).astype(o_ref.dtype)
    o_ref[1] = jnp.concatenate(im_parts, axis=1).astype(o_ref.dtype)


def _dft_stage1(plan, src, width, part=0):
    n1, n2, nk1 = plan["n1"], plan["n2"], plan["nk1"]
    g = DFT_ROW_GROUP
    tc = _tile(width, 1024, LANES)
    nc = width // tc
    sub = g // SUBLANES
    tw = lambda t: _lane_table(t).reshape(n2 // g, sub, nk1 * SUBLANES, LANES)
    tw_spec = pl.BlockSpec((None, sub, nk1 * SUBLANES, LANES), lambda i, j: (i, 0, 0, 0))
    f = jnp.asarray(plan["f1"])
    return pl.pallas_call(
        _stage1_body,
        grid=(n2 // g, nc),
        in_specs=[pl.BlockSpec(f.shape, lambda i, j: (0, 0)),
                  pl.BlockSpec((n1 // 2, g, tc), lambda i, j: (0, i, part * nc + j)),
                  tw_spec, tw_spec],
        out_specs=pl.BlockSpec((2, nk1, g, tc), lambda i, j: (0, 0, i, j)),
        out_shape=jax.ShapeDtypeStruct((2, nk1, n2, width), BF16),
        scratch_shapes=[pltpu.VMEM(f.shape, BF16)],
        compiler_params=_params("arbitrary", "arbitrary"),
        name="dft_stage1",
    )(f, src.reshape(n1 // 2, n2, src.shape[1]), tw(plan["tw1_cos"]), tw(plan["tw1_sin"]))


def _istage1_body(f_ref, q_ref, x_ref, v_ref, skip_ref, o_ref, fb_ref):
    _, nk1, g, tc = q_ref.shape
    n1h = o_ref.shape[0]
    nk1_pad = f_ref.shape[1] // (2 * SUBLANES)

    @pl.when(_first_step())
    def _():
        fb_ref[...] = f_ref[...].astype(BF16)

    q = q_ref[...].astype(F32)
    q = jnp.concatenate([q, jnp.zeros((2, nk1_pad - nk1, g, tc), F32)], axis=1)
    parts = []
    for a in range(g // SUBLANES):
        qq = q[:, :, a * SUBLANES:(a + 1) * SUBLANES, :].reshape(2 * nk1_pad * SUBLANES, tc)
        y = jnp.dot(fb_ref[...], qq.astype(BF16), preferred_element_type=F32)
        parts.append(y.reshape(n1h, SUBLANES, tc))
    y = jnp.concatenate(parts, axis=1)
    o_ref[...] = (x_ref[...] * (y + v_ref[...] * skip_ref[...])).astype(o_ref.dtype)


def _idft_gate(plan, q, x_src, x_part, v_src, v_part, skip, out_dtype):
    n1, n2, nk1 = plan["n1"], plan["n2"], plan["nk1"]
    d = q.shape[3]
    g = DFT_ROW_GROUP
    tc = _tile(d, 512, LANES)
    nc = d // tc
    f = jnp.asarray(plan["f1i"])
    tok = lambda part: pl.BlockSpec((n1 // 2, g, tc), lambda i, j: (0, i, part * nc + j))
    view = lambda arr: arr.reshape(n1 // 2, n2, arr.shape[1])
    out = pl.pallas_call(
        _istage1_body,
        grid=(n2 // g, nc),
        in_specs=[pl.BlockSpec(f.shape, lambda i, j: (0, 0)),
                  pl.BlockSpec((2, nk1, g, tc), lambda i, j: (0, 0, i, j)),
                  tok(x_part), tok(v_part), pl.BlockSpec((1, tc), lambda i, j: (0, j))],
        out_specs=tok(0),
        out_shape=jax.ShapeDtypeStruct((n1 // 2, n2, d), out_dtype),
        scratch_shapes=[pltpu.VMEM(f.shape, BF16)],
        compiler_params=_params("arbitrary", "arbitrary"),
        name="idft_gate",
    )(f, q, view(x_src), view(v_src), skip.reshape(1, d))
    return out.reshape(n1 // 2 * n2, d)


SPECTRAL_ROWS_PER_STEP = 5


def _spectral_body(a_ref, ah_ref, ss_ref, g2_ref, g2i_ref, c_ref, s_ref, o_ref):
    _, kb, n2, td = a_ref.shape
    g2 = g2_ref[...].astype(BF16)
    g2i = g2i_ref[...].astype(BF16)
    scale = lax.rsqrt(ss_ref[...] + EPS)
    reps = td // LANES
    for b in range(kb):
        stack = lambda ref: jnp.concatenate([ref[0, b], ref[1, b]], axis=0)
        x = jnp.dot(g2, stack(a_ref), preferred_element_type=F32)
        hf = jnp.dot(g2, stack(ah_ref), preferred_element_type=F32)
        xr, xi, hr, hi = x[:n2], x[n2:], hf[:n2], hf[n2:]
        y = jnp.concatenate([(xr * hr - xi * hi) * scale, (xr * hi + xi * hr) * scale], axis=0)
        p = jnp.dot(g2i, y.astype(BF16), preferred_element_type=F32)
        pr, pi = p[:n2], p[n2:]
        c = jnp.tile(c_ref[b], (1, reps))
        s = jnp.tile(s_ref[b], (1, reps))
        o_ref[0, b] = (pr * c - pi * s).astype(o_ref.dtype)
        o_ref[1, b] = (pr * s + pi * c).astype(o_ref.dtype)


def _spectral_conv(plan, a, ah, ss, order):
    n1, n2, d = a.shape[1:]
    td = _tile(d, 1024, LANES)
    nd = d // td
    kb = max(b for b in range(1, SPECTRAL_ROWS_PER_STEP + 1) if n1 % b == 0)
    blk = lambda off: pl.BlockSpec((2, kb, n2, td), lambda k1, j: (0, k1, 0, off + j))
    const = pl.BlockSpec((2 * n2, 2 * n2), lambda k1, j: (0, 0))
    tw = pl.BlockSpec((kb, n2, LANES), lambda k1, j: (k1, 0, 0))
    return pl.pallas_call(
        _spectral_body,
        grid=(n1 // kb, nd),
        in_specs=[blk(0), blk(order * nd), pl.BlockSpec((1, td), lambda k1, j: (0, order * nd + j)),
                  const, const, tw, tw],
        out_specs=blk(0),
        out_shape=jax.ShapeDtypeStruct((2, n1, n2, d), BF16),
        compiler_params=_params("parallel", "parallel"),
        name="spectral_conv",
    )(a, ah, ss, jnp.asarray(plan["g2"]), jnp.asarray(plan["g2i"]),
      _lane_table(plan["tw2_cos"]), _lane_table(plan["tw2_sin"]))


@functools.lru_cache(maxsize=None)
def _dft_plan_single(n_tok):
    n = 2 * n_tok
    af = 2.0 * np.pi * np.outer(np.arange(n), np.arange(n_tok)) / n
    fwd = np.concatenate([np.cos(af), -np.sin(af)], axis=0)
    ai = 2.0 * np.pi * np.outer(np.arange(n_tok // 2, n_tok // 2 + n_tok), np.arange(n)) / n
    inv = np.concatenate([np.cos(ai), -np.sin(ai)], axis=1) / n
    return dict(n=n, fwd=fwd.astype(np.float32), inv=inv.astype(np.float32))


def _cmul_body(x_ref, h_ref, ss_ref, o_ref):
    n = x_ref.shape[0] // 2
    xr, xi, hr, hi = x_ref[:n], x_ref[n:], h_ref[:n], h_ref[n:]
    scale = lax.rsqrt(ss_ref[...] + EPS)
    o_ref[:n] = ((xr * hr - xi * hi) * scale).astype(o_ref.dtype)
    o_ref[n:] = ((xr * hi + xi * hr) * scale).astype(o_ref.dtype)


def _cmul(x, h, ss, d, order):
    rows = x.shape[0]
    td = _tile(d, 512, LANES)
    nd = d // td
    return pl.pallas_call(
        _cmul_body,
        grid=(nd,),
        in_specs=[pl.BlockSpec((rows, td), lambda j: (0, j)),
                  pl.BlockSpec((rows, td), lambda j: (0, order * nd + j)),
                  pl.BlockSpec((1, td), lambda j: (0, order * nd + j))],
        out_specs=pl.BlockSpec((rows, td), lambda j: (0, j)),
        out_shape=jax.ShapeDtypeStruct((rows, d), BF16),
        compiler_params=_params("parallel"),
        name="spectrum_product",
    )(x, h, ss)


def _hyena_mixer(x, norm, w_in, layer, b_in, conv_w, conv_b, fparams, skip):
    n_tok, d = x.shape
    u = _norm_matmul_conv(x, *norm, w_in, layer, b_in, conv_w, conv_b, name="hyena_in")
    filt, ss = _hyena_filters(n_tok, d, *fparams)

    if n_tok <= SINGLE_STAGE_MAX_LEN:
        plan = _dft_plan_single(n_tok)
        fwd, inv = jnp.asarray(plan["fwd"]), jnp.asarray(plan["inv"])
        part_cols = lambda p: (lambda tn: (lambda j: p * (d // tn) + j))
        hf = _matmul(fwd, filt, out_dtype=F32, tm=2 * plan["n"], tn=512, name="dft_filter")
        z_src, z_cols = u, part_cols(HYENA_ORDER)
        for order in range(HYENA_ORDER):
            x = _matmul(fwd, z_src, out_dtype=F32, tm=2 * plan["n"], tn=512, w_cols=(d, z_cols),
                        name="dft_signal")
            y = _cmul(x, hf, ss, d, order)
            last = order == HYENA_ORDER - 1
            z_src = _matmul(inv, y, out_dtype=BF16 if last else F32, epilogue=_ep_gate, tn=512,
                            extras=(_tile_extra(u, part_cols(order)), _tile_extra(z_src, z_cols),
                                    _row_extra(skip[order])), name="idft_gate")
            z_cols = part_cols(0)
        return z_src

    plan = _dft_plan(n_tok)
    ah = _dft_stage1(plan, filt, HYENA_ORDER * d)
    z_src, z_part = u, HYENA_ORDER
    for order in range(HYENA_ORDER):
        a = _dft_stage1(plan, z_src, d, z_part)
        q = _spectral_conv(plan, a, ah, ss, order)
        last = order == HYENA_ORDER - 1
        z_src = _idft_gate(plan, q, u, order, z_src, z_part, skip[order], BF16 if last else F32)
        z_part = 0
    return z_src


def kernel(x, c, ctx, c_ctx, mod_w, mod_b, norm_mix_g, norm_mlp_g, attn_w_in, attn_w_out, attn_sink, attn_q_norm, attn_k_norm, hy_w_in, hy_b_in, hy_conv_w, hy_conv_b, hy_f_w1, hy_f_b1, hy_f_freq1, hy_f_w2, hy_f_b2, hy_f_freq2, hy_f_w3, hy_skip, hy_w_out, hy_b_out, mlp_w1, mlp_w2, final_g):
    batch, n_lat, d = x.shape
    n_ctx = ctx.shape[1]
    depth = mod_w.shape[0]
    assert batch == 1 and n_lat % GRID_W == 0 and n_lat % WINDOW == 0
    last_ctx_layer = 2 * ((depth - 1) // 2)
    rope = _rope_tables(n_lat)
    d_ff = mlp_w1.shape[2]
    mlp_w2_bf, attn_w_out_bf, hy_w_out_bf = (w.astype(BF16) for w in (mlp_w2, attn_w_out, hy_w_out))

    cond = jnp.zeros((16, d), F32).at[0].set(c[0]).at[1].set(c_ctx)
    mod = _modulation(cond, mod_w, mod_b)

    xs = x[0]
    cs = ctx[0]
    for i in range(depth):
        j = i // 2
        is_attn = i % 2 == 0
        ctx_updated = i < last_ctx_layer
        sh1, sc1, g1, sh2, sc2, g2 = jnp.split(mod[i, 0:1], N_MOD, axis=-1)
        csh1, csc1, cg1, csh2, csc2, cg2 = jnp.split(mod[i, 1:2], N_MOD, axis=-1)
        if is_attn:
            qkv, h_lat = _attn_in(xs, norm_mix_g[i], sh1, sc1, attn_w_in, j, attn_q_norm[j], attn_k_norm[j], rope)
            qkv_c, h_ctx = _attn_in(cs, norm_mix_g[i], csh1, csc1, attn_w_in, j, attn_q_norm[j], attn_k_norm[j],
                                    None)
            wv_t = attn_w_in[j][:, Q_WIDTH + KV_WIDTH:].T
            vt, vt_c = _project_t(wv_t, h_lat), _project_t(wv_t, h_ctx)
            o_a = _window_attention(qkv, vt, qkv_c, vt_c, attn_sink[j])
            o_b = _global_attention(qkv, vt, qkv_c, vt_c)
            xs = _attn_out(o_a, o_b, attn_w_out_bf, j, xs, g1)
            if ctx_updated:
                o_c = _context_attention(qkv_c, attn_sink[j])
                cs = _matmul(o_c, attn_w_out_bf, layer=j, out_dtype=F32, epilogue=_ep_resid,
                             extras=(_tile_extra(cs), _row_extra(cg1)), name="attn_out_ctx")
        else:
            fparams = (hy_f_w1[j], hy_f_b1[j], hy_f_freq1[j], hy_f_w2[j], hy_f_b2[j], hy_f_freq2[j], hy_f_w3[j])
            z = _hyena_mixer(xs, (norm_mix_g[i], sh1, sc1), hy_w_in, j, hy_b_in[j], hy_conv_w[j], hy_conv_b[j],
                             fparams, hy_skip[j])
            xs = _matmul(z, hy_w_out_bf, layer=j, out_dtype=F32, epilogue=_ep_resid_bias, tm=512, tn=2048,
                         extras=(_tile_extra(xs), _row_extra(g1), _row_extra(hy_b_out[j])), name="hyena_out")
            if ctx_updated:
                z_c = _hyena_mixer(cs, (norm_mix_g[i], csh1, csc1), hy_w_in, j, hy_b_in[j], hy_conv_w[j],
                                   hy_conv_b[j], fparams, hy_skip[j])
                cs = _matmul(z_c, hy_w_out_bf, layer=j, out_dtype=F32, epilogue=_ep_resid_bias,
                             extras=(_tile_extra(cs), _row_extra(cg1), _row_extra(hy_b_out[j])),
                             name="hyena_out_ctx")
        a1 = _norm_matmul(xs, norm_mlp_g[i], sh2, sc2, mlp_w1, layer=i, out_dtype=BF16, epilogue=_ep_relu2,
                          name="mlp_up")
        xs = _matmul(a1, mlp_w2_bf, layer=i, out_dtype=F32, epilogue=_ep_resid, tm=BIG_TM, tn=256, tk=d_ff,
                     extras=(_tile_extra(xs), _row_extra(g2)), name="mlp_down")
        if ctx_updated:
            a1c = _norm_matmul(cs, norm_mlp_g[i], csh2, csc2, mlp_w1, layer=i, out_dtype=BF16,
                               epilogue=_ep_relu2, name="mlp_up_ctx")
            cs = _matmul(a1c, mlp_w2_bf, layer=i, out_dtype=F32, epilogue=_ep_resid, tn=512, tk=d_ff,
                         extras=(_tile_extra(cs), _row_extra(cg2)), name="mlp_down_ctx")
    return _rms_norm(xs, final_g, out_dtype=F32)[None]
```

```python
import functools
import math

import numpy as np
import jax
import jax.numpy as jnp
from jax import lax
from jax.experimental import pallas as pl
from jax.experimental.pallas import tpu as pltpu

F32 = jnp.float32
BF16 = jnp.bfloat16

GRID_W = 64
HEAD_DIM = 128
A_Q_HEADS = 8
A_KV_HEADS = 2
B_Q_HEADS = 8
B_KV_HEADS = 2
GROUP = A_Q_HEADS // A_KV_HEADS
WINDOW = 128
ROPE_THETA = 10000.0
AXIS_DIM = HEAD_DIM // 2
Q_WIDTH = (A_Q_HEADS + B_Q_HEADS) * HEAD_DIM
KV_HEADS = A_KV_HEADS + B_KV_HEADS
KV_WIDTH = KV_HEADS * HEAD_DIM
HYENA_ORDER = 2
SHORT_CONV = 3
FILTER_EMB = 33
DECAY_TARGET = 1e-2
FAST_DECAY_PCT = 0.3
SLOW_DECAY_PCT = 1.5
MAX_DECAY = math.log(DECAY_TARGET) / FAST_DECAY_PCT
MIN_DECAY = math.log(DECAY_TARGET) / SLOW_DECAY_PCT
N_MOD = 6
EPS = 1e-6
LOG2_E = math.log2(math.e)
LN_2 = math.log(2.0)

VMEM_LIMIT_BYTES = 56 * 1024 * 1024
LANES = 128
SINGLE_STAGE_MAX_LEN = 512
BIG_TM = 1024


def _params(*sem):
    return pltpu.CompilerParams(dimension_semantics=sem, vmem_limit_bytes=VMEM_LIMIT_BYTES)


def _tile(n, pref, align):
    if n <= pref:
        return n
    t = (pref // align) * align
    while t >= align:
        if n % t == 0:
            return t
        t -= align
    return n


def _mod_body(a_ref, w_ref, b_ref, o_ref):
    a = a_ref[...]
    act = a * (1.0 / (1.0 + jnp.exp(-a)))
    o_ref[...] = jnp.dot(act.astype(BF16), w_ref[...].astype(BF16),
                         preferred_element_type=F32) + b_ref[...]


def _modulation(cond, mod_w, mod_b):
    depth, d, n = mod_w.shape
    r = cond.shape[0]
    tn = _tile(n, 1024, LANES)
    return pl.pallas_call(
        _mod_body,
        grid=(depth, n // tn),
        in_specs=[pl.BlockSpec((r, d), lambda l, j: (0, 0)),
                  pl.BlockSpec((None, d, tn), lambda l, j: (l, 0, j)),
                  pl.BlockSpec((None, 1, tn), lambda l, j: (l, 0, j))],
        out_specs=pl.BlockSpec((None, r, tn), lambda l, j: (l, 0, j)),
        out_shape=jax.ShapeDtypeStruct((depth, r, n), F32),
        compiler_params=_params("parallel", "parallel"),
        name="modulation",
    )(cond, mod_w, mod_b.reshape(depth, 1, n))


def _norm_body(x_ref, g_ref, *rest, modulated):
    x = x_ref[...]
    y = x * lax.rsqrt(jnp.mean(x * x, axis=-1, keepdims=True) + EPS) * g_ref[...]
    if modulated:
        sh_ref, sc_ref, o_ref = rest
        y = y * (1.0 + sc_ref[...]) + sh_ref[...]
    else:
        (o_ref,) = rest
    o_ref[...] = y.astype(o_ref.dtype)


def _rms_norm(x, g, shift=None, scale=None, out_dtype=None):
    t, d = x.shape
    out_dtype = BF16 if out_dtype is None else out_dtype
    tm = _tile(t, 512, 16)
    row = pl.BlockSpec((1, d), lambda i: (0, 0))
    vecs = [g.reshape(1, d)]
    if shift is not None:
        vecs += [shift.reshape(1, d), scale.reshape(1, d)]
    return pl.pallas_call(
        functools.partial(_norm_body, modulated=shift is not None),
        grid=(t // tm,),
        in_specs=[pl.BlockSpec((tm, d), lambda i: (i, 0))] + [row] * len(vecs),
        out_specs=pl.BlockSpec((tm, d), lambda i: (i, 0)),
        out_shape=jax.ShapeDtypeStruct((t, d), out_dtype),
        compiler_params=_params("parallel"),
        name="rms_norm",
    )(x, *vecs)


def _ep_none(acc):
    return acc


def _ep_relu2(acc):
    return jnp.square(jnp.maximum(acc, 0.0))


def _ep_resid(acc, x, g):
    return x + g * acc


def _ep_resid_bias(acc, x, g, b):
    return x + g * (acc + b)


def _ep_gate(acc, x, v, skip):
    return x * (acc + v * skip)


def _mm_body(*refs, nk, epilogue, n_extra):
    a_ref, w_ref = refs[0], refs[1]
    extra = refs[2:2 + n_extra]
    o_ref = refs[2 + n_extra]
    a = a_ref[...].astype(BF16)
    w = w_ref[...].astype(BF16)
    part = jnp.dot(a, w, preferred_element_type=F32)
    if nk == 1:
        o_ref[...] = epilogue(part, *[e[...] for e in extra]).astype(o_ref.dtype)
        return
    acc_ref = refs[3 + n_extra]
    k = pl.program_id(2)

    @pl.when(k == 0)
    def _():
        acc_ref[...] = part

    @pl.when(k > 0)
    def _():
        acc_ref[...] += part

    @pl.when(k == nk - 1)
    def _():
        o_ref[...] = epilogue(acc_ref[...], *[e[...] for e in extra]).astype(o_ref.dtype)


def _weight_spec(w, layer, tk, tn, index):
    if layer is None:
        return pl.BlockSpec((tk, tn), index)
    return pl.BlockSpec((None, tk, tn), lambda *ids: (layer,) + tuple(index(*ids)))


def _matmul(a, w, *, out_dtype, epilogue=_ep_none, extras=(), tm=512, tn=1024, tk=2048,
            w_cols=None, layer=None, name="matmul"):
    m, kdim = a.shape
    n = w.shape[-1] if w_cols is None else w_cols[0]
    tm = _tile(m, tm, 16)
    tn = _tile(n, tn, LANES)
    tk = _tile(kdim, tk, LANES)
    nk = kdim // tk
    wmap = (lambda j: j) if w_cols is None else w_cols[1](tn)
    in_specs = [pl.BlockSpec((tm, tk), lambda i, j, k: (i, k)),
                _weight_spec(w, layer, tk, tn, lambda i, j, k: (k, wmap(j)))]
    in_specs += [fn(tm, tn) for _, fn in extras]
    return pl.pallas_call(
        functools.partial(_mm_body, nk=nk, epilogue=epilogue, n_extra=len(extras)),
        grid=(m // tm, n // tn, nk),
        in_specs=in_specs,
        out_specs=pl.BlockSpec((tm, tn), lambda i, j, k: (i, j)),
        out_shape=jax.ShapeDtypeStruct((m, n), out_dtype),
        scratch_shapes=[pltpu.VMEM((tm, tn), F32)] if nk > 1 else [],
        compiler_params=_params("parallel", "parallel", "arbitrary"),
        name=name,
    )(a, w, *[arr for arr, _ in extras])


def _nmm_body(x_ref, g_ref, sh_ref, sc_ref, w_ref, *rest, epilogue, n_extra):
    extra = rest[:n_extra]
    o_ref, h_ref = rest[n_extra], rest[n_extra + 1]

    @pl.when(pl.program_id(1) == 0)
    def _():
        x = x_ref[...]
        y = x * lax.rsqrt(jnp.mean(x * x, axis=-1, keepdims=True) + EPS) * g_ref[...]
        h_ref[...] = (y * (1.0 + sc_ref[...]) + sh_ref[...]).astype(h_ref.dtype)

    acc = jnp.dot(h_ref[...], w_ref[...].astype(BF16), preferred_element_type=F32)
    o_ref[...] = epilogue(acc, *[e[...] for e in extra]).astype(o_ref.dtype)


def _norm_matmul(x, g, shift, scale, w, *, out_dtype, epilogue=_ep_none, extras=(), tm=1024, tn=1024,
                 layer=None, name="norm_matmul"):
    m, d = x.shape
    n = w.shape[-1]
    tm = _tile(m, tm, 16)
    tn = _tile(n, tn, LANES)
    vec = pl.BlockSpec((1, d), lambda i, j: (0, 0))
    with_k = lambda spec_fn: (lambda bs: pl.BlockSpec(bs.block_shape, lambda i, j: bs.index_map(i, j, 0)))(
        spec_fn(tm, tn))
    return pl.pallas_call(
        functools.partial(_nmm_body, epilogue=epilogue, n_extra=len(extras)),
        grid=(m // tm, n // tn),
        in_specs=[pl.BlockSpec((tm, d), lambda i, j: (i, 0)), vec, vec, vec,
                  _weight_spec(w, layer, d, tn, lambda i, j: (0, j))] + [with_k(fn) for _, fn in extras],
        out_specs=pl.BlockSpec((tm, tn), lambda i, j: (i, j)),
        out_shape=jax.ShapeDtypeStruct((m, n), out_dtype),
        scratch_shapes=[pltpu.VMEM((tm, d), BF16)],
        compiler_params=_params("arbitrary", "arbitrary"),
        name=name,
    )(x, g.reshape(1, d), shift.reshape(1, d), scale.reshape(1, d), w, *[arr for arr, _ in extras])


HALO = 16


def _nmm_conv_body(x_ref, xp_ref, xn_ref, g_ref, sh_ref, sc_ref, w_ref, b_ref, cw_ref, cb_ref, o_ref, h_ref,
                   *, n_row_tiles):
    i = pl.program_id(0)
    tm = x_ref.shape[0]

    @pl.when(pl.program_id(1) == 0)
    def _():
        def norm(ref):
            x = ref[...]
            y = x * lax.rsqrt(jnp.mean(x * x, axis=-1, keepdims=True) + EPS) * g_ref[...]
            return (y * (1.0 + sc_ref[...]) + sh_ref[...]).astype(h_ref.dtype)

        h_ref[:HALO] = norm(xp_ref)
        h_ref[HALO:HALO + tm] = norm(x_ref)
        h_ref[HALO + tm:] = norm(xn_ref)

    u = jnp.dot(h_ref[...], w_ref[...].astype(BF16), preferred_element_type=F32) + b_ref[...]
    before = jnp.where(i > 0, u[:HALO], 0.0)
    after = jnp.where(i + 1 < n_row_tiles, u[HALO + tm:], 0.0)
    u = jnp.concatenate([before, u[HALO:HALO + tm], after], axis=0)
    rows = u.shape[0]
    y = (pltpu.roll(u, 1, 0) * cw_ref[0:1, :] + u * cw_ref[1:2, :]
         + pltpu.roll(u, rows - 1, 0) * cw_ref[2:3, :] + cb_ref[...])
    o_ref[...] = y[HALO:HALO + tm]


def _norm_matmul_conv(x, g, shift, scale, w, layer, b, conv_w, conv_b, *, tm=1024, tn=512, name):
    m, d = x.shape
    n = w.shape[-1]
    tm = _tile(m, tm, HALO)
    tn = _tile(n, tn, LANES)
    hb = tm // HALO
    n_row_tiles = m // tm
    vec = pl.BlockSpec((1, d), lambda i, j: (0, 0))
    col = lambda rows: pl.BlockSpec((rows, tn), lambda i, j: (0, j))
    return pl.pallas_call(
        functools.partial(_nmm_conv_body, n_row_tiles=n_row_tiles),
        grid=(n_row_tiles, n // tn),
        in_specs=[pl.BlockSpec((tm, d), lambda i, j: (i, 0)),
                  pl.BlockSpec((HALO, d), lambda i, j: (jnp.maximum(i * hb - 1, 0), 0)),
                  pl.BlockSpec((HALO, d), lambda i, j: (jnp.minimum((i + 1) * hb, m // HALO - 1), 0)),
                  vec, vec, vec, _weight_spec(w, layer, d, tn, lambda i, j: (0, j)),
                  col(1), col(SHORT_CONV), col(1)],
        out_specs=pl.BlockSpec((tm, tn), lambda i, j: (i, j)),
        out_shape=jax.ShapeDtypeStruct((m, n), F32),
        scratch_shapes=[pltpu.VMEM((tm + 2 * HALO, d), BF16)],
        compiler_params=_params("arbitrary", "arbitrary"),
        name=name,
    )(x, x, x, g.reshape(1, d), shift.reshape(1, d), scale.reshape(1, d), w, b.reshape(1, n), conv_w,
      conv_b.reshape(1, n))


def _row_extra(vec, period_blocks=None):
    vec = vec.reshape(1, -1)
    if period_blocks is None:
        return vec, lambda tm, tn: pl.BlockSpec((1, tn), lambda i, j, k: (0, j))
    return vec, lambda tm, tn: pl.BlockSpec((1, tn), lambda i, j, k: (0, j % period_blocks(tn)))


def _tile_extra(arr, colmap=None):
    cm = colmap if colmap is not None else (lambda tn: (lambda j: j))
    return arr, lambda tm, tn: pl.BlockSpec((tm, tn), lambda i, j, k: (i, cm(tn)(j)))


def _rope_tables(n_tok):
    rows = n_tok // GRID_W
    row = jnp.repeat(jnp.arange(rows, dtype=F32), GRID_W)
    col = jnp.tile(jnp.arange(GRID_W, dtype=F32), rows)
    inv = ROPE_THETA ** (-jnp.arange(0, AXIS_DIM, 2, dtype=F32) / AXIS_DIM)
    ang_r = row[:, None] * inv[None]
    ang_c = col[:, None] * inv[None]
    cos = jnp.concatenate([jnp.cos(ang_r)] * 2 + [jnp.cos(ang_c)] * 2, axis=-1)
    sin = jnp.concatenate([-jnp.sin(ang_r), jnp.sin(ang_r), -jnp.sin(ang_c), jnp.sin(ang_c)], axis=-1)
    return cos, sin


HEADS_PER_TILE = 8
ATTN_IN_WIDTH = Q_WIDTH + 2 * KV_WIDTH


def _attn_in_body(*refs, rope):
    if rope:
        x_ref, g_ref, sh_ref, sc_ref, w_ref, qn_ref, kn_ref, cos_ref, sin_ref, o_ref, h_ref = refs
        cos, sin = cos_ref[...], sin_ref[...]
        lane = lax.broadcasted_iota(jnp.int32, cos.shape, 1)
        first = (lane % (AXIS_DIM)) < (AXIS_DIM // 2)
    else:
        x_ref, g_ref, sh_ref, sc_ref, w_ref, qn_ref, kn_ref, o_ref, h_ref = refs
    j = pl.program_id(1)
    scale = HEAD_DIM ** -0.5
    quarter = AXIS_DIM // 2

    @pl.when(j == 0)
    def _():
        x = x_ref[...]
        y = x * lax.rsqrt(jnp.mean(x * x, axis=-1, keepdims=True) + EPS) * g_ref[...]
        h_ref[...] = (y * (1.0 + sc_ref[...]) + sh_ref[...]).astype(h_ref.dtype)

    acc = jnp.dot(h_ref[...], w_ref[...].astype(BF16), preferred_element_type=F32)

    def rot(x):
        if not rope:
            return x
        swapped = jnp.where(first, pltpu.roll(x, HEAD_DIM - quarter, 1), pltpu.roll(x, quarter, 1))
        return x * cos + swapped * sin

    def nrm(x, g_ref_):
        return x * lax.rsqrt(jnp.mean(x * x, axis=-1, keepdims=True) + EPS) * g_ref_[...]

    q_a = lambda x: rot(x) * scale
    q_b = lambda x: rot(nrm(x, qn_ref)) * (scale * LOG2_E)
    k_b = lambda x: rot(nrm(x, kn_ref))
    kinds = ([q_a] * A_Q_HEADS + [q_b] * B_Q_HEADS + [rot] * A_KV_HEADS + [k_b] * B_KV_HEADS
             + [lambda x: x] * KV_HEADS)

    def emit(tile_kinds):
        for hh, kind in enumerate(tile_kinds):
            sl = slice(hh * HEAD_DIM, (hh + 1) * HEAD_DIM)
            o_ref[:, sl] = kind(acc[:, sl]).astype(o_ref.dtype)

    for t in range(len(kinds) // HEADS_PER_TILE):
        pl.when(j == t)(functools.partial(emit, kinds[t * HEADS_PER_TILE:(t + 1) * HEADS_PER_TILE]))


def _attn_in(x, g, shift, scale, w, layer, q_norm, k_norm, rope):
    t, d = x.shape
    tm = _tile(t, 512, 16)
    tn = HEADS_PER_TILE * HEAD_DIM
    vec = lambda n: pl.BlockSpec((1, n), lambda i, j: (0, 0))
    in_specs = [pl.BlockSpec((tm, d), lambda i, j: (i, 0)), vec(d), vec(d), vec(d),
                _weight_spec(w, layer, d, tn, lambda i, j: (0, j)), vec(HEAD_DIM), vec(HEAD_DIM)]
    args = [x, g.reshape(1, d), shift.reshape(1, d), scale.reshape(1, d), w,
            q_norm.reshape(1, HEAD_DIM), k_norm.reshape(1, HEAD_DIM)]
    if rope is not None:
        in_specs += [pl.BlockSpec((tm, HEAD_DIM), lambda i, j: (i, 0))] * 2
        args += list(rope)
    return pl.pallas_call(
        functools.partial(_attn_in_body, rope=rope is not None),
        grid=(t // tm, ATTN_IN_WIDTH // tn),
        in_specs=in_specs,
        out_specs=[pl.BlockSpec((tm, tn), lambda i, j: (i, j)), pl.BlockSpec((tm, d), lambda i, j: (i, 0))],
        out_shape=[jax.ShapeDtypeStruct((t, ATTN_IN_WIDTH), BF16), jax.ShapeDtypeStruct((t, d), BF16)],
        compiler_params=_params("arbitrary", "arbitrary"),
        name="attn_in",
    )(*args)


def _stack_heads(q_ref):
    return jnp.concatenate([q_ref[:, g * HEAD_DIM:(g + 1) * HEAD_DIM] for g in range(GROUP)], axis=0)


def _unstack_heads(o_ref, out, t):
    for g in range(GROUP):
        o_ref[:, g * HEAD_DIM:(g + 1) * HEAD_DIM] = out[g * t:(g + 1) * t].astype(o_ref.dtype)


def _sink_column(sink_ref, first, t):
    head = lax.broadcasted_iota(jnp.int32, (GROUP * t, 1), 0) // t
    col = jnp.full((GROUP * t, 1), sink_ref[first], F32)
    for g in range(1, GROUP):
        col = jnp.where(head == g, sink_ref[first + g], col)
    return col


def _qk(q, k):
    return lax.dot_general(q, k, (((1,), (1,)), ((), ())), preferred_element_type=F32)


ONES_ROWS = 16


def _gattn_body(q_ref, kc_ref, vtc_ref, k_ref, vt_ref, o_ref, m_ref, acc_ref, sa_ref, sb_ref,
                *, tq, ck, n_chunks):
    q = _stack_heads(q_ref)
    m_ref[...] = jnp.full(m_ref.shape, -jnp.inf, F32)
    acc_ref[...] = jnp.zeros(acc_ref.shape, F32)

    def scores(c):
        return _qk(k_ref[pl.ds(pl.multiple_of(c * ck, ck), ck), :], q)

    def consume(st, vtb):
        m_old = m_ref[...]
        m_new = jnp.maximum(m_old, jnp.max(st, axis=0, keepdims=True))
        p = jnp.exp2(st - m_new).astype(BF16)
        lhs = jnp.concatenate([vtb, jnp.ones((ONES_ROWS, vtb.shape[1]), BF16)], axis=0)
        acc_ref[...] = jnp.exp2(m_old - m_new) * acc_ref[...] + jnp.dot(lhs, p, preferred_element_type=F32)
        m_ref[...] = m_new

    def values(c):
        return vt_ref[:, pl.ds(pl.multiple_of(c * ck, ck), ck)]

    sa_ref[...] = scores(0)
    consume(_qk(kc_ref[...], q), vtc_ref[...])

    def pair(c2, carry):
        c = 2 * c2
        sb_ref[...] = scores(c + 1)
        consume(sa_ref[...], values(c))
        sa_ref[...] = scores(c + 2)
        consume(sb_ref[...], values(c + 1))
        return carry

    lax.fori_loop(0, n_chunks // 2 - 1, pair, 0)
    last = n_chunks - 2
    sb_ref[...] = scores(last + 1)
    consume(sa_ref[...], values(last))
    consume(sb_ref[...], values(last + 1))
    _unstack_heads(o_ref, (acc_ref[:HEAD_DIM] / acc_ref[HEAD_DIM:HEAD_DIM + 1]).T, tq)


def _attn_out_body(oa_ref, ob_ref, w_ref, x_ref, g_ref, o_ref):
    ka = oa_ref.shape[1]
    w = w_ref[...].astype(BF16)
    acc = (jnp.dot(oa_ref[...], w[:ka], preferred_element_type=F32)
           + jnp.dot(ob_ref[...], w[ka:], preferred_element_type=F32))
    o_ref[...] = x_ref[...] + g_ref[...] * acc


def _attn_out(o_a, o_b, w, layer, x, gate):
    m, ka = o_a.shape
    kb = o_b.shape[1]
    n = w.shape[-1]
    tm = _tile(m, 512, 16)
    tn = _tile(n, 2048, LANES)
    return pl.pallas_call(
        _attn_out_body,
        grid=(m // tm, n // tn),
        in_specs=[pl.BlockSpec((tm, ka), lambda i, j: (i, 0)), pl.BlockSpec((tm, kb), lambda i, j: (i, 0)),
                  _weight_spec(w, layer, ka + kb, tn, lambda i, j: (0, j)),
                  pl.BlockSpec((tm, tn), lambda i, j: (i, j)), pl.BlockSpec((1, tn), lambda i, j: (0, j))],
        out_specs=pl.BlockSpec((tm, tn), lambda i, j: (i, j)),
        out_shape=jax.ShapeDtypeStruct((m, n), F32),
        compiler_params=_params("parallel", "parallel"),
        name="attn_out",
    )(o_a, o_b, w, x, gate.reshape(1, n))


K_COL0 = Q_WIDTH // HEAD_DIM


def _global_attention(qkv, vt, qkv_ctx, vt_ctx):
    s, n_ctx = qkv.shape[0], qkv_ctx.shape[0]
    tq = _tile(s, 512, LANES)
    ck = _tile(s // 2, 1024, LANES)
    n_chunks = s // ck
    assert n_chunks % 2 == 0
    rows = GROUP * tq
    gw = GROUP * HEAD_DIM
    kspec = lambda n: pl.BlockSpec((n, HEAD_DIM), lambda h, i: (0, K_COL0 + A_KV_HEADS + h))
    vspec = lambda n: pl.BlockSpec((HEAD_DIM, n), lambda h, i: (A_KV_HEADS + h, 0))
    return pl.pallas_call(
        functools.partial(_gattn_body, tq=tq, ck=ck, n_chunks=n_chunks),
        grid=(B_KV_HEADS, s // tq),
        in_specs=[pl.BlockSpec((tq, gw), lambda h, i: (i, A_KV_HEADS + h)),
                  kspec(n_ctx), vspec(n_ctx), kspec(s), vspec(s)],
        out_specs=pl.BlockSpec((tq, gw), lambda h, i: (i, h)),
        out_shape=jax.ShapeDtypeStruct((s, B_Q_HEADS * HEAD_DIM), BF16),
        scratch_shapes=[pltpu.VMEM((1, rows), F32), pltpu.VMEM((HEAD_DIM + ONES_ROWS, rows), F32),
                        pltpu.VMEM((ck, rows), F32), pltpu.VMEM((ck, rows), F32)],
        compiler_params=_params("parallel", "parallel"),
        name="global_attention",
    )(qkv, qkv_ctx, vt_ctx, qkv, vt)


def _project_t_body(w_ref, h_ref, o_ref):
    o_ref[...] = _qk(w_ref[...].astype(BF16), h_ref[...]).astype(o_ref.dtype)


def _project_t(w_t, h):
    c, d = w_t.shape
    t = h.shape[0]
    tt = _tile(t, 1024, LANES)
    return pl.pallas_call(
        _project_t_body,
        grid=(t // tt,),
        in_specs=[pl.BlockSpec((c, d), lambda i: (0, 0)), pl.BlockSpec((tt, d), lambda i: (i, 0))],
        out_specs=pl.BlockSpec((c, tt), lambda i: (0, i)),
        out_shape=jax.ShapeDtypeStruct((c, t), BF16),
        compiler_params=_params("parallel"),
        name="project_t",
    )(w_t, h)


WINDOW_BLOCKS = 2


def _wattn_body(sink_ref, q_ref, kc_ref, vtc_ref, kp_ref, k0_ref, kn_ref, vtp_ref, vt0_ref, vtn_ref, o_ref,
                *, n_tiles, n_ctx):
    h = pl.program_id(0)
    i = pl.program_id(1)
    w = WINDOW
    tq = q_ref.shape[0]
    q = _stack_heads(q_ref)
    kcat = jnp.concatenate([kc_ref[...], kp_ref[...], k0_ref[...], kn_ref[...]], axis=0)
    vtcat = jnp.concatenate([vtc_ref[...], vtp_ref[...], vt0_ref[...], vtn_ref[...]], axis=1)
    st = _qk(kcat, q)
    c = lax.broadcasted_iota(jnp.int32, st.shape, 0) - n_ctx
    r = lax.broadcasted_iota(jnp.int32, st.shape, 1) % tq
    lo = jnp.maximum(r, jnp.where(i >= 1, 0, w))
    hi = jnp.minimum(r + 2 * w, jnp.where(i + 1 < n_tiles, tq + 2 * w - 1, tq + w - 1))
    valid = (c < 0) | ((c >= lo) & (c <= hi))
    st = jnp.where(valid, st, -jnp.inf)
    head = lax.broadcasted_iota(jnp.int32, (1, GROUP * tq), 1) // tq
    sink = jnp.full((1, GROUP * tq), sink_ref[h * GROUP], F32)
    for g in range(1, GROUP):
        sink = jnp.where(head == g, sink_ref[h * GROUP + g], sink)
    m = jnp.maximum(jnp.max(st, axis=0, keepdims=True), sink)
    p = jnp.exp(st - m).astype(BF16)
    lhs = jnp.concatenate([vtcat, jnp.ones((ONES_ROWS, vtcat.shape[1]), BF16)], axis=0)
    acc = jnp.dot(lhs, p, preferred_element_type=F32)
    denom = acc[HEAD_DIM:HEAD_DIM + 1] + jnp.exp(sink - m)
    _unstack_heads(o_ref, (acc[:HEAD_DIM] / denom).T, tq)


def _window_attention(qkv, vt, qkv_ctx, vt_ctx, sink):
    s, n_ctx = qkv.shape[0], qkv_ctx.shape[0]
    w = WINDOW
    nb = s // w
    tb = WINDOW_BLOCKS if nb % WINDOW_BLOCKS == 0 else 1
    tq = tb * w
    n_tiles = nb // tb
    gw = GROUP * HEAD_DIM
    before = lambda i: jnp.maximum(i * tb - 1, 0)
    after = lambda i: jnp.minimum((i + 1) * tb, nb - 1)
    kblk = lambda pos: pl.BlockSpec((w, HEAD_DIM), lambda h, i: (pos(i), K_COL0 + h))
    vblk = lambda pos: pl.BlockSpec((HEAD_DIM, w), lambda h, i: (h, pos(i)))
    return pl.pallas_call(
        functools.partial(_wattn_body, n_tiles=n_tiles, n_ctx=n_ctx),
        grid=(A_KV_HEADS, n_tiles),
        in_specs=[pl.BlockSpec(memory_space=pltpu.SMEM),
                  pl.BlockSpec((tq, gw), lambda h, i: (i, h)),
                  pl.BlockSpec((n_ctx, HEAD_DIM), lambda h, i: (0, K_COL0 + h)),
                  pl.BlockSpec((HEAD_DIM, n_ctx), lambda h, i: (h, 0)),
                  kblk(before), pl.BlockSpec((tq, HEAD_DIM), lambda h, i: (i, K_COL0 + h)), kblk(after),
                  vblk(before), pl.BlockSpec((HEAD_DIM, tq), lambda h, i: (h, i)), vblk(after)],
        out_specs=pl.BlockSpec((tq, gw), lambda h, i: (i, h)),
        out_shape=jax.ShapeDtypeStruct((s, A_Q_HEADS * HEAD_DIM), BF16),
        compiler_params=_params("parallel", "parallel"),
        name="window_attention",
    )(sink, qkv, qkv_ctx, vt_ctx, qkv, qkv, qkv, vt, vt, vt)


def _cattn_body(sink_ref, q_ref, k_ref, v_ref, o_ref, *, n_ctx):
    h = pl.program_id(0)
    q = _stack_heads(q_ref)
    s = _qk(q, k_ref[...]) * jnp.where(h >= A_KV_HEADS, LN_2, 1.0)
    sink = _sink_column(sink_ref, h * GROUP, n_ctx)
    m = jnp.maximum(jnp.max(s, axis=-1, keepdims=True), sink)
    p = jnp.exp(s - m)
    denom = jnp.sum(p, axis=-1, keepdims=True) + jnp.exp(sink - m)
    out = jnp.dot(p.astype(BF16), v_ref[...], preferred_element_type=F32) / denom
    _unstack_heads(o_ref, out, n_ctx)


def _context_attention(qkv, sink):
    n_ctx = qkv.shape[0]
    gw = GROUP * HEAD_DIM
    sink_all = jnp.concatenate([sink.astype(F32), jnp.full((B_Q_HEADS,), -jnp.inf, F32)])
    kv = lambda col0: pl.BlockSpec((n_ctx, HEAD_DIM), lambda h: (0, col0 + h))
    return pl.pallas_call(
        functools.partial(_cattn_body, n_ctx=n_ctx),
        grid=(KV_HEADS,),
        in_specs=[pl.BlockSpec(memory_space=pltpu.SMEM),
                  pl.BlockSpec((n_ctx, gw), lambda h: (0, h)), kv(K_COL0), kv(K_COL0 + KV_HEADS)],
        out_specs=pl.BlockSpec((n_ctx, gw), lambda h: (0, h)),
        out_shape=jax.ShapeDtypeStruct((n_ctx, Q_WIDTH), BF16),
        compiler_params=_params("parallel"),
        name="context_attention",
    )(sink_all, qkv, qkv, qkv)


def _filter_body(feat_ref, w1_ref, b1_ref, f1_ref, w2_ref, b2_ref, f2_ref, w3_ref, dl_ref,
                 h_ref, ss_ref, *, n_tok):
    i = pl.program_id(0)
    tl = feat_ref.shape[0]
    mm = lambda a, b: jnp.dot(a.astype(BF16), b.astype(BF16), preferred_element_type=F32)
    hid = jnp.sin(f1_ref[...] * (mm(feat_ref[...], w1_ref[...]) + b1_ref[...]))
    hid = jnp.sin(f2_ref[...] * (mm(hid, w2_ref[...]) + b2_ref[...]))
    h = mm(hid, w3_ref[...])
    t = i * tl + lax.broadcasted_iota(jnp.int32, (tl, 1), 0)
    offs = jnp.abs(t - n_tok // 2).astype(F32) * (2.0 / n_tok)
    h = h * jnp.exp(-offs * dl_ref[...])
    h_ref[...] = h.astype(h_ref.dtype)

    @pl.when(i == 0)
    def _():
        ss_ref[...] = jnp.zeros(ss_ref.shape, F32)

    ss_ref[...] += jnp.sum(h * h, axis=0, keepdims=True)


def _hyena_filters(n_tok, d, w1, b1, fr1, w2, b2, fr2, w3):
    bands = (FILTER_EMB - 1) // 2
    t = jnp.linspace(0.0, 1.0, n_tok, dtype=F32)[:, None]
    wv = 2.0 * math.pi * jnp.arange(n_tok, dtype=F32)[:, None] / n_tok
    f = jnp.linspace(1e-4, bands - 1, bands, dtype=F32)[None]
    feats = jnp.concatenate([t, jnp.cos(f * wv), -jnp.sin(f * wv)], axis=-1)
    deltas = jnp.abs(jnp.linspace(MIN_DECAY, MAX_DECAY, d, dtype=F32))
    hidden = w1.shape[1]
    pad_e, pad_h = LANES - FILTER_EMB, LANES - hidden
    padv = lambda v: jnp.pad(v.reshape(1, hidden), ((0, 0), (0, pad_h)))
    c = HYENA_ORDER * d
    tl = _tile(n_tok, 256, 8)
    full = lambda shape: pl.BlockSpec(shape, lambda i: (0, 0))
    return pl.pallas_call(
        functools.partial(_filter_body, n_tok=n_tok),
        grid=(n_tok // tl,),
        in_specs=[pl.BlockSpec((tl, LANES), lambda i: (i, 0)),
                  full((LANES, LANES)), full((1, LANES)), full((1, LANES)),
                  full((LANES, LANES)), full((1, LANES)), full((1, LANES)),
                  full((LANES, c)), full((1, c))],
        out_specs=[pl.BlockSpec((tl, c), lambda i: (i, 0)), full((1, c))],
        out_shape=[jax.ShapeDtypeStruct((n_tok, c), BF16), jax.ShapeDtypeStruct((1, c), F32)],
        compiler_params=_params("arbitrary"),
        name="hyena_filter",
    )(jnp.pad(feats, ((0, 0), (0, pad_e))),
      jnp.pad(w1, ((0, pad_e), (0, pad_h))), padv(b1), padv(fr1),
      jnp.pad(w2, ((0, pad_h), (0, pad_h))), padv(b2), padv(fr2),
      jnp.pad(w3, ((0, pad_h), (0, 0))), jnp.tile(deltas, HYENA_ORDER).reshape(1, c))


def _lane_table(vals):
    return jnp.broadcast_to(jnp.asarray(vals)[..., None], vals.shape + (LANES,))


SUBLANES = 8
DFT_ROW_GROUP = 16


@functools.lru_cache(maxsize=None)
def _dft_plan(n_tok):
    n = 2 * n_tok
    log = n.bit_length() - 1
    assert 1 << log == n, "sequence length must be a power of two"
    n2 = 1 << (log // 2)
    n1 = n // n2
    assert n1 % 4 == 0 and n2 % DFT_ROW_GROUP == 0
    eye = np.eye(SUBLANES)
    nk1 = n1 // 2 + 1
    nk1_pad = -(-nk1 // SUBLANES) * SUBLANES
    a1 = 2.0 * np.pi * np.outer(np.arange(nk1), np.arange(n1 // 2)) / n1
    f1 = np.kron(np.concatenate([np.cos(a1), -np.sin(a1)], axis=0), eye)
    a2 = 2.0 * np.pi * np.outer(np.arange(n2), np.arange(n2)) / n2
    c2, s2 = np.cos(a2), np.sin(a2)
    g2 = np.block([[c2, s2], [-s2, c2]])
    g2i = np.block([[c2, -s2], [s2, c2]])
    rows = np.arange(n1 // 4, n1 // 4 + n1 // 2)
    a1i = 2.0 * np.pi * np.outer(rows, np.arange(nk1_pad)) / n1
    weight = np.where((np.arange(nk1_pad) == 0) | (np.arange(nk1_pad) == n1 // 2), 1.0, 2.0)
    weight = np.where(np.arange(nk1_pad) < nk1, weight, 0.0) / n
    f1i = np.kron(np.concatenate([np.cos(a1i) * weight, -np.sin(a1i) * weight], axis=1), eye)
    tw = 2.0 * np.pi * np.outer(np.arange(n2), np.arange(nk1)) / n
    tw1 = tw.reshape(n2 // SUBLANES, SUBLANES, nk1).transpose(0, 2, 1).reshape(n2 // SUBLANES, nk1 * SUBLANES)
    f32 = lambda m: m.astype(np.float32)
    return dict(n1=n1, n2=n2, nk1=nk1, nk1_pad=nk1_pad, f1=f32(f1), g2=f32(g2), g2i=f32(g2i), f1i=f32(f1i),
                tw1_cos=f32(np.cos(tw1)), tw1_sin=f32(np.sin(tw1)),
                tw2_cos=f32(np.cos(tw.T)), tw2_sin=f32(np.sin(tw.T)))


def _first_step():
    return (pl.program_id(0) == 0) & (pl.program_id(1) == 0)


def _stage1_body(f_ref, z_ref, c_ref, s_ref, o_ref, fb_ref):
    n1h, g, tc = z_ref.shape
    nk1 = o_ref.shape[1]
    half = nk1 * SUBLANES
    reps = tc // LANES

    @pl.when(_first_step())
    def _():
        fb_ref[...] = f_ref[...].astype(BF16)

    z = z_ref[...].astype(F32)
    re_parts, im_parts = [], []
    for a in range(g // SUBLANES):
        zz = z[:, a * SUBLANES:(a + 1) * SUBLANES, :].reshape(n1h * SUBLANES, tc)
        acc = jnp.dot(fb_ref[...], zz.astype(BF16), preferred_element_type=F32)
        re, im = acc[:half], acc[half:]
        c = jnp.tile(c_ref[a], (1, reps))
        s = jnp.tile(s_ref[a], (1, reps))
        re_parts.append((re * c + im * s).reshape(nk1, SUBLANES, tc))
        im_parts.append((im * c - re * s).reshape(nk1, SUBLANES, tc))
    o_ref[0] = jnp.concatenate(re_parts, axis=1).astype(o_ref.dtype)
    o_ref[1] = jnp.concatenate(im_parts, axis=1).astype(o_ref.dtype)


def _dft_stage1(plan, src, width, part=0):
    n1, n2, nk1 = plan["n1"], plan["n2"], plan["nk1"]
    g = DFT_ROW_GROUP
    tc = _tile(width, 1024, LANES)
    nc = width // tc
    sub = g // SUBLANES
    tw = lambda t: t.reshape(n2 // g, sub, nk1 * SUBLANES, LANES)
    tw_spec = pl.BlockSpec((None, sub, nk1 * SUBLANES, LANES), lambda i, j: (i, 0, 0, 0))
    f = jnp.asarray(plan["f1"])
    return pl.pallas_call(
        _stage1_body,
        grid=(n2 // g, nc),
        in_specs=[pl.BlockSpec(f.shape, lambda i, j: (0, 0)),
                  pl.BlockSpec((n1 // 2, g, tc), lambda i, j: (0, i, part * nc + j)),
                  tw_spec, tw_spec],
        out_specs=pl.BlockSpec((2, nk1, g, tc), lambda i, j: (0, 0, i, j)),
        out_shape=jax.ShapeDtypeStruct((2, nk1, n2, width), BF16),
        scratch_shapes=[pltpu.VMEM(f.shape, BF16)],
        compiler_params=_params("arbitrary", "arbitrary"),
        name="dft_stage1",
    )(f, src.reshape(n1 // 2, n2, src.shape[1]), tw(plan["tw1_cos_lanes"]), tw(plan["tw1_sin_lanes"]))


def _istage1_body(f_ref, q_ref, x_ref, v_ref, skip_ref, o_ref, fb_ref):
    _, nk1, g, tc = q_ref.shape
    n1h = o_ref.shape[0]
    nk1_pad = f_ref.shape[1] // (2 * SUBLANES)

    @pl.when(_first_step())
    def _():
        fb_ref[...] = f_ref[...].astype(BF16)

    q = q_ref[...].astype(F32)
    q = jnp.concatenate([q, jnp.zeros((2, nk1_pad - nk1, g, tc), F32)], axis=1)
    parts = []
    for a in range(g // SUBLANES):
        qq = q[:, :, a * SUBLANES:(a + 1) * SUBLANES, :].reshape(2 * nk1_pad * SUBLANES, tc)
        y = jnp.dot(fb_ref[...], qq.astype(BF16), preferred_element_type=F32)
        parts.append(y.reshape(n1h, SUBLANES, tc))
    y = jnp.concatenate(parts, axis=1)
    o_ref[...] = (x_ref[...] * (y + v_ref[...] * skip_ref[...])).astype(o_ref.dtype)


def _idft_gate(plan, q, x_src, x_part, v_src, v_part, skip, out_dtype):
    n1, n2, nk1 = plan["n1"], plan["n2"], plan["nk1"]
    d = q.shape[3]
    g = DFT_ROW_GROUP
    tc = _tile(d, 512, LANES)
    nc = d // tc
    f = jnp.asarray(plan["f1i"])
    tok = lambda part: pl.BlockSpec((n1 // 2, g, tc), lambda i, j: (0, i, part * nc + j))
    view = lambda arr: arr.reshape(n1 // 2, n2, arr.shape[1])
    out = pl.pallas_call(
        _istage1_body,
        grid=(n2 // g, nc),
        in_specs=[pl.BlockSpec(f.shape, lambda i, j: (0, 0)),
                  pl.BlockSpec((2, nk1, g, tc), lambda i, j: (0, 0, i, j)),
                  tok(x_part), tok(v_part), pl.BlockSpec((1, tc), lambda i, j: (0, j))],
        out_specs=tok(0),
        out_shape=jax.ShapeDtypeStruct((n1 // 2, n2, d), out_dtype),
        scratch_shapes=[pltpu.VMEM(f.shape, BF16)],
        compiler_params=_params("arbitrary", "arbitrary"),
        name="idft_gate",
    )(f, q, view(x_src), view(v_src), skip.reshape(1, d))
    return out.reshape(n1 // 2 * n2, d)


SPECTRAL_ROWS_PER_STEP = 5


def _spectral_body(a_ref, ah_ref, ss_ref, g2_ref, g2i_ref, c_ref, s_ref, o_ref):
    _, kb, n2, td = a_ref.shape
    g2 = g2_ref[...].astype(BF16)
    g2i = g2i_ref[...].astype(BF16)
    scale = lax.rsqrt(ss_ref[...] + EPS)
    reps = td // LANES
    for b in range(kb):
        stack = lambda ref: jnp.concatenate([ref[0, b], ref[1, b]], axis=0)
        x = jnp.dot(g2, stack(a_ref), preferred_element_type=F32)
        hf = jnp.dot(g2, stack(ah_ref), preferred_element_type=F32)
        xr, xi, hr, hi = x[:n2], x[n2:], hf[:n2], hf[n2:]
        y = jnp.concatenate([(xr * hr - xi * hi) * scale, (xr * hi + xi * hr) * scale], axis=0)
        p = jnp.dot(g2i, y.astype(BF16), preferred_element_type=F32)
        pr, pi = p[:n2], p[n2:]
        c = jnp.tile(c_ref[b], (1, reps))
        s = jnp.tile(s_ref[b], (1, reps))
        o_ref[0, b] = (pr * c - pi * s).astype(o_ref.dtype)
        o_ref[1, b] = (pr * s + pi * c).astype(o_ref.dtype)


def _spectral_conv(plan, a, ah, ss, order):
    n1, n2, d = a.shape[1:]
    td = _tile(d, 1024, LANES)
    nd = d // td
    kb = max(b for b in range(1, SPECTRAL_ROWS_PER_STEP + 1) if n1 % b == 0)
    blk = lambda off: pl.BlockSpec((2, kb, n2, td), lambda k1, j: (0, k1, 0, off + j))
    const = pl.BlockSpec((2 * n2, 2 * n2), lambda k1, j: (0, 0))
    tw = pl.BlockSpec((kb, n2, LANES), lambda k1, j: (k1, 0, 0))
    return pl.pallas_call(
        _spectral_body,
        grid=(n1 // kb, nd),
        in_specs=[blk(0), blk(order * nd), pl.BlockSpec((1, td), lambda k1, j: (0, order * nd + j)),
                  const, const, tw, tw],
        out_specs=blk(0),
        out_shape=jax.ShapeDtypeStruct((2, n1, n2, d), BF16),
        compiler_params=_params("parallel", "parallel"),
        name="spectral_conv",
    )(a, ah, ss, jnp.asarray(plan["g2"]), jnp.asarray(plan["g2i"]),
      plan["tw2_cos_lanes"], plan["tw2_sin_lanes"])


@functools.lru_cache(maxsize=None)
def _dft_plan_single(n_tok):
    n = 2 * n_tok
    af = 2.0 * np.pi * np.outer(np.arange(n), np.arange(n_tok)) / n
    fwd = np.concatenate([np.cos(af), -np.sin(af)], axis=0)
    ai = 2.0 * np.pi * np.outer(np.arange(n_tok // 2, n_tok // 2 + n_tok), np.arange(n)) / n
    inv = np.concatenate([np.cos(ai), -np.sin(ai)], axis=1) / n
    return dict(n=n, fwd=fwd.astype(np.float32), inv=inv.astype(np.float32))


def _cmul_body(x_ref, h_ref, ss_ref, o_ref):
    n = x_ref.shape[0] // 2
    xr, xi, hr, hi = x_ref[:n], x_ref[n:], h_ref[:n], h_ref[n:]
    scale = lax.rsqrt(ss_ref[...] + EPS)
    o_ref[:n] = ((xr * hr - xi * hi) * scale).astype(o_ref.dtype)
    o_ref[n:] = ((xr * hi + xi * hr) * scale).astype(o_ref.dtype)


def _cmul(x, h, ss, d, order):
    rows = x.shape[0]
    td = _tile(d, 512, LANES)
    nd = d // td
    return pl.pallas_call(
        _cmul_body,
        grid=(nd,),
        in_specs=[pl.BlockSpec((rows, td), lambda j: (0, j)),
                  pl.BlockSpec((rows, td), lambda j: (0, order * nd + j)),
                  pl.BlockSpec((1, td), lambda j: (0, order * nd + j))],
        out_specs=pl.BlockSpec((rows, td), lambda j: (0, j)),
        out_shape=jax.ShapeDtypeStruct((rows, d), BF16),
        compiler_params=_params("parallel"),
        name="spectrum_product",
    )(x, h, ss)


def _hyena_mixer(x, norm, w_in, layer, b_in, conv_w, conv_b, fparams, skip, plans):
    n_tok, d = x.shape
    u = _norm_matmul_conv(x, *norm, w_in, layer, b_in, conv_w, conv_b, name="hyena_in")
    filt, ss = _hyena_filters(n_tok, d, *fparams)

    if n_tok <= SINGLE_STAGE_MAX_LEN:
        plan = _dft_plan_single(n_tok)
        fwd, inv = jnp.asarray(plan["fwd"]), jnp.asarray(plan["inv"])
        part_cols = lambda p: (lambda tn: (lambda j: p * (d // tn) + j))
        hf = _matmul(fwd, filt, out_dtype=F32, tm=2 * plan["n"], tn=512, name="dft_filter")
        z_src, z_cols = u, part_cols(HYENA_ORDER)
        for order in range(HYENA_ORDER):
            x = _matmul(fwd, z_src, out_dtype=F32, tm=2 * plan["n"], tn=512, w_cols=(d, z_cols),
                        name="dft_signal")
            y = _cmul(x, hf, ss, d, order)
            last = order == HYENA_ORDER - 1
            z_src = _matmul(inv, y, out_dtype=BF16 if last else F32, epilogue=_ep_gate, tn=512,
                            extras=(_tile_extra(u, part_cols(order)), _tile_extra(z_src, z_cols),
                                    _row_extra(skip[order])), name="idft_gate")
            z_cols = part_cols(0)
        return z_src

    if n_tok not in plans:
        base = _dft_plan(n_tok)
        plans[n_tok] = dict(base, **{name + "_lanes": _lane_table(base[name])
                                     for name in ("tw1_cos", "tw1_sin", "tw2_cos", "tw2_sin")})
    plan = plans[n_tok]
    ah = _dft_stage1(plan, filt, HYENA_ORDER * d)
    z_src, z_part = u, HYENA_ORDER
    for order in range(HYENA_ORDER):
        a = _dft_stage1(plan, z_src, d, z_part)
        q = _spectral_conv(plan, a, ah, ss, order)
        last = order == HYENA_ORDER - 1
        z_src = _idft_gate(plan, q, u, order, z_src, z_part, skip[order], BF16 if last else F32)
        z_part = 0
    return z_src


def kernel(x, c, ctx, c_ctx, mod_w, mod_b, norm_mix_g, norm_mlp_g, attn_w_in, attn_w_out, attn_sink, attn_q_norm, attn_k_norm, hy_w_in, hy_b_in, hy_conv_w, hy_conv_b, hy_f_w1, hy_f_b1, hy_f_freq1, hy_f_w2, hy_f_b2, hy_f_freq2, hy_f_w3, hy_skip, hy_w_out, hy_b_out, mlp_w1, mlp_w2, final_g):
    batch, n_lat, d = x.shape
    n_ctx = ctx.shape[1]
    depth = mod_w.shape[0]
    assert batch == 1 and n_lat % GRID_W == 0 and n_lat % WINDOW == 0
    last_ctx_layer = 2 * ((depth - 1) // 2)
    rope = _rope_tables(n_lat)
    d_ff = mlp_w1.shape[2]
    attn_w_out_bf, hy_w_out_bf = attn_w_out.astype(BF16), hy_w_out.astype(BF16)
    dft_plans = {}

    cond = jnp.zeros((16, d), F32).at[0].set(c[0]).at[1].set(c_ctx)
    mod = _modulation(cond, mod_w, mod_b)

    xs = x[0]
    cs = ctx[0]
    for i in range(depth):
        j = i // 2
        is_attn = i % 2 == 0
        ctx_updated = i < last_ctx_layer
        sh1, sc1, g1, sh2, sc2, g2 = jnp.split(mod[i, 0:1], N_MOD, axis=-1)
        csh1, csc1, cg1, csh2, csc2, cg2 = jnp.split(mod[i, 1:2], N_MOD, axis=-1)
        if is_attn:
            qkv, h_lat = _attn_in(xs, norm_mix_g[i], sh1, sc1, attn_w_in, j, attn_q_norm[j], attn_k_norm[j], rope)
            qkv_c, h_ctx = _attn_in(cs, norm_mix_g[i], csh1, csc1, attn_w_in, j, attn_q_norm[j], attn_k_norm[j],
                                    None)
            wv_t = attn_w_in[j][:, Q_WIDTH + KV_WIDTH:].T
            vt, vt_c = _project_t(wv_t, h_lat), _project_t(wv_t, h_ctx)
            o_a = _window_attention(qkv, vt, qkv_c, vt_c, attn_sink[j])
            o_b = _global_attention(qkv, vt, qkv_c, vt_c)
            xs = _attn_out(o_a, o_b, attn_w_out_bf, j, xs, g1)
            if ctx_updated:
                o_c = _context_attention(qkv_c, attn_sink[j])
                cs = _matmul(o_c, attn_w_out_bf, layer=j, out_dtype=F32, epilogue=_ep_resid,
                             extras=(_tile_extra(cs), _row_extra(cg1)), name="attn_out_ctx")
        else:
            fparams = (hy_f_w1[j], hy_f_b1[j], hy_f_freq1[j], hy_f_w2[j], hy_f_b2[j], hy_f_freq2[j], hy_f_w3[j])
            z = _hyena_mixer(xs, (norm_mix_g[i], sh1, sc1), hy_w_in, j, hy_b_in[j], hy_conv_w[j], hy_conv_b[j],
                             fparams, hy_skip[j], dft_plans)
            xs = _matmul(z, hy_w_out_bf, layer=j, out_dtype=F32, epilogue=_ep_resid_bias, tm=512, tn=2048,
                         extras=(_tile_extra(xs), _row_extra(g1), _row_extra(hy_b_out[j])), name="hyena_out")
            if ctx_updated:
                z_c = _hyena_mixer(cs, (norm_mix_g[i], csh1, csc1), hy_w_in, j, hy_b_in[j], hy_conv_w[j],
                                   hy_conv_b[j], fparams, hy_skip[j], dft_plans)
                cs = _matmul(z_c, hy_w_out_bf, layer=j, out_dtype=F32, epilogue=_ep_resid_bias,
                             extras=(_tile_extra(cs), _row_extra(cg1), _row_extra(hy_b_out[j])),
                             name="hyena_out_ctx")
        a1 = _norm_matmul(xs, norm_mlp_g[i], sh2, sc2, mlp_w1, layer=i, out_dtype=BF16, epilogue=_ep_relu2,
                          name="mlp_up")
        xs = _matmul(a1, mlp_w2, layer=i, out_dtype=F32, epilogue=_ep_resid, tm=BIG_TM, tn=256, tk=d_ff,
                     extras=(_tile_extra(xs), _row_extra(g2)), name="mlp_down")
        if ctx_updated:
            a1c = _norm_matmul(cs, norm_mlp_g[i], csh2, csc2, mlp_w1, layer=i, out_dtype=BF16,
                               epilogue=_ep_relu2, name="mlp_up_ctx")
            cs = _matmul(a1c, mlp_w2, layer=i, out_dtype=F32, epilogue=_ep_resid, tn=512, tk=d_ff,
                         extras=(_tile_extra(cs), _row_extra(cg2)), name="mlp_down_ctx")
    return _rms_norm(xs, final_g, out_dtype=F32)[None]
```

```python
import functools
import math

import numpy as np
import jax
import jax.numpy as jnp
from jax import lax
from jax.experimental import pallas as pl
from jax.experimental.pallas import tpu as pltpu

F32 = jnp.float32
BF16 = jnp.bfloat16

GRID_W = 64
HEAD_DIM = 128
A_Q_HEADS = 8
A_KV_HEADS = 2
B_Q_HEADS = 8
B_KV_HEADS = 2
GROUP = A_Q_HEADS // A_KV_HEADS
WINDOW = 128
ROPE_THETA = 10000.0
AXIS_DIM = HEAD_DIM // 2
Q_WIDTH = (A_Q_HEADS + B_Q_HEADS) * HEAD_DIM
KV_HEADS = A_KV_HEADS + B_KV_HEADS
KV_WIDTH = KV_HEADS * HEAD_DIM
HYENA_ORDER = 2
SHORT_CONV = 3
FILTER_EMB = 33
DECAY_TARGET = 1e-2
FAST_DECAY_PCT = 0.3
SLOW_DECAY_PCT = 1.5
MAX_DECAY = math.log(DECAY_TARGET) / FAST_DECAY_PCT
MIN_DECAY = math.log(DECAY_TARGET) / SLOW_DECAY_PCT
N_MOD = 6
EPS = 1e-6
LOG2_E = math.log2(math.e)
LN_2 = math.log(2.0)

VMEM_LIMIT_BYTES = 56 * 1024 * 1024
LANES = 128
SINGLE_STAGE_MAX_LEN = 512
BIG_TM = 1024


def _params(*sem):
    return pltpu.CompilerParams(dimension_semantics=sem, vmem_limit_bytes=VMEM_LIMIT_BYTES)


def _tile(n, pref, align):
    if n <= pref:
        return n
    t = (pref // align) * align
    while t >= align:
        if n % t == 0:
            return t
        t -= align
    return n


def _mod_body(a_ref, w_ref, b_ref, o_ref):
    a = a_ref[...]
    act = a * (1.0 / (1.0 + jnp.exp(-a)))
    o_ref[...] = jnp.dot(act.astype(BF16), w_ref[...].astype(BF16),
                         preferred_element_type=F32) + b_ref[...]


def _modulation(cond, mod_w, mod_b):
    depth, d, n = mod_w.shape
    r = cond.shape[0]
    tn = _tile(n, 1024, LANES)
    return pl.pallas_call(
        _mod_body,
        grid=(depth, n // tn),
        in_specs=[pl.BlockSpec((r, d), lambda l, j: (0, 0)),
                  pl.BlockSpec((None, d, tn), lambda l, j: (l, 0, j)),
                  pl.BlockSpec((None, 1, tn), lambda l, j: (l, 0, j))],
        out_specs=pl.BlockSpec((None, r, tn), lambda l, j: (l, 0, j)),
        out_shape=jax.ShapeDtypeStruct((depth, r, n), F32),
        compiler_params=_params("parallel", "parallel"),
        name="modulation",
    )(cond, mod_w, mod_b.reshape(depth, 1, n))


def _norm_body(x_ref, g_ref, *rest, modulated):
    x = x_ref[...]
    y = x * lax.rsqrt(jnp.mean(x * x, axis=-1, keepdims=True) + EPS) * g_ref[...]
    if modulated:
        sh_ref, sc_ref, o_ref = rest
        y = y * (1.0 + sc_ref[...]) + sh_ref[...]
    else:
        (o_ref,) = rest
    o_ref[...] = y.astype(o_ref.dtype)


def _rms_norm(x, g, shift=None, scale=None, out_dtype=None):
    t, d = x.shape
    out_dtype = BF16 if out_dtype is None else out_dtype
    tm = _tile(t, 512, 16)
    row = pl.BlockSpec((1, d), lambda i: (0, 0))
    vecs = [g.reshape(1, d)]
    if shift is not None:
        vecs += [shift.reshape(1, d), scale.reshape(1, d)]
    return pl.pallas_call(
        functools.partial(_norm_body, modulated=shift is not None),
        grid=(t // tm,),
        in_specs=[pl.BlockSpec((tm, d), lambda i: (i, 0))] + [row] * len(vecs),
        out_specs=pl.BlockSpec((tm, d), lambda i: (i, 0)),
        out_shape=jax.ShapeDtypeStruct((t, d), out_dtype),
        compiler_params=_params("parallel"),
        name="rms_norm",
    )(x, *vecs)


def _ep_none(acc):
    return acc


def _ep_relu2(acc):
    return jnp.square(jnp.maximum(acc, 0.0))


def _ep_resid(acc, x, g):
    return x + g * acc


def _ep_resid_bias(acc, x, g, b):
    return x + g * (acc + b)


def _ep_gate(acc, x, v, skip):
    return x * (acc + v * skip)


def _mm_body(*refs, nk, epilogue, n_extra):
    a_ref, w_ref = refs[0], refs[1]
    extra = refs[2:2 + n_extra]
    o_ref = refs[2 + n_extra]
    a = a_ref[...].astype(BF16)
    w = w_ref[...].astype(BF16)
    part = jnp.dot(a, w, preferred_element_type=F32)
    if nk == 1:
        o_ref[...] = epilogue(part, *[e[...] for e in extra]).astype(o_ref.dtype)
        return
    acc_ref = refs[3 + n_extra]
    k = pl.program_id(2)

    @pl.when(k == 0)
    def _():
        acc_ref[...] = part

    @pl.when(k > 0)
    def _():
        acc_ref[...] += part

    @pl.when(k == nk - 1)
    def _():
        o_ref[...] = epilogue(acc_ref[...], *[e[...] for e in extra]).astype(o_ref.dtype)


def _weight_spec(w, layer, tk, tn, index):
    if layer is None:
        return pl.BlockSpec((tk, tn), index)
    return pl.BlockSpec((None, tk, tn), lambda *ids: (layer,) + tuple(index(*ids)))


def _matmul(a, w, *, out_dtype, epilogue=_ep_none, extras=(), tm=512, tn=1024, tk=2048,
            w_cols=None, layer=None, name="matmul"):
    m, kdim = a.shape
    n = w.shape[-1] if w_cols is None else w_cols[0]
    tm = _tile(m, tm, 16)
    tn = _tile(n, tn, LANES)
    tk = _tile(kdim, tk, LANES)
    nk = kdim // tk
    wmap = (lambda j: j) if w_cols is None else w_cols[1](tn)
    in_specs = [pl.BlockSpec((tm, tk), lambda i, j, k: (i, k)),
                _weight_spec(w, layer, tk, tn, lambda i, j, k: (k, wmap(j)))]
    in_specs += [fn(tm, tn) for _, fn in extras]
    return pl.pallas_call(
        functools.partial(_mm_body, nk=nk, epilogue=epilogue, n_extra=len(extras)),
        grid=(m // tm, n // tn, nk),
        in_specs=in_specs,
        out_specs=pl.BlockSpec((tm, tn), lambda i, j, k: (i, j)),
        out_shape=jax.ShapeDtypeStruct((m, n), out_dtype),
        scratch_shapes=[pltpu.VMEM((tm, tn), F32)] if nk > 1 else [],
        compiler_params=_params("parallel", "parallel", "arbitrary"),
        name=name,
    )(a, w, *[arr for arr, _ in extras])


def _nmm_body(x_ref, g_ref, sh_ref, sc_ref, w_ref, *rest, epilogue, n_extra):
    extra = rest[:n_extra]
    o_ref, h_ref = rest[n_extra], rest[n_extra + 1]

    @pl.when(pl.program_id(1) == 0)
    def _():
        x = x_ref[...]
        y = x * lax.rsqrt(jnp.mean(x * x, axis=-1, keepdims=True) + EPS) * g_ref[...]
        h_ref[...] = (y * (1.0 + sc_ref[...]) + sh_ref[...]).astype(h_ref.dtype)

    acc = jnp.dot(h_ref[...], w_ref[...].astype(BF16), preferred_element_type=F32)
    o_ref[...] = epilogue(acc, *[e[...] for e in extra]).astype(o_ref.dtype)


def _norm_matmul(x, g, shift, scale, w, *, out_dtype, epilogue=_ep_none, extras=(), tm=1024, tn=1024,
                 layer=None, name="norm_matmul"):
    m, d = x.shape
    n = w.shape[-1]
    tm = _tile(m, tm, 16)
    tn = _tile(n, tn, LANES)
    vec = pl.BlockSpec((1, d), lambda i, j: (0, 0))
    with_k = lambda spec_fn: (lambda bs: pl.BlockSpec(bs.block_shape, lambda i, j: bs.index_map(i, j, 0)))(
        spec_fn(tm, tn))
    return pl.pallas_call(
        functools.partial(_nmm_body, epilogue=epilogue, n_extra=len(extras)),
        grid=(m // tm, n // tn),
        in_specs=[pl.BlockSpec((tm, d), lambda i, j: (i, 0)), vec, vec, vec,
                  _weight_spec(w, layer, d, tn, lambda i, j: (0, j))] + [with_k(fn) for _, fn in extras],
        out_specs=pl.BlockSpec((tm, tn), lambda i, j: (i, j)),
        out_shape=jax.ShapeDtypeStruct((m, n), out_dtype),
        scratch_shapes=[pltpu.VMEM((tm, d), BF16)],
        compiler_params=_params("arbitrary", "arbitrary"),
        name=name,
    )(x, g.reshape(1, d), shift.reshape(1, d), scale.reshape(1, d), w, *[arr for arr, _ in extras])


HALO = 16


def _nmm_conv_body(x_ref, xp_ref, xn_ref, g_ref, sh_ref, sc_ref, w_ref, b_ref, cw_ref, cb_ref, o_ref, h_ref,
                   *, n_row_tiles):
    i = pl.program_id(0)
    tm = x_ref.shape[0]

    @pl.when(pl.program_id(1) == 0)
    def _():
        def norm(ref):
            x = ref[...]
            y = x * lax.rsqrt(jnp.mean(x * x, axis=-1, keepdims=True) + EPS) * g_ref[...]
            return (y * (1.0 + sc_ref[...]) + sh_ref[...]).astype(h_ref.dtype)

        h_ref[:HALO] = norm(xp_ref)
        h_ref[HALO:HALO + tm] = norm(x_ref)
        h_ref[HALO + tm:] = norm(xn_ref)

    u = jnp.dot(h_ref[...], w_ref[...].astype(BF16), preferred_element_type=F32) + b_ref[...]
    before = jnp.where(i > 0, u[:HALO], 0.0)
    after = jnp.where(i + 1 < n_row_tiles, u[HALO + tm:], 0.0)
    u = jnp.concatenate([before, u[HALO:HALO + tm], after], axis=0)
    rows = u.shape[0]
    y = (pltpu.roll(u, 1, 0) * cw_ref[0:1, :] + u * cw_ref[1:2, :]
         + pltpu.roll(u, rows - 1, 0) * cw_ref[2:3, :] + cb_ref[...])
    o_ref[...] = y[HALO:HALO + tm]


def _norm_matmul_conv(x, g, shift, scale, w, layer, b, conv_w, conv_b, *, tm=1024, tn=512, name):
    m, d = x.shape
    n = w.shape[-1]
    tm = _tile(m, tm, HALO)
    tn = _tile(n, tn, LANES)
    hb = tm // HALO
    n_row_tiles = m // tm
    vec = pl.BlockSpec((1, d), lambda i, j: (0, 0))
    col = lambda rows: pl.BlockSpec((rows, tn), lambda i, j: (0, j))
    return pl.pallas_call(
        functools.partial(_nmm_conv_body, n_row_tiles=n_row_tiles),
        grid=(n_row_tiles, n // tn),
        in_specs=[pl.BlockSpec((tm, d), lambda i, j: (i, 0)),
                  pl.BlockSpec((HALO, d), lambda i, j: (jnp.maximum(i * hb - 1, 0), 0)),
                  pl.BlockSpec((HALO, d), lambda i, j: (jnp.minimum((i + 1) * hb, m // HALO - 1), 0)),
                  vec, vec, vec, _weight_spec(w, layer, d, tn, lambda i, j: (0, j)),
                  col(1), col(SHORT_CONV), col(1)],
        out_specs=pl.BlockSpec((tm, tn), lambda i, j: (i, j)),
        out_shape=jax.ShapeDtypeStruct((m, n), F32),
        scratch_shapes=[pltpu.VMEM((tm + 2 * HALO, d), BF16)],
        compiler_params=_params("arbitrary", "arbitrary"),
        name=name,
    )(x, x, x, g.reshape(1, d), shift.reshape(1, d), scale.reshape(1, d), w, b.reshape(1, n), conv_w,
      conv_b.reshape(1, n))


def _row_extra(vec, period_blocks=None):
    vec = vec.reshape(1, -1)
    if period_blocks is None:
        return vec, lambda tm, tn: pl.BlockSpec((1, tn), lambda i, j, k: (0, j))
    return vec, lambda tm, tn: pl.BlockSpec((1, tn), lambda i, j, k: (0, j % period_blocks(tn)))


def _tile_extra(arr, colmap=None):
    cm = colmap if colmap is not None else (lambda tn: (lambda j: j))
    return arr, lambda tm, tn: pl.BlockSpec((tm, tn), lambda i, j, k: (i, cm(tn)(j)))


def _rope_tables(n_tok):
    rows = n_tok // GRID_W
    row = jnp.repeat(jnp.arange(rows, dtype=F32), GRID_W)
    col = jnp.tile(jnp.arange(GRID_W, dtype=F32), rows)
    inv = ROPE_THETA ** (-jnp.arange(0, AXIS_DIM, 2, dtype=F32) / AXIS_DIM)
    ang_r = row[:, None] * inv[None]
    ang_c = col[:, None] * inv[None]
    cos = jnp.concatenate([jnp.cos(ang_r)] * 2 + [jnp.cos(ang_c)] * 2, axis=-1)
    sin = jnp.concatenate([-jnp.sin(ang_r), jnp.sin(ang_r), -jnp.sin(ang_c), jnp.sin(ang_c)], axis=-1)
    return cos, sin


HEADS_PER_TILE = 8
ATTN_IN_WIDTH = Q_WIDTH + 2 * KV_WIDTH


def _attn_in_body(*refs, rope):
    if rope:
        x_ref, g_ref, sh_ref, sc_ref, w_ref, qn_ref, kn_ref, cos_ref, sin_ref, o_ref, h_ref = refs
        cos, sin = cos_ref[...], sin_ref[...]
        lane = lax.broadcasted_iota(jnp.int32, cos.shape, 1)
        first = (lane % (AXIS_DIM)) < (AXIS_DIM // 2)
    else:
        x_ref, g_ref, sh_ref, sc_ref, w_ref, qn_ref, kn_ref, o_ref, h_ref = refs
    j = pl.program_id(1)
    scale = HEAD_DIM ** -0.5
    quarter = AXIS_DIM // 2

    @pl.when(j == 0)
    def _():
        x = x_ref[...]
        y = x * lax.rsqrt(jnp.mean(x * x, axis=-1, keepdims=True) + EPS) * g_ref[...]
        h_ref[...] = (y * (1.0 + sc_ref[...]) + sh_ref[...]).astype(h_ref.dtype)

    acc = jnp.dot(h_ref[...], w_ref[...].astype(BF16), preferred_element_type=F32)

    def rot(x):
        if not rope:
            return x
        swapped = jnp.where(first, pltpu.roll(x, HEAD_DIM - quarter, 1), pltpu.roll(x, quarter, 1))
        return x * cos + swapped * sin

    def nrm(x, g_ref_):
        return x * lax.rsqrt(jnp.mean(x * x, axis=-1, keepdims=True) + EPS) * g_ref_[...]

    q_a = lambda x: rot(x) * scale
    q_b = lambda x: rot(nrm(x, qn_ref)) * (scale * LOG2_E)
    k_b = lambda x: rot(nrm(x, kn_ref))
    kinds = ([q_a] * A_Q_HEADS + [q_b] * B_Q_HEADS + [rot] * A_KV_HEADS + [k_b] * B_KV_HEADS
             + [lambda x: x] * KV_HEADS)

    def emit(tile_kinds):
        for hh, kind in enumerate(tile_kinds):
            sl = slice(hh * HEAD_DIM, (hh + 1) * HEAD_DIM)
            o_ref[:, sl] = kind(acc[:, sl]).astype(o_ref.dtype)

    for t in range(len(kinds) // HEADS_PER_TILE):
        pl.when(j == t)(functools.partial(emit, kinds[t * HEADS_PER_TILE:(t + 1) * HEADS_PER_TILE]))


def _attn_in(x, g, shift, scale, w, layer, q_norm, k_norm, rope):
    t, d = x.shape
    tm = _tile(t, 512, 16)
    tn = HEADS_PER_TILE * HEAD_DIM
    vec = lambda n: pl.BlockSpec((1, n), lambda i, j: (0, 0))
    in_specs = [pl.BlockSpec((tm, d), lambda i, j: (i, 0)), vec(d), vec(d), vec(d),
                _weight_spec(w, layer, d, tn, lambda i, j: (0, j)), vec(HEAD_DIM), vec(HEAD_DIM)]
    args = [x, g.reshape(1, d), shift.reshape(1, d), scale.reshape(1, d), w,
            q_norm.reshape(1, HEAD_DIM), k_norm.reshape(1, HEAD_DIM)]
    if rope is not None:
        in_specs += [pl.BlockSpec((tm, HEAD_DIM), lambda i, j: (i, 0))] * 2
        args += list(rope)
    return pl.pallas_call(
        functools.partial(_attn_in_body, rope=rope is not None),
        grid=(t // tm, ATTN_IN_WIDTH // tn),
        in_specs=in_specs,
        out_specs=[pl.BlockSpec((tm, tn), lambda i, j: (i, j)), pl.BlockSpec((tm, d), lambda i, j: (i, 0))],
        out_shape=[jax.ShapeDtypeStruct((t, ATTN_IN_WIDTH), BF16), jax.ShapeDtypeStruct((t, d), BF16)],
        compiler_params=_params("arbitrary", "arbitrary"),
        name="attn_in",
    )(*args)


def _stack_heads(q_ref):
    return jnp.concatenate([q_ref[:, g * HEAD_DIM:(g + 1) * HEAD_DIM] for g in range(GROUP)], axis=0)


def _unstack_heads(o_ref, out, t):
    for g in range(GROUP):
        o_ref[:, g * HEAD_DIM:(g + 1) * HEAD_DIM] = out[g * t:(g + 1) * t].astype(o_ref.dtype)


def _sink_column(sink_ref, first, t):
    head = lax.broadcasted_iota(jnp.int32, (GROUP * t, 1), 0) // t
    col = jnp.full((GROUP * t, 1), sink_ref[first], F32)
    for g in range(1, GROUP):
        col = jnp.where(head == g, sink_ref[first + g], col)
    return col


def _qk(q, k):
    return lax.dot_general(q, k, (((1,), (1,)), ((), ())), preferred_element_type=F32)


ONES_ROWS = 16


def _gattn_body(q_ref, kc_ref, vtc_ref, k_ref, vt_ref, o_ref, m_ref, acc_ref, sa_ref, sb_ref,
                *, tq, ck, n_chunks):
    q = _stack_heads(q_ref)
    m_ref[...] = jnp.full(m_ref.shape, -jnp.inf, F32)
    acc_ref[...] = jnp.zeros(acc_ref.shape, F32)

    def scores(c):
        return _qk(k_ref[pl.ds(pl.multiple_of(c * ck, ck), ck), :], q)

    def consume(st, vtb):
        m_old = m_ref[...]
        m_new = jnp.maximum(m_old, jnp.max(st, axis=0, keepdims=True))
        p = jnp.exp2(st - m_new).astype(BF16)
        lhs = jnp.concatenate([vtb, jnp.ones((ONES_ROWS, vtb.shape[1]), BF16)], axis=0)
        acc_ref[...] = jnp.exp2(m_old - m_new) * acc_ref[...] + jnp.dot(lhs, p, preferred_element_type=F32)
        m_ref[...] = m_new

    def values(c):
        return vt_ref[:, pl.ds(pl.multiple_of(c * ck, ck), ck)]

    sa_ref[...] = scores(0)
    consume(_qk(kc_ref[...], q), vtc_ref[...])

    def pair(c2, carry):
        c = 2 * c2
        sb_ref[...] = scores(c + 1)
        consume(sa_ref[...], values(c))
        sa_ref[...] = scores(c + 2)
        consume(sb_ref[...], values(c + 1))
        return carry

    lax.fori_loop(0, n_chunks // 2 - 1, pair, 0)
    last = n_chunks - 2
    sb_ref[...] = scores(last + 1)
    consume(sa_ref[...], values(last))
    consume(sb_ref[...], values(last + 1))
    _unstack_heads(o_ref, (acc_ref[:HEAD_DIM] / acc_ref[HEAD_DIM:HEAD_DIM + 1]).T, tq)


def _attn_out_body(oa_ref, ob_ref, w_ref, x_ref, g_ref, o_ref):
    ka = oa_ref.shape[1]
    w = w_ref[...].astype(BF16)
    acc = (jnp.dot(oa_ref[...], w[:ka], preferred_element_type=F32)
           + jnp.dot(ob_ref[...], w[ka:], preferred_element_type=F32))
    o_ref[...] = x_ref[...] + g_ref[...] * acc


def _attn_out(o_a, o_b, w, layer, x, gate):
    m, ka = o_a.shape
    kb = o_b.shape[1]
    n = w.shape[-1]
    tm = _tile(m, 512, 16)
    tn = _tile(n, 2048, LANES)
    return pl.pallas_call(
        _attn_out_body,
        grid=(m // tm, n // tn),
        in_specs=[pl.BlockSpec((tm, ka), lambda i, j: (i, 0)), pl.BlockSpec((tm, kb), lambda i, j: (i, 0)),
                  _weight_spec(w, layer, ka + kb, tn, lambda i, j: (0, j)),
                  pl.BlockSpec((tm, tn), lambda i, j: (i, j)), pl.BlockSpec((1, tn), lambda i, j: (0, j))],
        out_specs=pl.BlockSpec((tm, tn), lambda i, j: (i, j)),
        out_shape=jax.ShapeDtypeStruct((m, n), F32),
        compiler_params=_params("parallel", "parallel"),
        name="attn_out",
    )(o_a, o_b, w, x, gate.reshape(1, n))


K_COL0 = Q_WIDTH // HEAD_DIM


def _global_attention(qkv, vt, qkv_ctx, vt_ctx):
    s, n_ctx = qkv.shape[0], qkv_ctx.shape[0]
    tq = _tile(s, 512, LANES)
    ck = _tile(s // 2, 1024, LANES)
    n_chunks = s // ck
    assert n_chunks % 2 == 0
    rows = GROUP * tq
    gw = GROUP * HEAD_DIM
    kspec = lambda n: pl.BlockSpec((n, HEAD_DIM), lambda h, i: (0, K_COL0 + A_KV_HEADS + h))
    vspec = lambda n: pl.BlockSpec((HEAD_DIM, n), lambda h, i: (A_KV_HEADS + h, 0))
    return pl.pallas_call(
        functools.partial(_gattn_body, tq=tq, ck=ck, n_chunks=n_chunks),
        grid=(B_KV_HEADS, s // tq),
        in_specs=[pl.BlockSpec((tq, gw), lambda h, i: (i, A_KV_HEADS + h)),
                  kspec(n_ctx), vspec(n_ctx), kspec(s), vspec(s)],
        out_specs=pl.BlockSpec((tq, gw), lambda h, i: (i, h)),
        out_shape=jax.ShapeDtypeStruct((s, B_Q_HEADS * HEAD_DIM), BF16),
        scratch_shapes=[pltpu.VMEM((1, rows), F32), pltpu.VMEM((HEAD_DIM + ONES_ROWS, rows), F32),
                        pltpu.VMEM((ck, rows), F32), pltpu.VMEM((ck, rows), F32)],
        compiler_params=_params("parallel", "parallel"),
        name="global_attention",
    )(qkv, qkv_ctx, vt_ctx, qkv, vt)


def _project_t_body(w_ref, h_ref, o_ref, wt_ref):
    @pl.when(pl.program_id(0) == 0)
    def _():
        wt_ref[...] = w_ref[...].T.astype(wt_ref.dtype)

    o_ref[...] = _qk(wt_ref[...], h_ref[...]).astype(o_ref.dtype)


def _project_t(w, layer, col0, c, h):
    d = w.shape[1]
    t = h.shape[0]
    tt = _tile(t, 1024, LANES)
    return pl.pallas_call(
        _project_t_body,
        grid=(t // tt,),
        in_specs=[pl.BlockSpec((None, d, c), lambda i: (layer, 0, col0 // c)),
                  pl.BlockSpec((tt, d), lambda i: (i, 0))],
        out_specs=pl.BlockSpec((c, tt), lambda i: (0, i)),
        out_shape=jax.ShapeDtypeStruct((c, t), BF16),
        scratch_shapes=[pltpu.VMEM((c, d), BF16)],
        compiler_params=_params("arbitrary"),
        name="project_t",
    )(w, h)


WINDOW_BLOCKS = 2


def _wattn_body(sink_ref, q_ref, kc_ref, vtc_ref, kp_ref, k0_ref, kn_ref, vtp_ref, vt0_ref, vtn_ref, o_ref,
                *, n_tiles, n_ctx):
    h = pl.program_id(0)
    i = pl.program_id(1)
    w = WINDOW
    tq = q_ref.shape[0]
    q = _stack_heads(q_ref)
    kcat = jnp.concatenate([kc_ref[...], kp_ref[...], k0_ref[...], kn_ref[...]], axis=0)
    vtcat = jnp.concatenate([vtc_ref[...], vtp_ref[...], vt0_ref[...], vtn_ref[...]], axis=1)
    st = _qk(kcat, q)
    c = lax.broadcasted_iota(jnp.int32, st.shape, 0) - n_ctx
    r = lax.broadcasted_iota(jnp.int32, st.shape, 1) % tq
    lo = jnp.maximum(r, jnp.where(i >= 1, 0, w))
    hi = jnp.minimum(r + 2 * w, jnp.where(i + 1 < n_tiles, tq + 2 * w - 1, tq + w - 1))
    valid = (c < 0) | ((c >= lo) & (c <= hi))
    st = jnp.where(valid, st, -jnp.inf)
    head = lax.broadcasted_iota(jnp.int32, (1, GROUP * tq), 1) // tq
    sink = jnp.full((1, GROUP * tq), sink_ref[h * GROUP], F32)
    for g in range(1, GROUP):
        sink = jnp.where(head == g, sink_ref[h * GROUP + g], sink)
    m = jnp.maximum(jnp.max(st, axis=0, keepdims=True), sink)
    p = jnp.exp(st - m).astype(BF16)
    lhs = jnp.concatenate([vtcat, jnp.ones((ONES_ROWS, vtcat.shape[1]), BF16)], axis=0)
    acc = jnp.dot(lhs, p, preferred_element_type=F32)
    denom = acc[HEAD_DIM:HEAD_DIM + 1] + jnp.exp(sink - m)
    _unstack_heads(o_ref, (acc[:HEAD_DIM] / denom).T, tq)


def _window_attention(qkv, vt, qkv_ctx, vt_ctx, sink):
    s, n_ctx = qkv.shape[0], qkv_ctx.shape[0]
    w = WINDOW
    nb = s // w
    tb = WINDOW_BLOCKS if nb % WINDOW_BLOCKS == 0 else 1
    tq = tb * w
    n_tiles = nb // tb
    gw = GROUP * HEAD_DIM
    before = lambda i: jnp.maximum(i * tb - 1, 0)
    after = lambda i: jnp.minimum((i + 1) * tb, nb - 1)
    kblk = lambda pos: pl.BlockSpec((w, HEAD_DIM), lambda h, i: (pos(i), K_COL0 + h))
    vblk = lambda pos: pl.BlockSpec((HEAD_DIM, w), lambda h, i: (h, pos(i)))
    return pl.pallas_call(
        functools.partial(_wattn_body, n_tiles=n_tiles, n_ctx=n_ctx),
        grid=(A_KV_HEADS, n_tiles),
        in_specs=[pl.BlockSpec(memory_space=pltpu.SMEM),
                  pl.BlockSpec((tq, gw), lambda h, i: (i, h)),
                  pl.BlockSpec((n_ctx, HEAD_DIM), lambda h, i: (0, K_COL0 + h)),
                  pl.BlockSpec((HEAD_DIM, n_ctx), lambda h, i: (h, 0)),
                  kblk(before), pl.BlockSpec((tq, HEAD_DIM), lambda h, i: (i, K_COL0 + h)), kblk(after),
                  vblk(before), pl.BlockSpec((HEAD_DIM, tq), lambda h, i: (h, i)), vblk(after)],
        out_specs=pl.BlockSpec((tq, gw), lambda h, i: (i, h)),
        out_shape=jax.ShapeDtypeStruct((s, A_Q_HEADS * HEAD_DIM), BF16),
        compiler_params=_params("parallel", "parallel"),
        name="window_attention",
    )(sink, qkv, qkv_ctx, vt_ctx, qkv, qkv, qkv, vt, vt, vt)


def _cattn_body(sink_ref, q_ref, k_ref, v_ref, o_ref, *, n_ctx):
    h = pl.program_id(0)
    q = _stack_heads(q_ref)
    s = _qk(q, k_ref[...]) * jnp.where(h >= A_KV_HEADS, LN_2, 1.0)
    sink = _sink_column(sink_ref, h * GROUP, n_ctx)
    m = jnp.maximum(jnp.max(s, axis=-1, keepdims=True), sink)
    p = jnp.exp(s - m)
    denom = jnp.sum(p, axis=-1, keepdims=True) + jnp.exp(sink - m)
    out = jnp.dot(p.astype(BF16), v_ref[...], preferred_element_type=F32) / denom
    _unstack_heads(o_ref, out, n_ctx)


def _context_attention(qkv, sink):
    n_ctx = qkv.shape[0]
    gw = GROUP * HEAD_DIM
    sink_all = jnp.concatenate([sink.astype(F32), jnp.full((B_Q_HEADS,), -jnp.inf, F32)])
    kv = lambda col0: pl.BlockSpec((n_ctx, HEAD_DIM), lambda h: (0, col0 + h))
    return pl.pallas_call(
        functools.partial(_cattn_body, n_ctx=n_ctx),
        grid=(KV_HEADS,),
        in_specs=[pl.BlockSpec(memory_space=pltpu.SMEM),
                  pl.BlockSpec((n_ctx, gw), lambda h: (0, h)), kv(K_COL0), kv(K_COL0 + KV_HEADS)],
        out_specs=pl.BlockSpec((n_ctx, gw), lambda h: (0, h)),
        out_shape=jax.ShapeDtypeStruct((n_ctx, Q_WIDTH), BF16),
        compiler_params=_params("parallel"),
        name="context_attention",
    )(sink_all, qkv, qkv, qkv)


def _filter_body(feat_ref, w1_ref, b1_ref, f1_ref, w2_ref, b2_ref, f2_ref, w3_ref, dl_ref,
                 h_ref, ss_ref, *, n_tok):
    i = pl.program_id(0)
    tl = feat_ref.shape[0]
    mm = lambda a, b: jnp.dot(a.astype(BF16), b.astype(BF16), preferred_element_type=F32)
    hid = jnp.sin(f1_ref[...] * (mm(feat_ref[...], w1_ref[...]) + b1_ref[...]))
    hid = jnp.sin(f2_ref[...] * (mm(hid, w2_ref[...]) + b2_ref[...]))
    h = mm(hid, w3_ref[...])
    t = i * tl + lax.broadcasted_iota(jnp.int32, (tl, 1), 0)
    offs = jnp.abs(t - n_tok // 2).astype(F32) * (2.0 / n_tok)
    h = h * jnp.exp(-offs * dl_ref[...])
    h_ref[...] = h.astype(h_ref.dtype)

    @pl.when(i == 0)
    def _():
        ss_ref[...] = jnp.zeros(ss_ref.shape, F32)

    ss_ref[...] += jnp.sum(h * h, axis=0, keepdims=True)


def _hyena_filters(n_tok, d, w1, b1, fr1, w2, b2, fr2, w3):
    bands = (FILTER_EMB - 1) // 2
    t = jnp.linspace(0.0, 1.0, n_tok, dtype=F32)[:, None]
    wv = 2.0 * math.pi * jnp.arange(n_tok, dtype=F32)[:, None] / n_tok
    f = jnp.linspace(1e-4, bands - 1, bands, dtype=F32)[None]
    feats = jnp.concatenate([t, jnp.cos(f * wv), -jnp.sin(f * wv)], axis=-1)
    deltas = jnp.abs(jnp.linspace(MIN_DECAY, MAX_DECAY, d, dtype=F32))
    hidden = w1.shape[1]
    pad_e, pad_h = LANES - FILTER_EMB, LANES - hidden
    padv = lambda v: jnp.pad(v.reshape(1, hidden), ((0, 0), (0, pad_h)))
    c = HYENA_ORDER * d
    tl = _tile(n_tok, 256, 8)
    full = lambda shape: pl.BlockSpec(shape, lambda i: (0, 0))
    return pl.pallas_call(
        functools.partial(_filter_body, n_tok=n_tok),
        grid=(n_tok // tl,),
        in_specs=[pl.BlockSpec((tl, LANES), lambda i: (i, 0)),
                  full((LANES, LANES)), full((1, LANES)), full((1, LANES)),
                  full((LANES, LANES)), full((1, LANES)), full((1, LANES)),
                  full((LANES, c)), full((1, c))],
        out_specs=[pl.BlockSpec((tl, c), lambda i: (i, 0)), full((1, c))],
        out_shape=[jax.ShapeDtypeStruct((n_tok, c), BF16), jax.ShapeDtypeStruct((1, c), F32)],
        compiler_params=_params("arbitrary"),
        name="hyena_filter",
    )(jnp.pad(feats, ((0, 0), (0, pad_e))),
      jnp.pad(w1, ((0, pad_e), (0, pad_h))), padv(b1), padv(fr1),
      jnp.pad(w2, ((0, pad_h), (0, pad_h))), padv(b2), padv(fr2),
      jnp.pad(w3, ((0, pad_h), (0, 0))), jnp.tile(deltas, HYENA_ORDER).reshape(1, c))


def _lane_table(vals):
    return jnp.broadcast_to(jnp.asarray(vals)[..., None], vals.shape + (LANES,))


SUBLANES = 8
DFT_ROW_GROUP = 16


@functools.lru_cache(maxsize=None)
def _dft_plan(n_tok):
    n = 2 * n_tok
    log = n.bit_length() - 1
    assert 1 << log == n, "sequence length must be a power of two"
    n2 = 1 << (log // 2)
    n1 = n // n2
    assert n1 % 4 == 0 and n2 % DFT_ROW_GROUP == 0
    eye = np.eye(SUBLANES)
    nk1 = n1 // 2 + 1
    nk1_pad = -(-nk1 // SUBLANES) * SUBLANES
    a1 = 2.0 * np.pi * np.outer(np.arange(nk1), np.arange(n1 // 2)) / n1
    f1 = np.kron(np.concatenate([np.cos(a1), -np.sin(a1)], axis=0), eye)
    a2 = 2.0 * np.pi * np.outer(np.arange(n2), np.arange(n2)) / n2
    c2, s2 = np.cos(a2), np.sin(a2)
    g2 = np.block([[c2, s2], [-s2, c2]])
    g2i = np.block([[c2, -s2], [s2, c2]])
    rows = np.arange(n1 // 4, n1 // 4 + n1 // 2)
    a1i = 2.0 * np.pi * np.outer(rows, np.arange(nk1_pad)) / n1
    weight = np.where((np.arange(nk1_pad) == 0) | (np.arange(nk1_pad) == n1 // 2), 1.0, 2.0)
    weight = np.where(np.arange(nk1_pad) < nk1, weight, 0.0) / n
    f1i = np.kron(np.concatenate([np.cos(a1i) * weight, -np.sin(a1i) * weight], axis=1), eye)
    tw = 2.0 * np.pi * np.outer(np.arange(n2), np.arange(nk1)) / n
    tw1 = tw.reshape(n2 // SUBLANES, SUBLANES, nk1).transpose(0, 2, 1).reshape(n2 // SUBLANES, nk1 * SUBLANES)
    f32 = lambda m: m.astype(np.float32)
    return dict(n1=n1, n2=n2, nk1=nk1, nk1_pad=nk1_pad, f1=f32(f1), g2=f32(g2), g2i=f32(g2i), f1i=f32(f1i),
                tw1_cos=f32(np.cos(tw1)), tw1_sin=f32(np.sin(tw1)),
                tw2_cos=f32(np.cos(tw.T)), tw2_sin=f32(np.sin(tw.T)))


def _first_step():
    return (pl.program_id(0) == 0) & (pl.program_id(1) == 0)


def _stage1_body(f_ref, z_ref, c_ref, s_ref, o_ref, fb_ref):
    n1h, g, tc = z_ref.shape
    nk1 = o_ref.shape[1]
    half = nk1 * SUBLANES
    reps = tc // LANES

    @pl.when(_first_step())
    def _():
        fb_ref[...] = f_ref[...].astype(BF16)

    z = z_ref[...].astype(F32)
    re_parts, im_parts = [], []
    for a in range(g // SUBLANES):
        zz = z[:, a * SUBLANES:(a + 1) * SUBLANES, :].reshape(n1h * SUBLANES, tc)
        acc = jnp.dot(fb_ref[...], zz.astype(BF16), preferred_element_type=F32)
        re, im = acc[:half], acc[half:]
        c = jnp.tile(c_ref[a], (1, reps))
        s = jnp.tile(s_ref[a], (1, reps))
        re_parts.append((re * c + im * s).reshape(nk1, SUBLANES, tc))
        im_parts.append((im * c - re * s).reshape(nk1, SUBLANES, tc))
    o_ref[0] = jnp.concatenate(re_parts, axis=1).astype(o_ref.dtype)
    o_ref[1] = jnp.concatenate(im_parts, axis=1).astype(o_ref.dtype)


def _dft_stage1(plan, src, width, part=0):
    n1, n2, nk1 = plan["n1"], plan["n2"], plan["nk1"]
    g = DFT_ROW_GROUP
    tc = _tile(width, 1024, LANES)
    nc = width // tc
    sub = g // SUBLANES
    tw = lambda t: t.reshape(n2 // g, sub, nk1 * SUBLANES, LANES)
    tw_spec = pl.BlockSpec((None, sub, nk1 * SUBLANES, LANES), lambda i, j: (i, 0, 0, 0))
    f = jnp.asarray(plan["f1"])
    return pl.pallas_call(
        _stage1_body,
        grid=(n2 // g, nc),
        in_specs=[pl.BlockSpec(f.shape, lambda i, j: (0, 0)),
                  pl.BlockSpec((n1 // 2, g, tc), lambda i, j: (0, i, part * nc + j)),
                  tw_spec, tw_spec],
        out_specs=pl.BlockSpec((2, nk1, g, tc), lambda i, j: (0, 0, i, j)),
        out_shape=jax.ShapeDtypeStruct((2, nk1, n2, width), BF16),
        scratch_shapes=[pltpu.VMEM(f.shape, BF16)],
        compiler_params=_params("arbitrary", "arbitrary"),
        name="dft_stage1",
    )(f, src.reshape(n1 // 2, n2, src.shape[1]), tw(plan["tw1_cos_lanes"]), tw(plan["tw1_sin_lanes"]))


def _istage1_body(f_ref, q_ref, x_ref, v_ref, skip_ref, o_ref, fb_ref):
    _, nk1, g, tc = q_ref.shape
    n1h = o_ref.shape[0]
    nk1_pad = f_ref.shape[1] // (2 * SUBLANES)

    @pl.when(_first_step())
    def _():
        fb_ref[...] = f_ref[...].astype(BF16)

    q = q_ref[...].astype(F32)
    q = jnp.concatenate([q, jnp.zeros((2, nk1_pad - nk1, g, tc), F32)], axis=1)
    parts = []
    for a in range(g // SUBLANES):
        qq = q[:, :, a * SUBLANES:(a + 1) * SUBLANES, :].reshape(2 * nk1_pad * SUBLANES, tc)
        y = jnp.dot(fb_ref[...], qq.astype(BF16), preferred_element_type=F32)
        parts.append(y.reshape(n1h, SUBLANES, tc))
    y = jnp.concatenate(parts, axis=1)
    o_ref[...] = (x_ref[...] * (y + v_ref[...] * skip_ref[...])).astype(o_ref.dtype)


def _idft_gate(plan, q, x_src, x_part, v_src, v_part, skip, out_dtype):
    n1, n2, nk1 = plan["n1"], plan["n2"], plan["nk1"]
    d = q.shape[3]
    g = DFT_ROW_GROUP
    tc = _tile(d, 512, LANES)
    nc = d // tc
    f = jnp.asarray(plan["f1i"])
    tok = lambda part: pl.BlockSpec((n1 // 2, g, tc), lambda i, j: (0, i, part * nc + j))
    view = lambda arr: arr.reshape(n1 // 2, n2, arr.shape[1])
    out = pl.pallas_call(
        _istage1_body,
        grid=(n2 // g, nc),
        in_specs=[pl.BlockSpec(f.shape, lambda i, j: (0, 0)),
                  pl.BlockSpec((2, nk1, g, tc), lambda i, j: (0, 0, i, j)),
                  tok(x_part), tok(v_part), pl.BlockSpec((1, tc), lambda i, j: (0, j))],
        out_specs=tok(0),
        out_shape=jax.ShapeDtypeStruct((n1 // 2, n2, d), out_dtype),
        scratch_shapes=[pltpu.VMEM(f.shape, BF16)],
        compiler_params=_params("arbitrary", "arbitrary"),
        name="idft_gate",
    )(f, q, view(x_src), view(v_src), skip.reshape(1, d))
    return out.reshape(n1 // 2 * n2, d)


SPECTRAL_ROWS_PER_STEP = 5


def _spectral_body(a_ref, ah_ref, ss_ref, g2_ref, g2i_ref, c_ref, s_ref, o_ref):
    _, kb, n2, td = a_ref.shape
    g2 = g2_ref[...].astype(BF16)
    g2i = g2i_ref[...].astype(BF16)
    scale = lax.rsqrt(ss_ref[...] + EPS)
    reps = td // LANES
    for b in range(kb):
        stack = lambda ref: jnp.concatenate([ref[0, b], ref[1, b]], axis=0)
        x = jnp.dot(g2, stack(a_ref), preferred_element_type=F32)
        hf = jnp.dot(g2, stack(ah_ref), preferred_element_type=F32)
        xr, xi, hr, hi = x[:n2], x[n2:], hf[:n2], hf[n2:]
        y = jnp.concatenate([(xr * hr - xi * hi) * scale, (xr * hi + xi * hr) * scale], axis=0)
        p = jnp.dot(g2i, y.astype(BF16), preferred_element_type=F32)
        pr, pi = p[:n2], p[n2:]
        c = jnp.tile(c_ref[b], (1, reps))
        s = jnp.tile(s_ref[b], (1, reps))
        o_ref[0, b] = (pr * c - pi * s).astype(o_ref.dtype)
        o_ref[1, b] = (pr * s + pi * c).astype(o_ref.dtype)


def _spectral_conv(plan, a, ah, ss, order):
    n1, n2, d = a.shape[1:]
    td = _tile(d, 1024, LANES)
    nd = d // td
    kb = max(b for b in range(1, SPECTRAL_ROWS_PER_STEP + 1) if n1 % b == 0)
    blk = lambda off: pl.BlockSpec((2, kb, n2, td), lambda k1, j: (0, k1, 0, off + j))
    const = pl.BlockSpec((2 * n2, 2 * n2), lambda k1, j: (0, 0))
    tw = pl.BlockSpec((kb, n2, LANES), lambda k1, j: (k1, 0, 0))
    return pl.pallas_call(
        _spectral_body,
        grid=(n1 // kb, nd),
        in_specs=[blk(0), blk(order * nd), pl.BlockSpec((1, td), lambda k1, j: (0, order * nd + j)),
                  const, const, tw, tw],
        out_specs=blk(0),
        out_shape=jax.ShapeDtypeStruct((2, n1, n2, d), BF16),
        compiler_params=_params("parallel", "parallel"),
        name="spectral_conv",
    )(a, ah, ss, jnp.asarray(plan["g2"]), jnp.asarray(plan["g2i"]),
      plan["tw2_cos_lanes"], plan["tw2_sin_lanes"])


@functools.lru_cache(maxsize=None)
def _dft_plan_single(n_tok):
    n = 2 * n_tok
    af = 2.0 * np.pi * np.outer(np.arange(n), np.arange(n_tok)) / n
    fwd = np.concatenate([np.cos(af), -np.sin(af)], axis=0)
    ai = 2.0 * np.pi * np.outer(np.arange(n_tok // 2, n_tok // 2 + n_tok), np.arange(n)) / n
    inv = np.concatenate([np.cos(ai), -np.sin(ai)], axis=1) / n
    return dict(n=n, fwd=fwd.astype(np.float32), inv=inv.astype(np.float32))


def _cmul_body(x_ref, h_ref, ss_ref, o_ref):
    n = x_ref.shape[0] // 2
    xr, xi, hr, hi = x_ref[:n], x_ref[n:], h_ref[:n], h_ref[n:]
    scale = lax.rsqrt(ss_ref[...] + EPS)
    o_ref[:n] = ((xr * hr - xi * hi) * scale).astype(o_ref.dtype)
    o_ref[n:] = ((xr * hi + xi * hr) * scale).astype(o_ref.dtype)


def _cmul(x, h, ss, d, order):
    rows = x.shape[0]
    td = _tile(d, 512, LANES)
    nd = d // td
    return pl.pallas_call(
        _cmul_body,
        grid=(nd,),
        in_specs=[pl.BlockSpec((rows, td), lambda j: (0, j)),
                  pl.BlockSpec((rows, td), lambda j: (0, order * nd + j)),
                  pl.BlockSpec((1, td), lambda j: (0, order * nd + j))],
        out_specs=pl.BlockSpec((rows, td), lambda j: (0, j)),
        out_shape=jax.ShapeDtypeStruct((rows, d), BF16),
        compiler_params=_params("parallel"),
        name="spectrum_product",
    )(x, h, ss)


def _hyena_mixer(x, norm, w_in, layer, b_in, conv_w, conv_b, fparams, skip, plans):
    n_tok, d = x.shape
    u = _norm_matmul_conv(x, *norm, w_in, layer, b_in, conv_w, conv_b, name="hyena_in")
    filt, ss = _hyena_filters(n_tok, d, *fparams)

    if n_tok <= SINGLE_STAGE_MAX_LEN:
        plan = _dft_plan_single(n_tok)
        fwd, inv = jnp.asarray(plan["fwd"]), jnp.asarray(plan["inv"])
        part_cols = lambda p: (lambda tn: (lambda j: p * (d // tn) + j))
        hf = _matmul(fwd, filt, out_dtype=F32, tm=2 * plan["n"], tn=512, name="dft_filter")
        z_src, z_cols = u, part_cols(HYENA_ORDER)
        for order in range(HYENA_ORDER):
            x = _matmul(fwd, z_src, out_dtype=F32, tm=2 * plan["n"], tn=512, w_cols=(d, z_cols),
                        name="dft_signal")
            y = _cmul(x, hf, ss, d, order)
            last = order == HYENA_ORDER - 1
            z_src = _matmul(inv, y, out_dtype=BF16 if last else F32, epilogue=_ep_gate, tn=512,
                            extras=(_tile_extra(u, part_cols(order)), _tile_extra(z_src, z_cols),
                                    _row_extra(skip[order])), name="idft_gate")
            z_cols = part_cols(0)
        return z_src

    if n_tok not in plans:
        base = _dft_plan(n_tok)
        plans[n_tok] = dict(base, **{name + "_lanes": _lane_table(base[name])
                                     for name in ("tw1_cos", "tw1_sin", "tw2_cos", "tw2_sin")})
    plan = plans[n_tok]
    ah = _dft_stage1(plan, filt, HYENA_ORDER * d)
    z_src, z_part = u, HYENA_ORDER
    for order in range(HYENA_ORDER):
        a = _dft_stage1(plan, z_src, d, z_part)
        q = _spectral_conv(plan, a, ah, ss, order)
        last = order == HYENA_ORDER - 1
        z_src = _idft_gate(plan, q, u, order, z_src, z_part, skip[order], BF16 if last else F32)
        z_part = 0
    return z_src


def kernel(x, c, ctx, c_ctx, mod_w, mod_b, norm_mix_g, norm_mlp_g, attn_w_in, attn_w_out, attn_sink, attn_q_norm, attn_k_norm, hy_w_in, hy_b_in, hy_conv_w, hy_conv_b, hy_f_w1, hy_f_b1, hy_f_freq1, hy_f_w2, hy_f_b2, hy_f_freq2, hy_f_w3, hy_skip, hy_w_out, hy_b_out, mlp_w1, mlp_w2, final_g):
    batch, n_lat, d = x.shape
    n_ctx = ctx.shape[1]
    depth = mod_w.shape[0]
    assert batch == 1 and n_lat % GRID_W == 0 and n_lat % WINDOW == 0
    last_ctx_layer = 2 * ((depth - 1) // 2)
    rope = _rope_tables(n_lat)
    d_ff = mlp_w1.shape[2]
    attn_w_out_bf, hy_w_out_bf = attn_w_out.astype(BF16), hy_w_out.astype(BF16)
    dft_plans = {}

    cond = jnp.zeros((16, d), F32).at[0].set(c[0]).at[1].set(c_ctx)
    mod = _modulation(cond, mod_w, mod_b)

    xs = x[0]
    cs = ctx[0]
    for i in range(depth):
        j = i // 2
        is_attn = i % 2 == 0
        ctx_updated = i < last_ctx_layer
        sh1, sc1, g1, sh2, sc2, g2 = jnp.split(mod[i, 0:1], N_MOD, axis=-1)
        csh1, csc1, cg1, csh2, csc2, cg2 = jnp.split(mod[i, 1:2], N_MOD, axis=-1)
        if is_attn:
            qkv, h_lat = _attn_in(xs, norm_mix_g[i], sh1, sc1, attn_w_in, j, attn_q_norm[j], attn_k_norm[j], rope)
            qkv_c, h_ctx = _attn_in(cs, norm_mix_g[i], csh1, csc1, attn_w_in, j, attn_q_norm[j], attn_k_norm[j],
                                    None)
            v_col0 = Q_WIDTH + KV_WIDTH
            vt = _project_t(attn_w_in, j, v_col0, KV_WIDTH, h_lat)
            vt_c = _project_t(attn_w_in, j, v_col0, KV_WIDTH, h_ctx)
            o_a = _window_attention(qkv, vt, qkv_c, vt_c, attn_sink[j])
            o_b = _global_attention(qkv, vt, qkv_c, vt_c)
            xs = _attn_out(o_a, o_b, attn_w_out_bf, j, xs, g1)
            if ctx_updated:
                o_c = _context_attention(qkv_c, attn_sink[j])
                cs = _matmul(o_c, attn_w_out_bf, layer=j, out_dtype=F32, epilogue=_ep_resid,
                             extras=(_tile_extra(cs), _row_extra(cg1)), name="attn_out_ctx")
        else:
            fparams = (hy_f_w1[j], hy_f_b1[j], hy_f_freq1[j], hy_f_w2[j], hy_f_b2[j], hy_f_freq2[j], hy_f_w3[j])
            z = _hyena_mixer(xs, (norm_mix_g[i], sh1, sc1), hy_w_in, j, hy_b_in[j], hy_conv_w[j], hy_conv_b[j],
                             fparams, hy_skip[j], dft_plans)
            xs = _matmul(z, hy_w_out_bf, layer=j, out_dtype=F32, epilogue=_ep_resid_bias, tm=512, tn=2048,
                         extras=(_tile_extra(xs), _row_extra(g1), _row_extra(hy_b_out[j])), name="hyena_out")
            if ctx_updated:
                z_c = _hyena_mixer(cs, (norm_mix_g[i], csh1, csc1), hy_w_in, j, hy_b_in[j], hy_conv_w[j],
                                   hy_conv_b[j], fparams, hy_skip[j], dft_plans)
                cs = _matmul(z_c, hy_w_out_bf, layer=j, out_dtype=F32, epilogue=_ep_resid_bias,
                             extras=(_tile_extra(cs), _row_extra(cg1), _row_extra(hy_b_out[j])),
                             name="hyena_out_ctx")
        a1 = _norm_matmul(xs, norm_mlp_g[i], sh2, sc2, mlp_w1, layer=i, out_dtype=BF16, epilogue=_ep_relu2,
                          name="mlp_up")
        xs = _matmul(a1, mlp_w2, layer=i, out_dtype=F32, epilogue=_ep_resid, tm=BIG_TM, tn=256, tk=d_ff,
                     extras=(_tile_extra(xs), _row_extra(g2)), name="mlp_down")
        if ctx_updated:
            a1c = _norm_matmul(cs, norm_mlp_g[i], csh2, csc2, mlp_w1, layer=i, out_dtype=BF16,
                               epilogue=_ep_relu2, name="mlp_up_ctx")
            cs = _matmul(a1c, mlp_w2, layer=i, out_dtype=F32, epilogue=_ep_resid, tn=512, tk=d_ff,
                         extras=(_tile_extra(cs), _row_extra(cg2)), name="mlp_down_ctx")
    return _rms_norm(xs, final_g, out_dtype=F32)[None]
```

```python
import functools
import math

import numpy as np
import jax
import jax.numpy as jnp
from jax import lax
from jax.experimental import pallas as pl
from jax.experimental.pallas import tpu as pltpu

F32 = jnp.float32
BF16 = jnp.bfloat16

GRID_W = 64
HEAD_DIM = 128
A_Q_HEADS = 8
A_KV_HEADS = 2
B_Q_HEADS = 8
B_KV_HEADS = 2
GROUP = A_Q_HEADS // A_KV_HEADS
WINDOW = 128
ROPE_THETA = 10000.0
AXIS_DIM = HEAD_DIM // 2
Q_WIDTH = (A_Q_HEADS + B_Q_HEADS) * HEAD_DIM
KV_HEADS = A_KV_HEADS + B_KV_HEADS
KV_WIDTH = KV_HEADS * HEAD_DIM
HYENA_ORDER = 2
SHORT_CONV = 3
FILTER_EMB = 33
DECAY_TARGET = 1e-2
FAST_DECAY_PCT = 0.3
SLOW_DECAY_PCT = 1.5
MAX_DECAY = math.log(DECAY_TARGET) / FAST_DECAY_PCT
MIN_DECAY = math.log(DECAY_TARGET) / SLOW_DECAY_PCT
N_MOD = 6
EPS = 1e-6
LOG2_E = math.log2(math.e)
LN_2 = math.log(2.0)

VMEM_LIMIT_BYTES = 56 * 1024 * 1024
LANES = 128
SINGLE_STAGE_MAX_LEN = 512
BIG_TM = 1024


def _params(*sem):
    return pltpu.CompilerParams(dimension_semantics=sem, vmem_limit_bytes=VMEM_LIMIT_BYTES)


def _tile(n, pref, align):
    if n <= pref:
        return n
    t = (pref // align) * align
    while t >= align:
        if n % t == 0:
            return t
        t -= align
    return n


def _mod_body(a_ref, w_ref, b_ref, o_ref):
    a = a_ref[...]
    act = a * (1.0 / (1.0 + jnp.exp(-a)))
    o_ref[...] = jnp.dot(act.astype(BF16), w_ref[...].astype(BF16),
                         preferred_element_type=F32) + b_ref[...]


def _modulation(cond, mod_w, mod_b):
    depth, d, n = mod_w.shape
    r = cond.shape[0]
    tn = _tile(n, 1024, LANES)
    return pl.pallas_call(
        _mod_body,
        grid=(depth, n // tn),
        in_specs=[pl.BlockSpec((r, d), lambda l, j: (0, 0)),
                  pl.BlockSpec((None, d, tn), lambda l, j: (l, 0, j)),
                  pl.BlockSpec((None, 1, tn), lambda l, j: (l, 0, j))],
        out_specs=pl.BlockSpec((None, r, tn), lambda l, j: (l, 0, j)),
        out_shape=jax.ShapeDtypeStruct((depth, r, n), F32),
        compiler_params=_params("parallel", "parallel"),
        name="modulation",
    )(cond, mod_w, mod_b.reshape(depth, 1, n))


def _norm_body(x_ref, g_ref, *rest, modulated):
    x = x_ref[...]
    y = x * lax.rsqrt(jnp.mean(x * x, axis=-1, keepdims=True) + EPS) * g_ref[...]
    if modulated:
        sh_ref, sc_ref, o_ref = rest
        y = y * (1.0 + sc_ref[...]) + sh_ref[...]
    else:
        (o_ref,) = rest
    o_ref[...] = y.astype(o_ref.dtype)


def _rms_norm(x, g, shift=None, scale=None, out_dtype=None):
    t, d = x.shape
    out_dtype = BF16 if out_dtype is None else out_dtype
    tm = _tile(t, 512, 16)
    row = pl.BlockSpec((1, d), lambda i: (0, 0))
    vecs = [g.reshape(1, d)]
    if shift is not None:
        vecs += [shift.reshape(1, d), scale.reshape(1, d)]
    return pl.pallas_call(
        functools.partial(_norm_body, modulated=shift is not None),
        grid=(t // tm,),
        in_specs=[pl.BlockSpec((tm, d), lambda i: (i, 0))] + [row] * len(vecs),
        out_specs=pl.BlockSpec((tm, d), lambda i: (i, 0)),
        out_shape=jax.ShapeDtypeStruct((t, d), out_dtype),
        compiler_params=_params("parallel"),
        name="rms_norm",
    )(x, *vecs)


def _ep_none(acc):
    return acc


def _ep_relu2(acc):
    return jnp.square(jnp.maximum(acc, 0.0))


def _ep_resid(acc, x, g):
    return x + g * acc


def _ep_resid_bias(acc, x, g, b):
    return x + g * (acc + b)


def _ep_gate(acc, x, v, skip):
    return x * (acc + v * skip)


def _mm_body(*refs, nk, epilogue, n_extra):
    a_ref, w_ref = refs[0], refs[1]
    extra = refs[2:2 + n_extra]
    o_ref = refs[2 + n_extra]
    a = a_ref[...].astype(BF16)
    w = w_ref[...].astype(BF16)
    part = jnp.dot(a, w, preferred_element_type=F32)
    if nk == 1:
        o_ref[...] = epilogue(part, *[e[...] for e in extra]).astype(o_ref.dtype)
        return
    acc_ref = refs[3 + n_extra]
    k = pl.program_id(2)

    @pl.when(k == 0)
    def _():
        acc_ref[...] = part

    @pl.when(k > 0)
    def _():
        acc_ref[...] += part

    @pl.when(k == nk - 1)
    def _():
        o_ref[...] = epilogue(acc_ref[...], *[e[...] for e in extra]).astype(o_ref.dtype)


def _weight_spec(w, layer, tk, tn, index):
    if layer is None:
        return pl.BlockSpec((tk, tn), index)
    return pl.BlockSpec((None, tk, tn), lambda *ids: (layer,) + tuple(index(*ids)))


def _matmul(a, w, *, out_dtype, epilogue=_ep_none, extras=(), tm=512, tn=1024, tk=2048,
            w_cols=None, layer=None, name="matmul"):
    m, kdim = a.shape
    n = w.shape[-1] if w_cols is None else w_cols[0]
    tm = _tile(m, tm, 16)
    tn = _tile(n, tn, LANES)
    tk = _tile(kdim, tk, LANES)
    nk = kdim // tk
    wmap = (lambda j: j) if w_cols is None else w_cols[1](tn)
    in_specs = [pl.BlockSpec((tm, tk), lambda i, j, k: (i, k)),
                _weight_spec(w, layer, tk, tn, lambda i, j, k: (k, wmap(j)))]
    in_specs += [fn(tm, tn) for _, fn in extras]
    return pl.pallas_call(
        functools.partial(_mm_body, nk=nk, epilogue=epilogue, n_extra=len(extras)),
        grid=(m // tm, n // tn, nk),
        in_specs=in_specs,
        out_specs=pl.BlockSpec((tm, tn), lambda i, j, k: (i, j)),
        out_shape=jax.ShapeDtypeStruct((m, n), out_dtype),
        scratch_shapes=[pltpu.VMEM((tm, tn), F32)] if nk > 1 else [],
        compiler_params=_params("parallel", "parallel", "arbitrary"),
        name=name,
    )(a, w, *[arr for arr, _ in extras])


def _modulated_norm(x, g, shift, scale):
    gain = g * (1.0 + scale)
    return x * lax.rsqrt(jnp.mean(x * x, axis=-1, keepdims=True) + EPS) * gain + shift


def _nmm_body(x_ref, g_ref, sh_ref, sc_ref, w_ref, *rest, epilogue, n_extra):
    extra = rest[:n_extra]
    o_ref, h_ref = rest[n_extra], rest[n_extra + 1]

    @pl.when(pl.program_id(1) == 0)
    def _():
        h_ref[...] = _modulated_norm(x_ref[...], g_ref[...], sh_ref[...], sc_ref[...]).astype(h_ref.dtype)

    acc = jnp.dot(h_ref[...], w_ref[...].astype(BF16), preferred_element_type=F32)
    o_ref[...] = epilogue(acc, *[e[...] for e in extra]).astype(o_ref.dtype)


def _norm_matmul(x, g, shift, scale, w, *, out_dtype, epilogue=_ep_none, extras=(), tm=1024, tn=1024,
                 layer=None, name="norm_matmul"):
    m, d = x.shape
    n = w.shape[-1]
    tm = _tile(m, tm, 16)
    tn = _tile(n, tn, LANES)
    vec = pl.BlockSpec((1, d), lambda i, j: (0, 0))
    with_k = lambda spec_fn: (lambda bs: pl.BlockSpec(bs.block_shape, lambda i, j: bs.index_map(i, j, 0)))(
        spec_fn(tm, tn))
    return pl.pallas_call(
        functools.partial(_nmm_body, epilogue=epilogue, n_extra=len(extras)),
        grid=(m // tm, n // tn),
        in_specs=[pl.BlockSpec((tm, d), lambda i, j: (i, 0)), vec, vec, vec,
                  _weight_spec(w, layer, d, tn, lambda i, j: (0, j))] + [with_k(fn) for _, fn in extras],
        out_specs=pl.BlockSpec((tm, tn), lambda i, j: (i, j)),
        out_shape=jax.ShapeDtypeStruct((m, n), out_dtype),
        scratch_shapes=[pltpu.VMEM((tm, d), BF16)],
        compiler_params=_params("arbitrary", "arbitrary"),
        name=name,
    )(x, g.reshape(1, d), shift.reshape(1, d), scale.reshape(1, d), w, *[arr for arr, _ in extras])


HALO = 16


def _nmm_conv_body(x_ref, xp_ref, xn_ref, g_ref, sh_ref, sc_ref, w_ref, b_ref, cw_ref, cb_ref, o_ref, h_ref,
                   *, n_row_tiles):
    i = pl.program_id(0)
    tm = x_ref.shape[0]

    @pl.when(pl.program_id(1) == 0)
    def _():
        def norm(ref):
            return _modulated_norm(ref[...], g_ref[...], sh_ref[...], sc_ref[...]).astype(h_ref.dtype)

        h_ref[:HALO] = norm(xp_ref)
        h_ref[HALO:HALO + tm] = norm(x_ref)
        h_ref[HALO + tm:] = norm(xn_ref)

    u = jnp.dot(h_ref[...], w_ref[...].astype(BF16), preferred_element_type=F32) + b_ref[...]
    before = jnp.where(i > 0, u[:HALO], 0.0)
    after = jnp.where(i + 1 < n_row_tiles, u[HALO + tm:], 0.0)
    u = jnp.concatenate([before, u[HALO:HALO + tm], after], axis=0)
    rows = u.shape[0]
    y = (pltpu.roll(u, 1, 0) * cw_ref[0:1, :] + u * cw_ref[1:2, :]
         + pltpu.roll(u, rows - 1, 0) * cw_ref[2:3, :] + cb_ref[...])
    o_ref[...] = y[HALO:HALO + tm]


def _norm_matmul_conv(x, g, shift, scale, w, layer, b, conv_w, conv_b, *, tm=1024, tn=512, name):
    m, d = x.shape
    n = w.shape[-1]
    tm = _tile(m, tm, HALO)
    tn = _tile(n, tn, LANES)
    hb = tm // HALO
    n_row_tiles = m // tm
    vec = pl.BlockSpec((1, d), lambda i, j: (0, 0))
    col = lambda rows: pl.BlockSpec((rows, tn), lambda i, j: (0, j))
    return pl.pallas_call(
        functools.partial(_nmm_conv_body, n_row_tiles=n_row_tiles),
        grid=(n_row_tiles, n // tn),
        in_specs=[pl.BlockSpec((tm, d), lambda i, j: (i, 0)),
                  pl.BlockSpec((HALO, d), lambda i, j: (jnp.maximum(i * hb - 1, 0), 0)),
                  pl.BlockSpec((HALO, d), lambda i, j: (jnp.minimum((i + 1) * hb, m // HALO - 1), 0)),
                  vec, vec, vec, _weight_spec(w, layer, d, tn, lambda i, j: (0, j)),
                  col(1), col(SHORT_CONV), col(1)],
        out_specs=pl.BlockSpec((tm, tn), lambda i, j: (i, j)),
        out_shape=jax.ShapeDtypeStruct((m, n), F32),
        scratch_shapes=[pltpu.VMEM((tm + 2 * HALO, d), BF16)],
        compiler_params=_params("arbitrary", "arbitrary"),
        name=name,
    )(x, x, x, g.reshape(1, d), shift.reshape(1, d), scale.reshape(1, d), w, b.reshape(1, n), conv_w,
      conv_b.reshape(1, n))


def _row_extra(vec, period_blocks=None):
    vec = vec.reshape(1, -1)
    if period_blocks is None:
        return vec, lambda tm, tn: pl.BlockSpec((1, tn), lambda i, j, k: (0, j))
    return vec, lambda tm, tn: pl.BlockSpec((1, tn), lambda i, j, k: (0, j % period_blocks(tn)))


def _tile_extra(arr, colmap=None):
    cm = colmap if colmap is not None else (lambda tn: (lambda j: j))
    return arr, lambda tm, tn: pl.BlockSpec((tm, tn), lambda i, j, k: (i, cm(tn)(j)))


def _rope_tables(n_tok):
    rows = n_tok // GRID_W
    row = jnp.repeat(jnp.arange(rows, dtype=F32), GRID_W)
    col = jnp.tile(jnp.arange(GRID_W, dtype=F32), rows)
    inv = ROPE_THETA ** (-jnp.arange(0, AXIS_DIM, 2, dtype=F32) / AXIS_DIM)
    ang_r = row[:, None] * inv[None]
    ang_c = col[:, None] * inv[None]
    cos = jnp.concatenate([jnp.cos(ang_r)] * 2 + [jnp.cos(ang_c)] * 2, axis=-1)
    sin = jnp.concatenate([-jnp.sin(ang_r), jnp.sin(ang_r), -jnp.sin(ang_c), jnp.sin(ang_c)], axis=-1)
    return cos, sin


HEADS_PER_TILE = 8
ATTN_IN_WIDTH = Q_WIDTH + 2 * KV_WIDTH


def _attn_in_body(*refs, rope):
    if rope:
        x_ref, g_ref, sh_ref, sc_ref, w_ref, qn_ref, kn_ref, cos_ref, sin_ref, o_ref, h_ref = refs
        cos, sin = cos_ref[...], sin_ref[...]
        lane = lax.broadcasted_iota(jnp.int32, cos.shape, 1)
        first = (lane % (AXIS_DIM)) < (AXIS_DIM // 2)
    else:
        x_ref, g_ref, sh_ref, sc_ref, w_ref, qn_ref, kn_ref, o_ref, h_ref = refs
    j = pl.program_id(1)
    scale = HEAD_DIM ** -0.5
    quarter = AXIS_DIM // 2

    @pl.when(j == 0)
    def _():
        h_ref[...] = _modulated_norm(x_ref[...], g_ref[...], sh_ref[...], sc_ref[...]).astype(h_ref.dtype)

    acc = jnp.dot(h_ref[...], w_ref[...].astype(BF16), preferred_element_type=F32)

    def rot(x):
        if not rope:
            return x
        swapped = jnp.where(first, pltpu.roll(x, HEAD_DIM - quarter, 1), pltpu.roll(x, quarter, 1))
        return x * cos + swapped * sin

    def nrm(x, g_ref_):
        return x * lax.rsqrt(jnp.mean(x * x, axis=-1, keepdims=True) + EPS) * g_ref_[...]

    q_a = lambda x: rot(x) * scale
    q_b = lambda x: rot(nrm(x, qn_ref)) * (scale * LOG2_E)
    k_b = lambda x: rot(nrm(x, kn_ref))
    kinds = ([q_a] * A_Q_HEADS + [q_b] * B_Q_HEADS + [rot] * A_KV_HEADS + [k_b] * B_KV_HEADS
             + [lambda x: x] * KV_HEADS)

    def emit(tile_kinds):
        for hh, kind in enumerate(tile_kinds):
            sl = slice(hh * HEAD_DIM, (hh + 1) * HEAD_DIM)
            o_ref[:, sl] = kind(acc[:, sl]).astype(o_ref.dtype)

    for t in range(len(kinds) // HEADS_PER_TILE):
        pl.when(j == t)(functools.partial(emit, kinds[t * HEADS_PER_TILE:(t + 1) * HEADS_PER_TILE]))


def _attn_in(x, g, shift, scale, w, layer, q_norm, k_norm, rope):
    t, d = x.shape
    tm = _tile(t, 512, 16)
    tn = HEADS_PER_TILE * HEAD_DIM
    vec = lambda n: pl.BlockSpec((1, n), lambda i, j: (0, 0))
    in_specs = [pl.BlockSpec((tm, d), lambda i, j: (i, 0)), vec(d), vec(d), vec(d),
                _weight_spec(w, layer, d, tn, lambda i, j: (0, j)), vec(HEAD_DIM), vec(HEAD_DIM)]
    args = [x, g.reshape(1, d), shift.reshape(1, d), scale.reshape(1, d), w,
            q_norm.reshape(1, HEAD_DIM), k_norm.reshape(1, HEAD_DIM)]
    if rope is not None:
        in_specs += [pl.BlockSpec((tm, HEAD_DIM), lambda i, j: (i, 0))] * 2
        args += list(rope)
    return pl.pallas_call(
        functools.partial(_attn_in_body, rope=rope is not None),
        grid=(t // tm, ATTN_IN_WIDTH // tn),
        in_specs=in_specs,
        out_specs=[pl.BlockSpec((tm, tn), lambda i, j: (i, j)), pl.BlockSpec((tm, d), lambda i, j: (i, 0))],
        out_shape=[jax.ShapeDtypeStruct((t, ATTN_IN_WIDTH), BF16), jax.ShapeDtypeStruct((t, d), BF16)],
        compiler_params=_params("arbitrary", "arbitrary"),
        name="attn_in",
    )(*args)


def _stack_heads(q_ref):
    return jnp.concatenate([q_ref[:, g * HEAD_DIM:(g + 1) * HEAD_DIM] for g in range(GROUP)], axis=0)


def _unstack_heads(o_ref, out, t):
    for g in range(GROUP):
        o_ref[:, g * HEAD_DIM:(g + 1) * HEAD_DIM] = out[g * t:(g + 1) * t].astype(o_ref.dtype)


def _sink_column(sink_ref, first, t):
    head = lax.broadcasted_iota(jnp.int32, (GROUP * t, 1), 0) // t
    col = jnp.full((GROUP * t, 1), sink_ref[first], F32)
    for g in range(1, GROUP):
        col = jnp.where(head == g, sink_ref[first + g], col)
    return col


def _qk(q, k):
    return lax.dot_general(q, k, (((1,), (1,)), ((), ())), preferred_element_type=F32)


ONES_ROWS = 16


def _gattn_body(q_ref, kc_ref, vtc_ref, k_ref, vt_ref, o_ref, m_ref, acc_ref, sa_ref, sb_ref,
                *, tq, ck, n_chunks):
    q = _stack_heads(q_ref)
    m_ref[...] = jnp.full(m_ref.shape, -jnp.inf, F32)
    acc_ref[...] = jnp.zeros(acc_ref.shape, F32)

    def scores(c):
        return _qk(k_ref[pl.ds(pl.multiple_of(c * ck, ck), ck), :], q)

    def consume(st, vtb):
        m_old = m_ref[...]
        m_new = jnp.maximum(m_old, jnp.max(st, axis=0, keepdims=True))
        p = jnp.exp2(st - m_new).astype(BF16)
        lhs = jnp.concatenate([vtb, jnp.ones((ONES_ROWS, vtb.shape[1]), BF16)], axis=0)
        acc_ref[...] = jnp.exp2(m_old - m_new) * acc_ref[...] + jnp.dot(lhs, p, preferred_element_type=F32)
        m_ref[...] = m_new

    def values(c):
        return vt_ref[:, pl.ds(pl.multiple_of(c * ck, ck), ck)]

    sa_ref[...] = scores(0)
    consume(_qk(kc_ref[...], q), vtc_ref[...])

    def pair(c2, carry):
        c = 2 * c2
        sb_ref[...] = scores(c + 1)
        consume(sa_ref[...], values(c))
        sa_ref[...] = scores(c + 2)
        consume(sb_ref[...], values(c + 1))
        return carry

    lax.fori_loop(0, n_chunks // 2 - 1, pair, 0)
    last = n_chunks - 2
    sb_ref[...] = scores(last + 1)
    consume(sa_ref[...], values(last))
    consume(sb_ref[...], values(last + 1))
    _unstack_heads(o_ref, (acc_ref[:HEAD_DIM] / acc_ref[HEAD_DIM:HEAD_DIM + 1]).T, tq)


def _attn_out_body(oa_ref, ob_ref, w_ref, x_ref, g_ref, o_ref):
    ka = oa_ref.shape[1]
    w = w_ref[...].astype(BF16)
    acc = (jnp.dot(oa_ref[...], w[:ka], preferred_element_type=F32)
           + jnp.dot(ob_ref[...], w[ka:], preferred_element_type=F32))
    o_ref[...] = x_ref[...] + g_ref[...] * acc


def _attn_out(o_a, o_b, w, layer, x, gate):
    m, ka = o_a.shape
    kb = o_b.shape[1]
    n = w.shape[-1]
    tm = _tile(m, 512, 16)
    tn = _tile(n, 2048, LANES)
    return pl.pallas_call(
        _attn_out_body,
        grid=(m // tm, n // tn),
        in_specs=[pl.BlockSpec((tm, ka), lambda i, j: (i, 0)), pl.BlockSpec((tm, kb), lambda i, j: (i, 0)),
                  _weight_spec(w, layer, ka + kb, tn, lambda i, j: (0, j)),
                  pl.BlockSpec((tm, tn), lambda i, j: (i, j)), pl.BlockSpec((1, tn), lambda i, j: (0, j))],
        out_specs=pl.BlockSpec((tm, tn), lambda i, j: (i, j)),
        out_shape=jax.ShapeDtypeStruct((m, n), F32),
        compiler_params=_params("parallel", "parallel"),
        name="attn_out",
    )(o_a, o_b, w, x, gate.reshape(1, n))


K_COL0 = Q_WIDTH // HEAD_DIM


def _global_attention(qkv, vt, qkv_ctx, vt_ctx):
    s, n_ctx = qkv.shape[0], qkv_ctx.shape[0]
    tq = _tile(s, 1024, LANES)
    ck = _tile(s // 2, 512, LANES)
    n_chunks = s // ck
    assert n_chunks % 2 == 0
    rows = GROUP * tq
    gw = GROUP * HEAD_DIM
    kspec = lambda n: pl.BlockSpec((n, HEAD_DIM), lambda h, i: (0, K_COL0 + A_KV_HEADS + h))
    vspec = lambda n: pl.BlockSpec((HEAD_DIM, n), lambda h, i: (A_KV_HEADS + h, 0))
    return pl.pallas_call(
        functools.partial(_gattn_body, tq=tq, ck=ck, n_chunks=n_chunks),
        grid=(B_KV_HEADS, s // tq),
        in_specs=[pl.BlockSpec((tq, gw), lambda h, i: (i, A_KV_HEADS + h)),
                  kspec(n_ctx), vspec(n_ctx), kspec(s), vspec(s)],
        out_specs=pl.BlockSpec((tq, gw), lambda h, i: (i, h)),
        out_shape=jax.ShapeDtypeStruct((s, B_Q_HEADS * HEAD_DIM), BF16),
        scratch_shapes=[pltpu.VMEM((1, rows), F32), pltpu.VMEM((HEAD_DIM + ONES_ROWS, rows), F32),
                        pltpu.VMEM((ck, rows), F32), pltpu.VMEM((ck, rows), F32)],
        compiler_params=_params("parallel", "parallel"),
        name="global_attention",
    )(qkv, qkv_ctx, vt_ctx, qkv, vt)


def _project_t_body(w_ref, h_ref, o_ref, wt_ref):
    @pl.when(pl.program_id(0) == 0)
    def _():
        wt_ref[...] = w_ref[...].T.astype(wt_ref.dtype)

    o_ref[...] = _qk(wt_ref[...], h_ref[...]).astype(o_ref.dtype)


def _project_t(w, layer, col0, c, h):
    d = w.shape[1]
    t = h.shape[0]
    tt = _tile(t, 1024, LANES)
    return pl.pallas_call(
        _project_t_body,
        grid=(t // tt,),
        in_specs=[pl.BlockSpec((None, d, c), lambda i: (layer, 0, col0 // c)),
                  pl.BlockSpec((tt, d), lambda i: (i, 0))],
        out_specs=pl.BlockSpec((c, tt), lambda i: (0, i)),
        out_shape=jax.ShapeDtypeStruct((c, t), BF16),
        scratch_shapes=[pltpu.VMEM((c, d), BF16)],
        compiler_params=_params("arbitrary"),
        name="project_t",
    )(w, h)


WINDOW_BLOCKS = 2


def _wattn_body(sink_ref, q_ref, kc_ref, vtc_ref, kp_ref, k0_ref, kn_ref, vtp_ref, vt0_ref, vtn_ref, o_ref,
                *, n_tiles, n_ctx):
    h = pl.program_id(0)
    i = pl.program_id(1)
    w = WINDOW
    tq = q_ref.shape[0]
    q = _stack_heads(q_ref)
    kcat = jnp.concatenate([kc_ref[...], kp_ref[...], k0_ref[...], kn_ref[...]], axis=0)
    vtcat = jnp.concatenate([vtc_ref[...], vtp_ref[...], vt0_ref[...], vtn_ref[...]], axis=1)
    st = _qk(kcat, q)
    c = lax.broadcasted_iota(jnp.int32, st.shape, 0) - n_ctx
    r = lax.broadcasted_iota(jnp.int32, st.shape, 1) % tq
    lo = jnp.maximum(r, jnp.where(i >= 1, 0, w))
    hi = jnp.minimum(r + 2 * w, jnp.where(i + 1 < n_tiles, tq + 2 * w - 1, tq + w - 1))
    valid = (c < 0) | ((c >= lo) & (c <= hi))
    st = jnp.where(valid, st, -jnp.inf)
    head = lax.broadcasted_iota(jnp.int32, (1, GROUP * tq), 1) // tq
    sink = jnp.full((1, GROUP * tq), sink_ref[h * GROUP], F32)
    for g in range(1, GROUP):
        sink = jnp.where(head == g, sink_ref[h * GROUP + g], sink)
    m = jnp.maximum(jnp.max(st, axis=0, keepdims=True), sink)
    p = jnp.exp(st - m).astype(BF16)
    lhs = jnp.concatenate([vtcat, jnp.ones((ONES_ROWS, vtcat.shape[1]), BF16)], axis=0)
    acc = jnp.dot(lhs, p, preferred_element_type=F32)
    denom = acc[HEAD_DIM:HEAD_DIM + 1] + jnp.exp(sink - m)
    _unstack_heads(o_ref, (acc[:HEAD_DIM] / denom).T, tq)


def _window_attention(qkv, vt, qkv_ctx, vt_ctx, sink):
    s, n_ctx = qkv.shape[0], qkv_ctx.shape[0]
    w = WINDOW
    nb = s // w
    tb = WINDOW_BLOCKS if nb % WINDOW_BLOCKS == 0 else 1
    tq = tb * w
    n_tiles = nb // tb
    gw = GROUP * HEAD_DIM
    before = lambda i: jnp.maximum(i * tb - 1, 0)
    after = lambda i: jnp.minimum((i + 1) * tb, nb - 1)
    kblk = lambda pos: pl.BlockSpec((w, HEAD_DIM), lambda h, i: (pos(i), K_COL0 + h))
    vblk = lambda pos: pl.BlockSpec((HEAD_DIM, w), lambda h, i: (h, pos(i)))
    return pl.pallas_call(
        functools.partial(_wattn_body, n_tiles=n_tiles, n_ctx=n_ctx),
        grid=(A_KV_HEADS, n_tiles),
        in_specs=[pl.BlockSpec(memory_space=pltpu.SMEM),
                  pl.BlockSpec((tq, gw), lambda h, i: (i, h)),
                  pl.BlockSpec((n_ctx, HEAD_DIM), lambda h, i: (0, K_COL0 + h)),
                  pl.BlockSpec((HEAD_DIM, n_ctx), lambda h, i: (h, 0)),
                  kblk(before), pl.BlockSpec((tq, HEAD_DIM), lambda h, i: (i, K_COL0 + h)), kblk(after),
                  vblk(before), pl.BlockSpec((HEAD_DIM, tq), lambda h, i: (h, i)), vblk(after)],
        out_specs=pl.BlockSpec((tq, gw), lambda h, i: (i, h)),
        out_shape=jax.ShapeDtypeStruct((s, A_Q_HEADS * HEAD_DIM), BF16),
        compiler_params=_params("parallel", "parallel"),
        name="window_attention",
    )(sink, qkv, qkv_ctx, vt_ctx, qkv, qkv, qkv, vt, vt, vt)


def _cattn_body(sink_ref, q_ref, k_ref, v_ref, o_ref, *, n_ctx):
    h = pl.program_id(0)
    q = _stack_heads(q_ref)
    s = _qk(q, k_ref[...]) * jnp.where(h >= A_KV_HEADS, LN_2, 1.0)
    sink = _sink_column(sink_ref, h * GROUP, n_ctx)
    m = jnp.maximum(jnp.max(s, axis=-1, keepdims=True), sink)
    p = jnp.exp(s - m)
    denom = jnp.sum(p, axis=-1, keepdims=True) + jnp.exp(sink - m)
    out = jnp.dot(p.astype(BF16), v_ref[...], preferred_element_type=F32) / denom
    _unstack_heads(o_ref, out, n_ctx)


def _context_attention(qkv, sink):
    n_ctx = qkv.shape[0]
    gw = GROUP * HEAD_DIM
    sink_all = jnp.concatenate([sink.astype(F32), jnp.full((B_Q_HEADS,), -jnp.inf, F32)])
    kv = lambda col0: pl.BlockSpec((n_ctx, HEAD_DIM), lambda h: (0, col0 + h))
    return pl.pallas_call(
        functools.partial(_cattn_body, n_ctx=n_ctx),
        grid=(KV_HEADS,),
        in_specs=[pl.BlockSpec(memory_space=pltpu.SMEM),
                  pl.BlockSpec((n_ctx, gw), lambda h: (0, h)), kv(K_COL0), kv(K_COL0 + KV_HEADS)],
        out_specs=pl.BlockSpec((n_ctx, gw), lambda h: (0, h)),
        out_shape=jax.ShapeDtypeStruct((n_ctx, Q_WIDTH), BF16),
        compiler_params=_params("parallel"),
        name="context_attention",
    )(sink_all, qkv, qkv, qkv)


def _filter_body(feat_ref, w1_ref, b1_ref, f1_ref, w2_ref, b2_ref, f2_ref, w3_ref, dl_ref,
                 h_ref, ss_ref, *, n_tok):
    i = pl.program_id(0)
    tl = feat_ref.shape[0]
    mm = lambda a, b: jnp.dot(a.astype(BF16), b.astype(BF16), preferred_element_type=F32)
    hid = jnp.sin(f1_ref[...] * (mm(feat_ref[...], w1_ref[...]) + b1_ref[...]))
    hid = jnp.sin(f2_ref[...] * (mm(hid, w2_ref[...]) + b2_ref[...]))
    h = mm(hid, w3_ref[...])
    t = i * tl + lax.broadcasted_iota(jnp.int32, (tl, 1), 0)
    offs = jnp.abs(t - n_tok // 2).astype(F32) * (2.0 / n_tok)
    h = h * jnp.exp(-offs * dl_ref[...])
    h_ref[...] = h.astype(h_ref.dtype)

    @pl.when(i == 0)
    def _():
        ss_ref[...] = jnp.zeros(ss_ref.shape, F32)

    ss_ref[...] += jnp.sum(h * h, axis=0, keepdims=True)


def _hyena_filters(n_tok, d, w1, b1, fr1, w2, b2, fr2, w3):
    bands = (FILTER_EMB - 1) // 2
    t = jnp.linspace(0.0, 1.0, n_tok, dtype=F32)[:, None]
    wv = 2.0 * math.pi * jnp.arange(n_tok, dtype=F32)[:, None] / n_tok
    f = jnp.linspace(1e-4, bands - 1, bands, dtype=F32)[None]
    feats = jnp.concatenate([t, jnp.cos(f * wv), -jnp.sin(f * wv)], axis=-1)
    deltas = jnp.abs(jnp.linspace(MIN_DECAY, MAX_DECAY, d, dtype=F32))
    hidden = w1.shape[1]
    pad_e, pad_h = LANES - FILTER_EMB, LANES - hidden
    padv = lambda v: jnp.pad(v.reshape(1, hidden), ((0, 0), (0, pad_h)))
    c = HYENA_ORDER * d
    tl = _tile(n_tok, 256, 8)
    full = lambda shape: pl.BlockSpec(shape, lambda i: (0, 0))
    return pl.pallas_call(
        functools.partial(_filter_body, n_tok=n_tok),
        grid=(n_tok // tl,),
        in_specs=[pl.BlockSpec((tl, LANES), lambda i: (i, 0)),
                  full((LANES, LANES)), full((1, LANES)), full((1, LANES)),
                  full((LANES, LANES)), full((1, LANES)), full((1, LANES)),
                  full((LANES, c)), full((1, c))],
        out_specs=[pl.BlockSpec((tl, c), lambda i: (i, 0)), full((1, c))],
        out_shape=[jax.ShapeDtypeStruct((n_tok, c), BF16), jax.ShapeDtypeStruct((1, c), F32)],
        compiler_params=_params("arbitrary"),
        name="hyena_filter",
    )(jnp.pad(feats, ((0, 0), (0, pad_e))),
      jnp.pad(w1, ((0, pad_e), (0, pad_h))), padv(b1), padv(fr1),
      jnp.pad(w2, ((0, pad_h), (0, pad_h))), padv(b2), padv(fr2),
      jnp.pad(w3, ((0, pad_h), (0, 0))), jnp.tile(deltas, HYENA_ORDER).reshape(1, c))


def _lane_table(vals):
    return jnp.broadcast_to(jnp.asarray(vals)[..., None], vals.shape + (LANES,))


SUBLANES = 8
DFT_ROW_GROUP = 16


@functools.lru_cache(maxsize=None)
def _dft_plan(n_tok):
    n = 2 * n_tok
    log = n.bit_length() - 1
    assert 1 << log == n, "sequence length must be a power of two"
    n2 = 1 << (log // 2)
    n1 = n // n2
    assert n1 % 4 == 0 and n2 % DFT_ROW_GROUP == 0
    eye = np.eye(SUBLANES)
    nk1 = n1 // 2 + 1
    nk1_pad = -(-nk1 // SUBLANES) * SUBLANES
    a1 = 2.0 * np.pi * np.outer(np.arange(nk1), np.arange(n1 // 2)) / n1
    f1 = np.kron(np.concatenate([np.cos(a1), -np.sin(a1)], axis=0), eye)
    a2 = 2.0 * np.pi * np.outer(np.arange(n2), np.arange(n2)) / n2
    c2, s2 = np.cos(a2), np.sin(a2)
    g2 = np.block([[c2, s2], [-s2, c2]])
    g2i = np.block([[c2, -s2], [s2, c2]])
    rows = np.arange(n1 // 4, n1 // 4 + n1 // 2)
    a1i = 2.0 * np.pi * np.outer(rows, np.arange(nk1_pad)) / n1
    weight = np.where((np.arange(nk1_pad) == 0) | (np.arange(nk1_pad) == n1 // 2), 1.0, 2.0)
    weight = np.where(np.arange(nk1_pad) < nk1, weight, 0.0) / n
    f1i = np.kron(np.concatenate([np.cos(a1i) * weight, -np.sin(a1i) * weight], axis=1), eye)
    tw = 2.0 * np.pi * np.outer(np.arange(n2), np.arange(nk1)) / n
    tw1 = tw.reshape(n2 // SUBLANES, SUBLANES, nk1).transpose(0, 2, 1).reshape(n2 // SUBLANES, nk1 * SUBLANES)
    f32 = lambda m: m.astype(np.float32)
    return dict(n1=n1, n2=n2, nk1=nk1, nk1_pad=nk1_pad, f1=f32(f1), g2=f32(g2), g2i=f32(g2i), f1i=f32(f1i),
                tw1_cos=f32(np.cos(tw1)), tw1_sin=f32(np.sin(tw1)),
                tw2_cos=f32(np.cos(tw.T)), tw2_sin=f32(np.sin(tw.T)))


def _first_step():
    return (pl.program_id(0) == 0) & (pl.program_id(1) == 0)


def _stage1_body(f_ref, z_ref, c_ref, s_ref, o_ref, fb_ref):
    n1h, g, tc = z_ref.shape
    nk1 = o_ref.shape[1]
    half = nk1 * SUBLANES
    reps = tc // LANES

    @pl.when(_first_step())
    def _():
        fb_ref[...] = f_ref[...].astype(BF16)

    z = z_ref[...].astype(F32)
    re_parts, im_parts = [], []
    for a in range(g // SUBLANES):
        zz = z[:, a * SUBLANES:(a + 1) * SUBLANES, :].reshape(n1h * SUBLANES, tc)
        acc = jnp.dot(fb_ref[...], zz.astype(BF16), preferred_element_type=F32)
        re, im = acc[:half], acc[half:]
        c = jnp.tile(c_ref[a], (1, reps))
        s = jnp.tile(s_ref[a], (1, reps))
        re_parts.append((re * c + im * s).reshape(nk1, SUBLANES, tc))
        im_parts.append((im * c - re * s).reshape(nk1, SUBLANES, tc))
    o_ref[0] = jnp.concatenate(re_parts, axis=1).astype(o_ref.dtype)
    o_ref[1] = jnp.concatenate(im_parts, axis=1).astype(o_ref.dtype)


def _dft_stage1(plan, src, width, part=0):
    n1, n2, nk1 = plan["n1"], plan["n2"], plan["nk1"]
    g = DFT_ROW_GROUP
    tc = _tile(width, 1024, LANES)
    nc = width // tc
    sub = g // SUBLANES
    tw = lambda t: t.reshape(n2 // g, sub, nk1 * SUBLANES, LANES)
    tw_spec = pl.BlockSpec((None, sub, nk1 * SUBLANES, LANES), lambda i, j: (i, 0, 0, 0))
    f = jnp.asarray(plan["f1"])
    return pl.pallas_call(
        _stage1_body,
        grid=(n2 // g, nc),
        in_specs=[pl.BlockSpec(f.shape, lambda i, j: (0, 0)),
                  pl.BlockSpec((n1 // 2, g, tc), lambda i, j: (0, i, part * nc + j)),
                  tw_spec, tw_spec],
        out_specs=pl.BlockSpec((2, nk1, g, tc), lambda i, j: (0, 0, i, j)),
        out_shape=jax.ShapeDtypeStruct((2, nk1, n2, width), BF16),
        scratch_shapes=[pltpu.VMEM(f.shape, BF16)],
        compiler_params=_params("arbitrary", "arbitrary"),
        name="dft_stage1",
    )(f, src.reshape(n1 // 2, n2, src.shape[1]), tw(plan["tw1_cos_lanes"]), tw(plan["tw1_sin_lanes"]))


def _istage1_body(f_ref, q_ref, x_ref, v_ref, skip_ref, o_ref, fb_ref):
    _, nk1, g, tc = q_ref.shape
    n1h = o_ref.shape[0]
    nk1_pad = f_ref.shape[1] // (2 * SUBLANES)

    @pl.when(_first_step())
    def _():
        fb_ref[...] = f_ref[...].astype(BF16)

    q = q_ref[...].astype(F32)
    q = jnp.concatenate([q, jnp.zeros((2, nk1_pad - nk1, g, tc), F32)], axis=1)
    parts = []
    for a in range(g // SUBLANES):
        qq = q[:, :, a * SUBLANES:(a + 1) * SUBLANES, :].reshape(2 * nk1_pad * SUBLANES, tc)
        y = jnp.dot(fb_ref[...], qq.astype(BF16), preferred_element_type=F32)
        parts.append(y.reshape(n1h, SUBLANES, tc))
    y = jnp.concatenate(parts, axis=1)
    o_ref[...] = (x_ref[...] * (y + v_ref[...] * skip_ref[...])).astype(o_ref.dtype)


def _idft_gate(plan, q, x_src, x_part, v_src, v_part, skip, out_dtype):
    n1, n2, nk1 = plan["n1"], plan["n2"], plan["nk1"]
    d = q.shape[3]
    g = DFT_ROW_GROUP
    tc = _tile(d, 512, LANES)
    nc = d // tc
    f = jnp.asarray(plan["f1i"])
    tok = lambda part: pl.BlockSpec((n1 // 2, g, tc), lambda i, j: (0, i, part * nc + j))
    view = lambda arr: arr.reshape(n1 // 2, n2, arr.shape[1])
    out = pl.pallas_call(
        _istage1_body,
        grid=(n2 // g, nc),
        in_specs=[pl.BlockSpec(f.shape, lambda i, j: (0, 0)),
                  pl.BlockSpec((2, nk1, g, tc), lambda i, j: (0, 0, i, j)),
                  tok(x_part), tok(v_part), pl.BlockSpec((1, tc), lambda i, j: (0, j))],
        out_specs=tok(0),
        out_shape=jax.ShapeDtypeStruct((n1 // 2, n2, d), out_dtype),
        scratch_shapes=[pltpu.VMEM(f.shape, BF16)],
        compiler_params=_params("arbitrary", "arbitrary"),
        name="idft_gate",
    )(f, q, view(x_src), view(v_src), skip.reshape(1, d))
    return out.reshape(n1 // 2 * n2, d)


SPECTRAL_ROWS_PER_STEP = 5


def _spectral_body(a_ref, ah_ref, ss_ref, g2_ref, g2i_ref, c_ref, s_ref, o_ref):
    _, kb, n2, td = a_ref.shape
    g2 = g2_ref[...].astype(BF16)
    g2i = g2i_ref[...].astype(BF16)
    scale = lax.rsqrt(ss_ref[...] + EPS)
    reps = td // LANES
    for b in range(kb):
        stack = lambda ref: jnp.concatenate([ref[0, b], ref[1, b]], axis=0)
        x = jnp.dot(g2, stack(a_ref), preferred_element_type=F32)
        hf = jnp.dot(g2, stack(ah_ref), preferred_element_type=F32)
        xr, xi, hr, hi = x[:n2], x[n2:], hf[:n2], hf[n2:]
        y = jnp.concatenate([(xr * hr - xi * hi) * scale, (xr * hi + xi * hr) * scale], axis=0)
        p = jnp.dot(g2i, y.astype(BF16), preferred_element_type=F32)
        pr, pi = p[:n2], p[n2:]
        c = jnp.tile(c_ref[b], (1, reps))
        s = jnp.tile(s_ref[b], (1, reps))
        o_ref[0, b] = (pr * c - pi * s).astype(o_ref.dtype)
        o_ref[1, b] = (pr * s + pi * c).astype(o_ref.dtype)


def _spectral_conv(plan, a, ah, ss, order):
    n1, n2, d = a.shape[1:]
    td = _tile(d, 1024, LANES)
    nd = d // td
    kb = max(b for b in range(1, SPECTRAL_ROWS_PER_STEP + 1) if n1 % b == 0)
    blk = lambda off: pl.BlockSpec((2, kb, n2, td), lambda k1, j: (0, k1, 0, off + j))
    const = pl.BlockSpec((2 * n2, 2 * n2), lambda k1, j: (0, 0))
    tw = pl.BlockSpec((kb, n2, LANES), lambda k1, j: (k1, 0, 0))
    return pl.pallas_call(
        _spectral_body,
        grid=(n1 // kb, nd),
        in_specs=[blk(0), blk(order * nd), pl.BlockSpec((1, td), lambda k1, j: (0, order * nd + j)),
                  const, const, tw, tw],
        out_specs=blk(0),
        out_shape=jax.ShapeDtypeStruct((2, n1, n2, d), BF16),
        compiler_params=_params("parallel", "parallel"),
        name="spectral_conv",
    )(a, ah, ss, jnp.asarray(plan["g2"]), jnp.asarray(plan["g2i"]),
      plan["tw2_cos_lanes"], plan["tw2_sin_lanes"])


@functools.lru_cache(maxsize=None)
def _dft_plan_single(n_tok):
    n = 2 * n_tok
    af = 2.0 * np.pi * np.outer(np.arange(n), np.arange(n_tok)) / n
    fwd = np.concatenate([np.cos(af), -np.sin(af)], axis=0)
    ai = 2.0 * np.pi * np.outer(np.arange(n_tok // 2, n_tok // 2 + n_tok), np.arange(n)) / n
    inv = np.concatenate([np.cos(ai), -np.sin(ai)], axis=1) / n
    return dict(n=n, fwd=fwd.astype(np.float32), inv=inv.astype(np.float32))


def _cmul_body(x_ref, h_ref, ss_ref, o_ref):
    n = x_ref.shape[0] // 2
    xr, xi, hr, hi = x_ref[:n], x_ref[n:], h_ref[:n], h_ref[n:]
    scale = lax.rsqrt(ss_ref[...] + EPS)
    o_ref[:n] = ((xr * hr - xi * hi) * scale).astype(o_ref.dtype)
    o_ref[n:] = ((xr * hi + xi * hr) * scale).astype(o_ref.dtype)


def _cmul(x, h, ss, d, order):
    rows = x.shape[0]
    td = _tile(d, 512, LANES)
    nd = d // td
    return pl.pallas_call(
        _cmul_body,
        grid=(nd,),
        in_specs=[pl.BlockSpec((rows, td), lambda j: (0, j)),
                  pl.BlockSpec((rows, td), lambda j: (0, order * nd + j)),
                  pl.BlockSpec((1, td), lambda j: (0, order * nd + j))],
        out_specs=pl.BlockSpec((rows, td), lambda j: (0, j)),
        out_shape=jax.ShapeDtypeStruct((rows, d), BF16),
        compiler_params=_params("parallel"),
        name="spectrum_product",
    )(x, h, ss)


def _hyena_mixer(x, norm, w_in, layer, b_in, conv_w, conv_b, fparams, skip, plans):
    n_tok, d = x.shape
    u = _norm_matmul_conv(x, *norm, w_in, layer, b_in, conv_w, conv_b, name="hyena_in")
    filt, ss = _hyena_filters(n_tok, d, *fparams)

    if n_tok <= SINGLE_STAGE_MAX_LEN:
        plan = _dft_plan_single(n_tok)
        fwd, inv = jnp.asarray(plan["fwd"]), jnp.asarray(plan["inv"])
        part_cols = lambda p: (lambda tn: (lambda j: p * (d // tn) + j))
        hf = _matmul(fwd, filt, out_dtype=F32, tm=2 * plan["n"], tn=512, name="dft_filter")
        z_src, z_cols = u, part_cols(HYENA_ORDER)
        for order in range(HYENA_ORDER):
            x = _matmul(fwd, z_src, out_dtype=F32, tm=2 * plan["n"], tn=512, w_cols=(d, z_cols),
                        name="dft_signal")
            y = _cmul(x, hf, ss, d, order)
            last = order == HYENA_ORDER - 1
            z_src = _matmul(inv, y, out_dtype=BF16 if last else F32, epilogue=_ep_gate, tn=512,
                            extras=(_tile_extra(u, part_cols(order)), _tile_extra(z_src, z_cols),
                                    _row_extra(skip[order])), name="idft_gate")
            z_cols = part_cols(0)
        return z_src

    if n_tok not in plans:
        base = _dft_plan(n_tok)
        plans[n_tok] = dict(base, **{name + "_lanes": _lane_table(base[name])
                                     for name in ("tw1_cos", "tw1_sin", "tw2_cos", "tw2_sin")})
    plan = plans[n_tok]
    ah = _dft_stage1(plan, filt, HYENA_ORDER * d)
    z_src, z_part = u, HYENA_ORDER
    for order in range(HYENA_ORDER):
        a = _dft_stage1(plan, z_src, d, z_part)
        q = _spectral_conv(plan, a, ah, ss, order)
        last = order == HYENA_ORDER - 1
        z_src = _idft_gate(plan, q, u, order, z_src, z_part, skip[order], BF16 if last else F32)
        z_part = 0
    return z_src


def kernel(x, c, ctx, c_ctx, mod_w, mod_b, norm_mix_g, norm_mlp_g, attn_w_in, attn_w_out, attn_sink, attn_q_norm, attn_k_norm, hy_w_in, hy_b_in, hy_conv_w, hy_conv_b, hy_f_w1, hy_f_b1, hy_f_freq1, hy_f_w2, hy_f_b2, hy_f_freq2, hy_f_w3, hy_skip, hy_w_out, hy_b_out, mlp_w1, mlp_w2, final_g):
    batch, n_lat, d = x.shape
    n_ctx = ctx.shape[1]
    depth = mod_w.shape[0]
    assert batch == 1 and n_lat % GRID_W == 0 and n_lat % WINDOW == 0
    last_ctx_layer = 2 * ((depth - 1) // 2)
    rope = _rope_tables(n_lat)
    d_ff = mlp_w1.shape[2]
    attn_w_out_bf, hy_w_out_bf = attn_w_out.astype(BF16), hy_w_out.astype(BF16)
    dft_plans = {}

    cond = jnp.zeros((16, d), F32).at[0].set(c[0]).at[1].set(c_ctx)
    mod = _modulation(cond, mod_w, mod_b)

    xs = x[0]
    cs = ctx[0]
    for i in range(depth):
        j = i // 2
        is_attn = i % 2 == 0
        ctx_updated = i < last_ctx_layer
        sh1, sc1, g1, sh2, sc2, g2 = jnp.split(mod[i, 0:1], N_MOD, axis=-1)
        csh1, csc1, cg1, csh2, csc2, cg2 = jnp.split(mod[i, 1:2], N_MOD, axis=-1)
        if is_attn:
            qkv, h_lat = _attn_in(xs, norm_mix_g[i], sh1, sc1, attn_w_in, j, attn_q_norm[j], attn_k_norm[j], rope)
            qkv_c, h_ctx = _attn_in(cs, norm_mix_g[i], csh1, csc1, attn_w_in, j, attn_q_norm[j], attn_k_norm[j],
                                    None)
            v_col0 = Q_WIDTH + KV_WIDTH
            vt = _project_t(attn_w_in, j, v_col0, KV_WIDTH, h_lat)
            vt_c = _project_t(attn_w_in, j, v_col0, KV_WIDTH, h_ctx)
            o_a = _window_attention(qkv, vt, qkv_c, vt_c, attn_sink[j])
            o_b = _global_attention(qkv, vt, qkv_c, vt_c)
            xs = _attn_out(o_a, o_b, attn_w_out_bf, j, xs, g1)
            if ctx_updated:
                o_c = _context_attention(qkv_c, attn_sink[j])
                cs = _matmul(o_c, attn_w_out_bf, layer=j, out_dtype=F32, epilogue=_ep_resid,
                             extras=(_tile_extra(cs), _row_extra(cg1)), name="attn_out_ctx")
        else:
            fparams = (hy_f_w1[j], hy_f_b1[j], hy_f_freq1[j], hy_f_w2[j], hy_f_b2[j], hy_f_freq2[j], hy_f_w3[j])
            z = _hyena_mixer(xs, (norm_mix_g[i], sh1, sc1), hy_w_in, j, hy_b_in[j], hy_conv_w[j], hy_conv_b[j],
                             fparams, hy_skip[j], dft_plans)
            xs = _matmul(z, hy_w_out_bf, layer=j, out_dtype=F32, epilogue=_ep_resid_bias, tm=512, tn=2048,
                         extras=(_tile_extra(xs), _row_extra(g1), _row_extra(hy_b_out[j])), name="hyena_out")
            if ctx_updated:
                z_c = _hyena_mixer(cs, (norm_mix_g[i], csh1, csc1), hy_w_in, j, hy_b_in[j], hy_conv_w[j],
                                   hy_conv_b[j], fparams, hy_skip[j], dft_plans)
                cs = _matmul(z_c, hy_w_out_bf, layer=j, out_dtype=F32, epilogue=_ep_resid_bias,
                             extras=(_tile_extra(cs), _row_extra(cg1), _row_extra(hy_b_out[j])),
                             name="hyena_out_ctx")
        a1 = _norm_matmul(xs, norm_mlp_g[i], sh2, sc2, mlp_w1, layer=i, out_dtype=BF16, epilogue=_ep_relu2,
                          name="mlp_up")
        xs = _matmul(a1, mlp_w2, layer=i, out_dtype=F32, epilogue=_ep_resid, tm=BIG_TM, tn=256, tk=d_ff,
                     extras=(_tile_extra(xs), _row_extra(g2)), name="mlp_down")
        if ctx_updated:
            a1c = _norm_matmul(cs, norm_mlp_g[i], csh2, csc2, mlp_w1, layer=i, out_dtype=BF16,
                               epilogue=_ep_relu2, name="mlp_up_ctx")
            cs = _matmul(a1c, mlp_w2, layer=i, out_dtype=F32, epilogue=_ep_resid, tn=512, tk=d_ff,
                         extras=(_tile_extra(cs), _row_extra(cg2)), name="mlp_down_ctx")
    return _rms_norm(xs, final_g, out_dtype=F32)[None]
```

```python
import functools
import math

import numpy as np
import jax
import jax.numpy as jnp
from jax import lax
from jax.experimental import pallas as pl
from jax.experimental.pallas import tpu as pltpu

F32 = jnp.float32
BF16 = jnp.bfloat16

GRID_W = 64
HEAD_DIM = 128
A_Q_HEADS = 8
A_KV_HEADS = 2
B_Q_HEADS = 8
B_KV_HEADS = 2
GROUP = A_Q_HEADS // A_KV_HEADS
WINDOW = 128
ROPE_THETA = 10000.0
AXIS_DIM = HEAD_DIM // 2
Q_WIDTH = (A_Q_HEADS + B_Q_HEADS) * HEAD_DIM
KV_HEADS = A_KV_HEADS + B_KV_HEADS
KV_WIDTH = KV_HEADS * HEAD_DIM
HYENA_ORDER = 2
SHORT_CONV = 3
FILTER_EMB = 33
DECAY_TARGET = 1e-2
FAST_DECAY_PCT = 0.3
SLOW_DECAY_PCT = 1.5
MAX_DECAY = math.log(DECAY_TARGET) / FAST_DECAY_PCT
MIN_DECAY = math.log(DECAY_TARGET) / SLOW_DECAY_PCT
N_MOD = 6
EPS = 1e-6
LOG2_E = math.log2(math.e)
LN_2 = math.log(2.0)

VMEM_LIMIT_BYTES = 56 * 1024 * 1024
LANES = 128
SINGLE_STAGE_MAX_LEN = 512
BIG_TM = 1024


def _params(*sem):
    return pltpu.CompilerParams(dimension_semantics=sem, vmem_limit_bytes=VMEM_LIMIT_BYTES)


def _tile(n, pref, align):
    if n <= pref:
        return n
    t = (pref // align) * align
    while t >= align:
        if n % t == 0:
            return t
        t -= align
    return n


def _mod_body(a_ref, w_ref, b_ref, o_ref):
    a = a_ref[...]
    act = a * (1.0 / (1.0 + jnp.exp(-a)))
    o_ref[...] = jnp.dot(act.astype(BF16), w_ref[...].astype(BF16),
                         preferred_element_type=F32) + b_ref[...]


def _modulation(cond, mod_w, mod_b):
    depth, d, n = mod_w.shape
    r = cond.shape[0]
    tn = _tile(n, 1024, LANES)
    return pl.pallas_call(
        _mod_body,
        grid=(depth, n // tn),
        in_specs=[pl.BlockSpec((r, d), lambda l, j: (0, 0)),
                  pl.BlockSpec((None, d, tn), lambda l, j: (l, 0, j)),
                  pl.BlockSpec((None, 1, tn), lambda l, j: (l, 0, j))],
        out_specs=pl.BlockSpec((None, r, tn), lambda l, j: (l, 0, j)),
        out_shape=jax.ShapeDtypeStruct((depth, r, n), F32),
        compiler_params=_params("parallel", "parallel"),
        name="modulation",
    )(cond, mod_w, mod_b.reshape(depth, 1, n))


def _norm_body(x_ref, g_ref, *rest, modulated):
    x = x_ref[...]
    y = x * lax.rsqrt(jnp.mean(x * x, axis=-1, keepdims=True) + EPS) * g_ref[...]
    if modulated:
        sh_ref, sc_ref, o_ref = rest
        y = y * (1.0 + sc_ref[...]) + sh_ref[...]
    else:
        (o_ref,) = rest
    o_ref[...] = y.astype(o_ref.dtype)


def _rms_norm(x, g, shift=None, scale=None, out_dtype=None):
    t, d = x.shape
    out_dtype = BF16 if out_dtype is None else out_dtype
    tm = _tile(t, 512, 16)
    row = pl.BlockSpec((1, d), lambda i: (0, 0))
    vecs = [g.reshape(1, d)]
    if shift is not None:
        vecs += [shift.reshape(1, d), scale.reshape(1, d)]
    return pl.pallas_call(
        functools.partial(_norm_body, modulated=shift is not None),
        grid=(t // tm,),
        in_specs=[pl.BlockSpec((tm, d), lambda i: (i, 0))] + [row] * len(vecs),
        out_specs=pl.BlockSpec((tm, d), lambda i: (i, 0)),
        out_shape=jax.ShapeDtypeStruct((t, d), out_dtype),
        compiler_params=_params("parallel"),
        name="rms_norm",
    )(x, *vecs)


def _ep_none(acc):
    return acc


def _ep_relu2(acc):
    return jnp.square(jnp.maximum(acc, 0.0))


def _ep_resid(acc, x, g):
    return x + g * acc


def _ep_resid_bias(acc, x, g, b):
    return x + g * (acc + b)


def _ep_gate(acc, x, v, skip):
    return x * (acc + v * skip)


def _mm_body(*refs, nk, epilogue, n_extra):
    a_ref, w_ref = refs[0], refs[1]
    extra = refs[2:2 + n_extra]
    o_ref = refs[2 + n_extra]
    a = a_ref[...].astype(BF16)
    w = w_ref[...].astype(BF16)
    part = jnp.dot(a, w, preferred_element_type=F32)
    if nk == 1:
        o_ref[...] = epilogue(part, *[e[...] for e in extra]).astype(o_ref.dtype)
        return
    acc_ref = refs[3 + n_extra]
    k = pl.program_id(2)

    @pl.when(k == 0)
    def _():
        acc_ref[...] = part

    @pl.when(k > 0)
    def _():
        acc_ref[...] += part

    @pl.when(k == nk - 1)
    def _():
        o_ref[...] = epilogue(acc_ref[...], *[e[...] for e in extra]).astype(o_ref.dtype)


def _weight_spec(w, layer, tk, tn, index):
    if layer is None:
        return pl.BlockSpec((tk, tn), index)
    return pl.BlockSpec((None, tk, tn), lambda *ids: (layer,) + tuple(index(*ids)))


def _matmul(a, w, *, out_dtype, epilogue=_ep_none, extras=(), tm=512, tn=1024, tk=2048,
            w_cols=None, layer=None, name="matmul"):
    m, kdim = a.shape
    n = w.shape[-1] if w_cols is None else w_cols[0]
    tm = _tile(m, tm, 16)
    tn = _tile(n, tn, LANES)
    tk = _tile(kdim, tk, LANES)
    nk = kdim // tk
    wmap = (lambda j: j) if w_cols is None else w_cols[1](tn)
    in_specs = [pl.BlockSpec((tm, tk), lambda i, j, k: (i, k)),
                _weight_spec(w, layer, tk, tn, lambda i, j, k: (k, wmap(j)))]
    in_specs += [fn(tm, tn) for _, fn in extras]
    return pl.pallas_call(
        functools.partial(_mm_body, nk=nk, epilogue=epilogue, n_extra=len(extras)),
        grid=(m // tm, n // tn, nk),
        in_specs=in_specs,
        out_specs=pl.BlockSpec((tm, tn), lambda i, j, k: (i, j)),
        out_shape=jax.ShapeDtypeStruct((m, n), out_dtype),
        scratch_shapes=[pltpu.VMEM((tm, tn), F32)] if nk > 1 else [],
        compiler_params=_params("parallel", "parallel", "arbitrary"),
        name=name,
    )(a, w, *[arr for arr, _ in extras])


def _modulated_norm(x, g, shift, scale):
    gain = g * (1.0 + scale)
    return x * lax.rsqrt(jnp.mean(x * x, axis=-1, keepdims=True) + EPS) * gain + shift


def _nmm_body(x_ref, g_ref, sh_ref, sc_ref, w_ref, *rest, epilogue, n_extra):
    extra = rest[:n_extra]
    o_ref, h_ref = rest[n_extra], rest[n_extra + 1]

    @pl.when(pl.program_id(1) == 0)
    def _():
        h_ref[...] = _modulated_norm(x_ref[...], g_ref[...], sh_ref[...], sc_ref[...]).astype(h_ref.dtype)

    acc = jnp.dot(h_ref[...], w_ref[...].astype(BF16), preferred_element_type=F32)
    o_ref[...] = epilogue(acc, *[e[...] for e in extra]).astype(o_ref.dtype)


def _norm_matmul(x, g, shift, scale, w, *, out_dtype, epilogue=_ep_none, extras=(), tm=1024, tn=1024,
                 layer=None, name="norm_matmul"):
    m, d = x.shape
    n = w.shape[-1]
    tm = _tile(m, tm, 16)
    tn = _tile(n, tn, LANES)
    vec = pl.BlockSpec((1, d), lambda i, j: (0, 0))
    with_k = lambda spec_fn: (lambda bs: pl.BlockSpec(bs.block_shape, lambda i, j: bs.index_map(i, j, 0)))(
        spec_fn(tm, tn))
    return pl.pallas_call(
        functools.partial(_nmm_body, epilogue=epilogue, n_extra=len(extras)),
        grid=(m // tm, n // tn),
        in_specs=[pl.BlockSpec((tm, d), lambda i, j: (i, 0)), vec, vec, vec,
                  _weight_spec(w, layer, d, tn, lambda i, j: (0, j))] + [with_k(fn) for _, fn in extras],
        out_specs=pl.BlockSpec((tm, tn), lambda i, j: (i, j)),
        out_shape=jax.ShapeDtypeStruct((m, n), out_dtype),
        scratch_shapes=[pltpu.VMEM((tm, d), BF16)],
        compiler_params=_params("arbitrary", "arbitrary"),
        name=name,
    )(x, g.reshape(1, d), shift.reshape(1, d), scale.reshape(1, d), w, *[arr for arr, _ in extras])


HALO = 16


def _nmm_conv_body(x_ref, xp_ref, xn_ref, g_ref, sh_ref, sc_ref, w_ref, b_ref, cw_ref, cb_ref, o_ref, h_ref,
                   *, n_row_tiles):
    i = pl.program_id(0)
    tm = x_ref.shape[0]

    @pl.when(pl.program_id(1) == 0)
    def _():
        def norm(ref):
            return _modulated_norm(ref[...], g_ref[...], sh_ref[...], sc_ref[...]).astype(h_ref.dtype)

        h_ref[:HALO] = norm(xp_ref)
        h_ref[HALO:HALO + tm] = norm(x_ref)
        h_ref[HALO + tm:] = norm(xn_ref)

    u = jnp.dot(h_ref[...], w_ref[...].astype(BF16), preferred_element_type=F32) + b_ref[...]
    before = jnp.where(i > 0, u[:HALO], 0.0)
    after = jnp.where(i + 1 < n_row_tiles, u[HALO + tm:], 0.0)
    u = jnp.concatenate([before, u[HALO:HALO + tm], after], axis=0)
    rows = u.shape[0]
    y = (pltpu.roll(u, 1, 0) * cw_ref[0:1, :] + u * cw_ref[1:2, :]
         + pltpu.roll(u, rows - 1, 0) * cw_ref[2:3, :] + cb_ref[...])
    o_ref[...] = y[HALO:HALO + tm]


def _norm_matmul_conv(x, g, shift, scale, w, layer, b, conv_w, conv_b, *, tm=1024, tn=512, name):
    m, d = x.shape
    n = w.shape[-1]
    tm = _tile(m, tm, HALO)
    tn = _tile(n, tn, LANES)
    hb = tm // HALO
    n_row_tiles = m // tm
    vec = pl.BlockSpec((1, d), lambda i, j: (0, 0))
    col = lambda rows: pl.BlockSpec((rows, tn), lambda i, j: (0, j))
    return pl.pallas_call(
        functools.partial(_nmm_conv_body, n_row_tiles=n_row_tiles),
        grid=(n_row_tiles, n // tn),
        in_specs=[pl.BlockSpec((tm, d), lambda i, j: (i, 0)),
                  pl.BlockSpec((HALO, d), lambda i, j: (jnp.maximum(i * hb - 1, 0), 0)),
                  pl.BlockSpec((HALO, d), lambda i, j: (jnp.minimum((i + 1) * hb, m // HALO - 1), 0)),
                  vec, vec, vec, _weight_spec(w, layer, d, tn, lambda i, j: (0, j)),
                  col(1), col(SHORT_CONV), col(1)],
        out_specs=pl.BlockSpec((tm, tn), lambda i, j: (i, j)),
        out_shape=jax.ShapeDtypeStruct((m, n), F32),
        scratch_shapes=[pltpu.VMEM((tm + 2 * HALO, d), BF16)],
        compiler_params=_params("arbitrary", "arbitrary"),
        name=name,
    )(x, x, x, g.reshape(1, d), shift.reshape(1, d), scale.reshape(1, d), w, b.reshape(1, n), conv_w,
      conv_b.reshape(1, n))


def _row_extra(vec, period_blocks=None):
    vec = vec.reshape(1, -1)
    if period_blocks is None:
        return vec, lambda tm, tn: pl.BlockSpec((1, tn), lambda i, j, k: (0, j))
    return vec, lambda tm, tn: pl.BlockSpec((1, tn), lambda i, j, k: (0, j % period_blocks(tn)))


def _tile_extra(arr, colmap=None):
    cm = colmap if colmap is not None else (lambda tn: (lambda j: j))
    return arr, lambda tm, tn: pl.BlockSpec((tm, tn), lambda i, j, k: (i, cm(tn)(j)))


def _rope_tables(n_tok):
    rows = n_tok // GRID_W
    row = jnp.repeat(jnp.arange(rows, dtype=F32), GRID_W)
    col = jnp.tile(jnp.arange(GRID_W, dtype=F32), rows)
    inv = ROPE_THETA ** (-jnp.arange(0, AXIS_DIM, 2, dtype=F32) / AXIS_DIM)
    ang_r = row[:, None] * inv[None]
    ang_c = col[:, None] * inv[None]
    cos = jnp.concatenate([jnp.cos(ang_r)] * 2 + [jnp.cos(ang_c)] * 2, axis=-1)
    sin = jnp.concatenate([-jnp.sin(ang_r), jnp.sin(ang_r), -jnp.sin(ang_c), jnp.sin(ang_c)], axis=-1)
    return cos, sin


HEADS_PER_TILE = 8
ATTN_IN_WIDTH = Q_WIDTH + 2 * KV_WIDTH


def _attn_in_body(*refs, rope):
    if rope:
        x_ref, g_ref, sh_ref, sc_ref, w_ref, qn_ref, kn_ref, cos_ref, sin_ref, o_ref, h_ref = refs
        cos, sin = cos_ref[...], sin_ref[...]
        lane = lax.broadcasted_iota(jnp.int32, cos.shape, 1)
        first = (lane % (AXIS_DIM)) < (AXIS_DIM // 2)
    else:
        x_ref, g_ref, sh_ref, sc_ref, w_ref, qn_ref, kn_ref, o_ref, h_ref = refs
    j = pl.program_id(1)
    scale = HEAD_DIM ** -0.5
    quarter = AXIS_DIM // 2

    @pl.when(j == 0)
    def _():
        h_ref[...] = _modulated_norm(x_ref[...], g_ref[...], sh_ref[...], sc_ref[...]).astype(h_ref.dtype)

    acc = jnp.dot(h_ref[...], w_ref[...].astype(BF16), preferred_element_type=F32)

    def rot(x):
        if not rope:
            return x
        swapped = jnp.where(first, pltpu.roll(x, HEAD_DIM - quarter, 1), pltpu.roll(x, quarter, 1))
        return x * cos + swapped * sin

    def nrm(x, g_ref_):
        return x * lax.rsqrt(jnp.mean(x * x, axis=-1, keepdims=True) + EPS) * g_ref_[...]

    q_a = lambda x: rot(x) * scale
    q_b = lambda x: rot(nrm(x, qn_ref)) * (scale * LOG2_E)
    k_b = lambda x: rot(nrm(x, kn_ref))
    kinds = ([q_a] * A_Q_HEADS + [q_b] * B_Q_HEADS + [rot] * A_KV_HEADS + [k_b] * B_KV_HEADS
             + [lambda x: x] * KV_HEADS)

    def emit(tile_kinds):
        for hh, kind in enumerate(tile_kinds):
            sl = slice(hh * HEAD_DIM, (hh + 1) * HEAD_DIM)
            o_ref[:, sl] = kind(acc[:, sl]).astype(o_ref.dtype)

    for t in range(len(kinds) // HEADS_PER_TILE):
        pl.when(j == t)(functools.partial(emit, kinds[t * HEADS_PER_TILE:(t + 1) * HEADS_PER_TILE]))


def _attn_in(x, g, shift, scale, w, layer, q_norm, k_norm, rope):
    t, d = x.shape
    tm = _tile(t, 512, 16)
    tn = HEADS_PER_TILE * HEAD_DIM
    vec = lambda n: pl.BlockSpec((1, n), lambda i, j: (0, 0))
    in_specs = [pl.BlockSpec((tm, d), lambda i, j: (i, 0)), vec(d), vec(d), vec(d),
                _weight_spec(w, layer, d, tn, lambda i, j: (0, j)), vec(HEAD_DIM), vec(HEAD_DIM)]
    args = [x, g.reshape(1, d), shift.reshape(1, d), scale.reshape(1, d), w,
            q_norm.reshape(1, HEAD_DIM), k_norm.reshape(1, HEAD_DIM)]
    if rope is not None:
        in_specs += [pl.BlockSpec((tm, HEAD_DIM), lambda i, j: (i, 0))] * 2
        args += list(rope)
    return pl.pallas_call(
        functools.partial(_attn_in_body, rope=rope is not None),
        grid=(t // tm, ATTN_IN_WIDTH // tn),
        in_specs=in_specs,
        out_specs=[pl.BlockSpec((tm, tn), lambda i, j: (i, j)), pl.BlockSpec((tm, d), lambda i, j: (i, 0))],
        out_shape=[jax.ShapeDtypeStruct((t, ATTN_IN_WIDTH), BF16), jax.ShapeDtypeStruct((t, d), BF16)],
        compiler_params=_params("arbitrary", "arbitrary"),
        name="attn_in",
    )(*args)


def _stack_heads(q_ref):
    return jnp.concatenate([q_ref[:, g * HEAD_DIM:(g + 1) * HEAD_DIM] for g in range(GROUP)], axis=0)


def _unstack_heads(o_ref, out, t):
    for g in range(GROUP):
        o_ref[:, g * HEAD_DIM:(g + 1) * HEAD_DIM] = out[g * t:(g + 1) * t].astype(o_ref.dtype)


def _sink_column(sink_ref, first, t):
    head = lax.broadcasted_iota(jnp.int32, (GROUP * t, 1), 0) // t
    col = jnp.full((GROUP * t, 1), sink_ref[first], F32)
    for g in range(1, GROUP):
        col = jnp.where(head == g, sink_ref[first + g], col)
    return col


def _qk(q, k):
    return lax.dot_general(q, k, (((1,), (1,)), ((), ())), preferred_element_type=F32)


ONES_ROWS = 16


def _gattn_body(q_ref, kc_ref, vtc_ref, k_ref, vt_ref, o_ref, m_ref, acc_ref, sa_ref, sb_ref,
                *, tq, ck, n_chunks):
    q = _stack_heads(q_ref)
    m_ref[...] = jnp.full(m_ref.shape, -jnp.inf, F32)
    acc_ref[...] = jnp.zeros(acc_ref.shape, F32)

    def scores(c):
        return _qk(k_ref[pl.ds(pl.multiple_of(c * ck, ck), ck), :], q)

    def consume(st, vtb):
        m_old = m_ref[...]
        m_new = jnp.maximum(m_old, jnp.max(st, axis=0, keepdims=True))
        p = jnp.exp2(st - m_new).astype(BF16)
        lhs = jnp.concatenate([vtb, jnp.ones((ONES_ROWS, vtb.shape[1]), BF16)], axis=0)
        acc_ref[...] = jnp.exp2(m_old - m_new) * acc_ref[...] + jnp.dot(lhs, p, preferred_element_type=F32)
        m_ref[...] = m_new

    def values(c):
        return vt_ref[:, pl.ds(pl.multiple_of(c * ck, ck), ck)]

    sa_ref[...] = scores(0)
    consume(_qk(kc_ref[...], q), vtc_ref[...])

    def pair(c2, carry):
        c = 2 * c2
        sb_ref[...] = scores(c + 1)
        consume(sa_ref[...], values(c))
        sa_ref[...] = scores(c + 2)
        consume(sb_ref[...], values(c + 1))
        return carry

    lax.fori_loop(0, n_chunks // 2 - 1, pair, 0)
    last = n_chunks - 2
    sb_ref[...] = scores(last + 1)
    consume(sa_ref[...], values(last))
    consume(sb_ref[...], values(last + 1))
    _unstack_heads(o_ref, (acc_ref[:HEAD_DIM] / acc_ref[HEAD_DIM:HEAD_DIM + 1]).T, tq)


def _attn_out_body(oa_ref, ob_ref, w_ref, x_ref, g_ref, o_ref):
    ka = oa_ref.shape[1]
    w = w_ref[...].astype(BF16)
    acc = (jnp.dot(oa_ref[...], w[:ka], preferred_element_type=F32)
           + jnp.dot(ob_ref[...], w[ka:], preferred_element_type=F32))
    o_ref[...] = x_ref[...] + g_ref[...] * acc


def _attn_out(o_a, o_b, w, layer, x, gate):
    m, ka = o_a.shape
    kb = o_b.shape[1]
    n = w.shape[-1]
    tm = _tile(m, 512, 16)
    tn = _tile(n, 2048, LANES)
    return pl.pallas_call(
        _attn_out_body,
        grid=(m // tm, n // tn),
        in_specs=[pl.BlockSpec((tm, ka), lambda i, j: (i, 0)), pl.BlockSpec((tm, kb), lambda i, j: (i, 0)),
                  _weight_spec(w, layer, ka + kb, tn, lambda i, j: (0, j)),
                  pl.BlockSpec((tm, tn), lambda i, j: (i, j)), pl.BlockSpec((1, tn), lambda i, j: (0, j))],
        out_specs=pl.BlockSpec((tm, tn), lambda i, j: (i, j)),
        out_shape=jax.ShapeDtypeStruct((m, n), F32),
        compiler_params=_params("parallel", "parallel"),
        name="attn_out",
    )(o_a, o_b, w, x, gate.reshape(1, n))


K_COL0 = Q_WIDTH // HEAD_DIM


def _global_attention(qkv, vt, qkv_ctx, vt_ctx):
    s, n_ctx = qkv.shape[0], qkv_ctx.shape[0]
    tq = _tile(s, 1024, LANES)
    ck = _tile(s // 2, 512, LANES)
    n_chunks = s // ck
    assert n_chunks % 2 == 0
    rows = GROUP * tq
    gw = GROUP * HEAD_DIM
    kspec = lambda n: pl.BlockSpec((n, HEAD_DIM), lambda h, i: (0, K_COL0 + A_KV_HEADS + h))
    vspec = lambda n: pl.BlockSpec((HEAD_DIM, n), lambda h, i: (A_KV_HEADS + h, 0))
    return pl.pallas_call(
        functools.partial(_gattn_body, tq=tq, ck=ck, n_chunks=n_chunks),
        grid=(B_KV_HEADS, s // tq),
        in_specs=[pl.BlockSpec((tq, gw), lambda h, i: (i, A_KV_HEADS + h)),
                  kspec(n_ctx), vspec(n_ctx), kspec(s), vspec(s)],
        out_specs=pl.BlockSpec((tq, gw), lambda h, i: (i, h)),
        out_shape=jax.ShapeDtypeStruct((s, B_Q_HEADS * HEAD_DIM), BF16),
        scratch_shapes=[pltpu.VMEM((1, rows), F32), pltpu.VMEM((HEAD_DIM + ONES_ROWS, rows), F32),
                        pltpu.VMEM((ck, rows), F32), pltpu.VMEM((ck, rows), F32)],
        compiler_params=_params("parallel", "parallel"),
        name="global_attention",
    )(qkv, qkv_ctx, vt_ctx, qkv, vt)


def _project_t_body(w_ref, h_ref, o_ref, wt_ref):
    @pl.when(pl.program_id(0) == 0)
    def _():
        wt_ref[...] = w_ref[...].T.astype(wt_ref.dtype)

    o_ref[...] = _qk(wt_ref[...], h_ref[...]).astype(o_ref.dtype)


def _project_t(w, layer, col0, c, h):
    d = w.shape[1]
    t = h.shape[0]
    tt = _tile(t, 1024, LANES)
    return pl.pallas_call(
        _project_t_body,
        grid=(t // tt,),
        in_specs=[pl.BlockSpec((None, d, c), lambda i: (layer, 0, col0 // c)),
                  pl.BlockSpec((tt, d), lambda i: (i, 0))],
        out_specs=pl.BlockSpec((c, tt), lambda i: (0, i)),
        out_shape=jax.ShapeDtypeStruct((c, t), BF16),
        scratch_shapes=[pltpu.VMEM((c, d), BF16)],
        compiler_params=_params("arbitrary"),
        name="project_t",
    )(w, h)


WINDOW_BLOCKS = 2


def _wattn_body(sink_ref, q_ref, kc_ref, vtc_ref, kp_ref, k0_ref, kn_ref, vtp_ref, vt0_ref, vtn_ref, o_ref,
                *, n_tiles, n_ctx):
    h = pl.program_id(0)
    i = pl.program_id(1)
    w = WINDOW
    tq = q_ref.shape[0]
    q = _stack_heads(q_ref)
    kcat = jnp.concatenate([kc_ref[...], kp_ref[...], k0_ref[...], kn_ref[...]], axis=0)
    vtcat = jnp.concatenate([vtc_ref[...], vtp_ref[...], vt0_ref[...], vtn_ref[...]], axis=1)
    st = _qk(kcat, q)
    c = lax.broadcasted_iota(jnp.int32, st.shape, 0) - n_ctx
    r = lax.broadcasted_iota(jnp.int32, st.shape, 1) % tq
    lo = jnp.maximum(r, jnp.where(i >= 1, 0, w))
    hi = jnp.minimum(r + 2 * w, jnp.where(i + 1 < n_tiles, tq + 2 * w - 1, tq + w - 1))
    valid = (c < 0) | ((c >= lo) & (c <= hi))
    st = jnp.where(valid, st, -jnp.inf)
    head = lax.broadcasted_iota(jnp.int32, (1, GROUP * tq), 1) // tq
    sink = jnp.full((1, GROUP * tq), sink_ref[h * GROUP], F32)
    for g in range(1, GROUP):
        sink = jnp.where(head == g, sink_ref[h * GROUP + g], sink)
    m = jnp.maximum(jnp.max(st, axis=0, keepdims=True), sink)
    p = jnp.exp(st - m).astype(BF16)
    lhs = jnp.concatenate([vtcat, jnp.ones((ONES_ROWS, vtcat.shape[1]), BF16)], axis=0)
    acc = jnp.dot(lhs, p, preferred_element_type=F32)
    denom = acc[HEAD_DIM:HEAD_DIM + 1] + jnp.exp(sink - m)
    _unstack_heads(o_ref, (acc[:HEAD_DIM] / denom).T, tq)


def _window_attention(qkv, vt, qkv_ctx, vt_ctx, sink):
    s, n_ctx = qkv.shape[0], qkv_ctx.shape[0]
    w = WINDOW
    nb = s // w
    tb = WINDOW_BLOCKS if nb % WINDOW_BLOCKS == 0 else 1
    tq = tb * w
    n_tiles = nb // tb
    gw = GROUP * HEAD_DIM
    before = lambda i: jnp.maximum(i * tb - 1, 0)
    after = lambda i: jnp.minimum((i + 1) * tb, nb - 1)
    kblk = lambda pos: pl.BlockSpec((w, HEAD_DIM), lambda h, i: (pos(i), K_COL0 + h))
    vblk = lambda pos: pl.BlockSpec((HEAD_DIM, w), lambda h, i: (h, pos(i)))
    return pl.pallas_call(
        functools.partial(_wattn_body, n_tiles=n_tiles, n_ctx=n_ctx),
        grid=(A_KV_HEADS, n_tiles),
        in_specs=[pl.BlockSpec(memory_space=pltpu.SMEM),
                  pl.BlockSpec((tq, gw), lambda h, i: (i, h)),
                  pl.BlockSpec((n_ctx, HEAD_DIM), lambda h, i: (0, K_COL0 + h)),
                  pl.BlockSpec((HEAD_DIM, n_ctx), lambda h, i: (h, 0)),
                  kblk(before), pl.BlockSpec((tq, HEAD_DIM), lambda h, i: (i, K_COL0 + h)), kblk(after),
                  vblk(before), pl.BlockSpec((HEAD_DIM, tq), lambda h, i: (h, i)), vblk(after)],
        out_specs=pl.BlockSpec((tq, gw), lambda h, i: (i, h)),
        out_shape=jax.ShapeDtypeStruct((s, A_Q_HEADS * HEAD_DIM), BF16),
        compiler_params=_params("parallel", "parallel"),
        name="window_attention",
    )(sink, qkv, qkv_ctx, vt_ctx, qkv, qkv, qkv, vt, vt, vt)


def _cattn_body(sink_ref, q_ref, k_ref, v_ref, o_ref, *, n_ctx):
    h = pl.program_id(0)
    q = _stack_heads(q_ref)
    s = _qk(q, k_ref[...]) * jnp.where(h >= A_KV_HEADS, LN_2, 1.0)
    sink = _sink_column(sink_ref, h * GROUP, n_ctx)
    m = jnp.maximum(jnp.max(s, axis=-1, keepdims=True), sink)
    p = jnp.exp(s - m)
    denom = jnp.sum(p, axis=-1, keepdims=True) + jnp.exp(sink - m)
    out = jnp.dot(p.astype(BF16), v_ref[...], preferred_element_type=F32) / denom
    _unstack_heads(o_ref, out, n_ctx)


def _context_attention(qkv, sink):
    n_ctx = qkv.shape[0]
    gw = GROUP * HEAD_DIM
    sink_all = jnp.concatenate([sink.astype(F32), jnp.full((B_Q_HEADS,), -jnp.inf, F32)])
    kv = lambda col0: pl.BlockSpec((n_ctx, HEAD_DIM), lambda h: (0, col0 + h))
    return pl.pallas_call(
        functools.partial(_cattn_body, n_ctx=n_ctx),
        grid=(KV_HEADS,),
        in_specs=[pl.BlockSpec(memory_space=pltpu.SMEM),
                  pl.BlockSpec((n_ctx, gw), lambda h: (0, h)), kv(K_COL0), kv(K_COL0 + KV_HEADS)],
        out_specs=pl.BlockSpec((n_ctx, gw), lambda h: (0, h)),
        out_shape=jax.ShapeDtypeStruct((n_ctx, Q_WIDTH), BF16),
        compiler_params=_params("parallel"),
        name="context_attention",
    )(sink_all, qkv, qkv, qkv)


def _filter_body(feat_ref, w1_ref, b1_ref, f1_ref, w2_ref, b2_ref, f2_ref, w3_ref, dl_ref,
                 h_ref, ss_ref, *, n_tok):
    i = pl.program_id(0)
    tl = feat_ref.shape[0]
    mm = lambda a, b: jnp.dot(a.astype(BF16), b.astype(BF16), preferred_element_type=F32)
    hid = jnp.sin(f1_ref[...] * (mm(feat_ref[...], w1_ref[...]) + b1_ref[...]))
    hid = jnp.sin(f2_ref[...] * (mm(hid, w2_ref[...]) + b2_ref[...]))
    h = mm(hid, w3_ref[...])
    t = i * tl + lax.broadcasted_iota(jnp.int32, (tl, 1), 0)
    offs = jnp.abs(t - n_tok // 2).astype(F32) * (2.0 / n_tok)
    h = h * jnp.exp(-offs * dl_ref[...])
    h_ref[...] = h.astype(h_ref.dtype)

    @pl.when(i == 0)
    def _():
        ss_ref[...] = jnp.zeros(ss_ref.shape, F32)

    ss_ref[...] += jnp.sum(h * h, axis=0, keepdims=True)


def _hyena_filters(n_tok, d, w1, b1, fr1, w2, b2, fr2, w3):
    bands = (FILTER_EMB - 1) // 2
    t = jnp.linspace(0.0, 1.0, n_tok, dtype=F32)[:, None]
    wv = 2.0 * math.pi * jnp.arange(n_tok, dtype=F32)[:, None] / n_tok
    f = jnp.linspace(1e-4, bands - 1, bands, dtype=F32)[None]
    feats = jnp.concatenate([t, jnp.cos(f * wv), -jnp.sin(f * wv)], axis=-1)
    deltas = jnp.abs(jnp.linspace(MIN_DECAY, MAX_DECAY, d, dtype=F32))
    hidden = w1.shape[1]
    pad_e, pad_h = LANES - FILTER_EMB, LANES - hidden
    padv = lambda v: jnp.pad(v.reshape(1, hidden), ((0, 0), (0, pad_h)))
    c = HYENA_ORDER * d
    tl = _tile(n_tok, 256, 8)
    full = lambda shape: pl.BlockSpec(shape, lambda i: (0, 0))
    return pl.pallas_call(
        functools.partial(_filter_body, n_tok=n_tok),
        grid=(n_tok // tl,),
        in_specs=[pl.BlockSpec((tl, LANES), lambda i: (i, 0)),
                  full((LANES, LANES)), full((1, LANES)), full((1, LANES)),
                  full((LANES, LANES)), full((1, LANES)), full((1, LANES)),
                  full((LANES, c)), full((1, c))],
        out_specs=[pl.BlockSpec((tl, c), lambda i: (i, 0)), full((1, c))],
        out_shape=[jax.ShapeDtypeStruct((n_tok, c), BF16), jax.ShapeDtypeStruct((1, c), F32)],
        compiler_params=_params("arbitrary"),
        name="hyena_filter",
    )(jnp.pad(feats, ((0, 0), (0, pad_e))),
      jnp.pad(w1, ((0, pad_e), (0, pad_h))), padv(b1), padv(fr1),
      jnp.pad(w2, ((0, pad_h), (0, pad_h))), padv(b2), padv(fr2),
      jnp.pad(w3, ((0, pad_h), (0, 0))), jnp.tile(deltas, HYENA_ORDER).reshape(1, c))


def _lane_table(vals):
    return jnp.broadcast_to(jnp.asarray(vals)[..., None], vals.shape + (LANES,))


SUBLANES = 8
DFT_ROW_GROUP = 16


@functools.lru_cache(maxsize=None)
def _dft_plan(n_tok):
    n = 2 * n_tok
    log = n.bit_length() - 1
    assert 1 << log == n, "sequence length must be a power of two"
    n2 = 1 << (log // 2)
    n1 = n // n2
    assert n1 % 4 == 0 and n2 % DFT_ROW_GROUP == 0
    eye = np.eye(SUBLANES)
    nk1 = n1 // 2 + 1
    nk1_pad = -(-nk1 // SUBLANES) * SUBLANES
    a1 = 2.0 * np.pi * np.outer(np.arange(nk1), np.arange(n1 // 2)) / n1
    f1 = np.kron(np.concatenate([np.cos(a1), -np.sin(a1)], axis=0), eye)
    a2 = 2.0 * np.pi * np.outer(np.arange(n2), np.arange(n2)) / n2
    c2, s2 = np.cos(a2), np.sin(a2)
    g2 = np.block([[c2, s2], [-s2, c2]])
    g2i = np.block([[c2, -s2], [s2, c2]])
    rows = np.arange(n1 // 4, n1 // 4 + n1 // 2)
    a1i = 2.0 * np.pi * np.outer(rows, np.arange(nk1_pad)) / n1
    weight = np.where((np.arange(nk1_pad) == 0) | (np.arange(nk1_pad) == n1 // 2), 1.0, 2.0)
    weight = np.where(np.arange(nk1_pad) < nk1, weight, 0.0) / n
    f1i = np.kron(np.concatenate([np.cos(a1i) * weight, -np.sin(a1i) * weight], axis=1), eye)
    tw = 2.0 * np.pi * np.outer(np.arange(n2), np.arange(nk1)) / n
    tw1 = tw.reshape(n2 // SUBLANES, SUBLANES, nk1).transpose(0, 2, 1).reshape(n2 // SUBLANES, nk1 * SUBLANES)
    f32 = lambda m: m.astype(np.float32)
    return dict(n1=n1, n2=n2, nk1=nk1, nk1_pad=nk1_pad, f1=f32(f1), g2=f32(g2), g2i=f32(g2i), f1i=f32(f1i),
                tw1_cos=f32(np.cos(tw1)), tw1_sin=f32(np.sin(tw1)),
                tw2_cos=f32(np.cos(tw.T)), tw2_sin=f32(np.sin(tw.T)))


def _first_step():
    return (pl.program_id(0) == 0) & (pl.program_id(1) == 0)


def _stage1_body(f_ref, z_ref, c_ref, s_ref, o_ref, fb_ref):
    n1h, g, tc = z_ref.shape
    nk1 = o_ref.shape[1]
    half = nk1 * SUBLANES
    reps = tc // LANES

    @pl.when(_first_step())
    def _():
        fb_ref[...] = f_ref[...].astype(BF16)

    z = z_ref[...].astype(F32)
    re_parts, im_parts = [], []
    for a in range(g // SUBLANES):
        zz = z[:, a * SUBLANES:(a + 1) * SUBLANES, :].reshape(n1h * SUBLANES, tc)
        acc = jnp.dot(fb_ref[...], zz.astype(BF16), preferred_element_type=F32)
        re, im = acc[:half], acc[half:]
        c = jnp.tile(c_ref[a], (1, reps))
        s = jnp.tile(s_ref[a], (1, reps))
        re_parts.append((re * c + im * s).reshape(nk1, SUBLANES, tc))
        im_parts.append((im * c - re * s).reshape(nk1, SUBLANES, tc))
    o_ref[0] = jnp.concatenate(re_parts, axis=1).astype(o_ref.dtype)
    o_ref[1] = jnp.concatenate(im_parts, axis=1).astype(o_ref.dtype)


def _dft_stage1(plan, src, width, part=0):
    n1, n2, nk1 = plan["n1"], plan["n2"], plan["nk1"]
    g = DFT_ROW_GROUP
    tc = _tile(width, 1024, LANES)
    nc = width // tc
    sub = g // SUBLANES
    tw = lambda t: t.reshape(n2 // g, sub, nk1 * SUBLANES, LANES)
    tw_spec = pl.BlockSpec((None, sub, nk1 * SUBLANES, LANES), lambda i, j: (i, 0, 0, 0))
    f = jnp.asarray(plan["f1"])
    return pl.pallas_call(
        _stage1_body,
        grid=(n2 // g, nc),
        in_specs=[pl.BlockSpec(f.shape, lambda i, j: (0, 0)),
                  pl.BlockSpec((n1 // 2, g, tc), lambda i, j: (0, i, part * nc + j)),
                  tw_spec, tw_spec],
        out_specs=pl.BlockSpec((2, nk1, g, tc), lambda i, j: (0, 0, i, j)),
        out_shape=jax.ShapeDtypeStruct((2, nk1, n2, width), BF16),
        scratch_shapes=[pltpu.VMEM(f.shape, BF16)],
        compiler_params=_params("arbitrary", "arbitrary"),
        name="dft_stage1",
    )(f, src.reshape(n1 // 2, n2, src.shape[1]), tw(plan["tw1_cos_lanes"]), tw(plan["tw1_sin_lanes"]))


def _istage1_body(f_ref, q_ref, x_ref, v_ref, skip_ref, o_ref, fb_ref):
    _, nk1, g, tc = q_ref.shape
    n1h = o_ref.shape[0]
    nk1_pad = f_ref.shape[1] // (2 * SUBLANES)

    @pl.when(_first_step())
    def _():
        fb_ref[...] = f_ref[...].astype(BF16)

    q = q_ref[...].astype(F32)
    q = jnp.concatenate([q, jnp.zeros((2, nk1_pad - nk1, g, tc), F32)], axis=1)
    parts = []
    for a in range(g // SUBLANES):
        qq = q[:, :, a * SUBLANES:(a + 1) * SUBLANES, :].reshape(2 * nk1_pad * SUBLANES, tc)
        y = jnp.dot(fb_ref[...], qq.astype(BF16), preferred_element_type=F32)
        parts.append(y.reshape(n1h, SUBLANES, tc))
    y = jnp.concatenate(parts, axis=1)
    o_ref[...] = (x_ref[...] * (y + v_ref[...] * skip_ref[...])).astype(o_ref.dtype)


def _idft_gate(plan, q, x_src, x_part, v_src, v_part, skip, out_dtype):
    n1, n2, nk1 = plan["n1"], plan["n2"], plan["nk1"]
    d = q.shape[3]
    g = DFT_ROW_GROUP
    tc = _tile(d, 1024, LANES)
    nc = d // tc
    f = jnp.asarray(plan["f1i"])
    tok = lambda part: pl.BlockSpec((n1 // 2, g, tc), lambda i, j: (0, i, part * nc + j))
    view = lambda arr: arr.reshape(n1 // 2, n2, arr.shape[1])
    out = pl.pallas_call(
        _istage1_body,
        grid=(n2 // g, nc),
        in_specs=[pl.BlockSpec(f.shape, lambda i, j: (0, 0)),
                  pl.BlockSpec((2, nk1, g, tc), lambda i, j: (0, 0, i, j)),
                  tok(x_part), tok(v_part), pl.BlockSpec((1, tc), lambda i, j: (0, j))],
        out_specs=tok(0),
        out_shape=jax.ShapeDtypeStruct((n1 // 2, n2, d), out_dtype),
        scratch_shapes=[pltpu.VMEM(f.shape, BF16)],
        compiler_params=_params("arbitrary", "arbitrary"),
        name="idft_gate",
    )(f, q, view(x_src), view(v_src), skip.reshape(1, d))
    return out.reshape(n1 // 2 * n2, d)


SPECTRAL_ROWS_PER_STEP = 5


def _spectral_body(a_ref, ah_ref, ss_ref, g2_ref, g2i_ref, c_ref, s_ref, o_ref):
    _, kb, n2, td = a_ref.shape
    g2 = g2_ref[...].astype(BF16)
    g2i = g2i_ref[...].astype(BF16)
    scale = lax.rsqrt(ss_ref[...] + EPS)
    reps = td // LANES
    for b in range(kb):
        stack = lambda ref: jnp.concatenate([ref[0, b], ref[1, b]], axis=0)
        x = jnp.dot(g2, stack(a_ref), preferred_element_type=F32)
        hf = jnp.dot(g2, stack(ah_ref), preferred_element_type=F32)
        xr, xi, hr, hi = x[:n2], x[n2:], hf[:n2], hf[n2:]
        y = jnp.concatenate([(xr * hr - xi * hi) * scale, (xr * hi + xi * hr) * scale], axis=0)
        p = jnp.dot(g2i, y.astype(BF16), preferred_element_type=F32)
        pr, pi = p[:n2], p[n2:]
        c = jnp.tile(c_ref[b], (1, reps))
        s = jnp.tile(s_ref[b], (1, reps))
        o_ref[0, b] = (pr * c - pi * s).astype(o_ref.dtype)
        o_ref[1, b] = (pr * s + pi * c).astype(o_ref.dtype)


def _spectral_conv(plan, a, ah, ss, order):
    n1, n2, d = a.shape[1:]
    td = _tile(d, 1024, LANES)
    nd = d // td
    kb = max(b for b in range(1, SPECTRAL_ROWS_PER_STEP + 1) if n1 % b == 0)
    blk = lambda off: pl.BlockSpec((2, kb, n2, td), lambda k1, j: (0, k1, 0, off + j))
    const = pl.BlockSpec((2 * n2, 2 * n2), lambda k1, j: (0, 0))
    tw = pl.BlockSpec((kb, n2, LANES), lambda k1, j: (k1, 0, 0))
    return pl.pallas_call(
        _spectral_body,
        grid=(n1 // kb, nd),
        in_specs=[blk(0), blk(order * nd), pl.BlockSpec((1, td), lambda k1, j: (0, order * nd + j)),
                  const, const, tw, tw],
        out_specs=blk(0),
        out_shape=jax.ShapeDtypeStruct((2, n1, n2, d), BF16),
        compiler_params=_params("parallel", "parallel"),
        name="spectral_conv",
    )(a, ah, ss, jnp.asarray(plan["g2"]), jnp.asarray(plan["g2i"]),
      plan["tw2_cos_lanes"], plan["tw2_sin_lanes"])


@functools.lru_cache(maxsize=None)
def _dft_plan_single(n_tok):
    n = 2 * n_tok
    af = 2.0 * np.pi * np.outer(np.arange(n), np.arange(n_tok)) / n
    fwd = np.concatenate([np.cos(af), -np.sin(af)], axis=0)
    ai = 2.0 * np.pi * np.outer(np.arange(n_tok // 2, n_tok // 2 + n_tok), np.arange(n)) / n
    inv = np.concatenate([np.cos(ai), -np.sin(ai)], axis=1) / n
    return dict(n=n, fwd=fwd.astype(np.float32), inv=inv.astype(np.float32))


def _cmul_body(x_ref, h_ref, ss_ref, o_ref):
    n = x_ref.shape[0] // 2
    xr, xi, hr, hi = x_ref[:n], x_ref[n:], h_ref[:n], h_ref[n:]
    scale = lax.rsqrt(ss_ref[...] + EPS)
    o_ref[:n] = ((xr * hr - xi * hi) * scale).astype(o_ref.dtype)
    o_ref[n:] = ((xr * hi + xi * hr) * scale).astype(o_ref.dtype)


def _cmul(x, h, ss, d, order):
    rows = x.shape[0]
    td = _tile(d, 512, LANES)
    nd = d // td
    return pl.pallas_call(
        _cmul_body,
        grid=(nd,),
        in_specs=[pl.BlockSpec((rows, td), lambda j: (0, j)),
                  pl.BlockSpec((rows, td), lambda j: (0, order * nd + j)),
                  pl.BlockSpec((1, td), lambda j: (0, order * nd + j))],
        out_specs=pl.BlockSpec((rows, td), lambda j: (0, j)),
        out_shape=jax.ShapeDtypeStruct((rows, d), BF16),
        compiler_params=_params("parallel"),
        name="spectrum_product",
    )(x, h, ss)


def _hyena_mixer(x, norm, w_in, layer, b_in, conv_w, conv_b, fparams, skip, plans):
    n_tok, d = x.shape
    u = _norm_matmul_conv(x, *norm, w_in, layer, b_in, conv_w, conv_b, name="hyena_in")
    filt, ss = _hyena_filters(n_tok, d, *fparams)

    if n_tok <= SINGLE_STAGE_MAX_LEN:
        plan = _dft_plan_single(n_tok)
        fwd, inv = jnp.asarray(plan["fwd"]), jnp.asarray(plan["inv"])
        part_cols = lambda p: (lambda tn: (lambda j: p * (d // tn) + j))
        hf = _matmul(fwd, filt, out_dtype=F32, tm=2 * plan["n"], tn=512, name="dft_filter")
        z_src, z_cols = u, part_cols(HYENA_ORDER)
        for order in range(HYENA_ORDER):
            x = _matmul(fwd, z_src, out_dtype=F32, tm=2 * plan["n"], tn=512, w_cols=(d, z_cols),
                        name="dft_signal")
            y = _cmul(x, hf, ss, d, order)
            last = order == HYENA_ORDER - 1
            z_src = _matmul(inv, y, out_dtype=BF16 if last else F32, epilogue=_ep_gate, tn=512,
                            extras=(_tile_extra(u, part_cols(order)), _tile_extra(z_src, z_cols),
                                    _row_extra(skip[order])), name="idft_gate")
            z_cols = part_cols(0)
        return z_src

    if n_tok not in plans:
        base = _dft_plan(n_tok)
        plans[n_tok] = dict(base, **{name + "_lanes": _lane_table(base[name])
                                     for name in ("tw1_cos", "tw1_sin", "tw2_cos", "tw2_sin")})
    plan = plans[n_tok]
    ah = _dft_stage1(plan, filt, HYENA_ORDER * d)
    z_src, z_part = u, HYENA_ORDER
    for order in range(HYENA_ORDER):
        a = _dft_stage1(plan, z_src, d, z_part)
        q = _spectral_conv(plan, a, ah, ss, order)
        last = order == HYENA_ORDER - 1
        z_src = _idft_gate(plan, q, u, order, z_src, z_part, skip[order], BF16 if last else F32)
        z_part = 0
    return z_src


def kernel(x, c, ctx, c_ctx, mod_w, mod_b, norm_mix_g, norm_mlp_g, attn_w_in, attn_w_out, attn_sink, attn_q_norm, attn_k_norm, hy_w_in, hy_b_in, hy_conv_w, hy_conv_b, hy_f_w1, hy_f_b1, hy_f_freq1, hy_f_w2, hy_f_b2, hy_f_freq2, hy_f_w3, hy_skip, hy_w_out, hy_b_out, mlp_w1, mlp_w2, final_g):
    batch, n_lat, d = x.shape
    n_ctx = ctx.shape[1]
    depth = mod_w.shape[0]
    assert batch == 1 and n_lat % GRID_W == 0 and n_lat % WINDOW == 0
    last_ctx_layer = 2 * ((depth - 1) // 2)
    rope = _rope_tables(n_lat)
    d_ff = mlp_w1.shape[2]
    attn_w_out_bf, hy_w_out_bf = attn_w_out.astype(BF16), hy_w_out.astype(BF16)
    dft_plans = {}

    cond = jnp.zeros((16, d), F32).at[0].set(c[0]).at[1].set(c_ctx)
    mod = _modulation(cond, mod_w, mod_b)

    xs = x[0]
    cs = ctx[0]
    for i in range(depth):
        j = i // 2
        is_attn = i % 2 == 0
        ctx_updated = i < last_ctx_layer
        sh1, sc1, g1, sh2, sc2, g2 = jnp.split(mod[i, 0:1], N_MOD, axis=-1)
        csh1, csc1, cg1, csh2, csc2, cg2 = jnp.split(mod[i, 1:2], N_MOD, axis=-1)
        if is_attn:
            qkv, h_lat = _attn_in(xs, norm_mix_g[i], sh1, sc1, attn_w_in, j, attn_q_norm[j], attn_k_norm[j], rope)
            qkv_c, h_ctx = _attn_in(cs, norm_mix_g[i], csh1, csc1, attn_w_in, j, attn_q_norm[j], attn_k_norm[j],
                                    None)
            v_col0 = Q_WIDTH + KV_WIDTH
            vt = _project_t(attn_w_in, j, v_col0, KV_WIDTH, h_lat)
            vt_c = _project_t(attn_w_in, j, v_col0, KV_WIDTH, h_ctx)
            o_a = _window_attention(qkv, vt, qkv_c, vt_c, attn_sink[j])
            o_b = _global_attention(qkv, vt, qkv_c, vt_c)
            xs = _attn_out(o_a, o_b, attn_w_out_bf, j, xs, g1)
            if ctx_updated:
                o_c = _context_attention(qkv_c, attn_sink[j])
                cs = _matmul(o_c, attn_w_out_bf, layer=j, out_dtype=F32, epilogue=_ep_resid,
                             extras=(_tile_extra(cs), _row_extra(cg1)), name="attn_out_ctx")
        else:
            fparams = (hy_f_w1[j], hy_f_b1[j], hy_f_freq1[j], hy_f_w2[j], hy_f_b2[j], hy_f_freq2[j], hy_f_w3[j])
            z = _hyena_mixer(xs, (norm_mix_g[i], sh1, sc1), hy_w_in, j, hy_b_in[j], hy_conv_w[j], hy_conv_b[j],
                             fparams, hy_skip[j], dft_plans)
            xs = _matmul(z, hy_w_out_bf, layer=j, out_dtype=F32, epilogue=_ep_resid_bias, tm=512, tn=2048,
                         extras=(_tile_extra(xs), _row_extra(g1), _row_extra(hy_b_out[j])), name="hyena_out")
            if ctx_updated:
                z_c = _hyena_mixer(cs, (norm_mix_g[i], csh1, csc1), hy_w_in, j, hy_b_in[j], hy_conv_w[j],
                                   hy_conv_b[j], fparams, hy_skip[j], dft_plans)
                cs = _matmul(z_c, hy_w_out_bf, layer=j, out_dtype=F32, epilogue=_ep_resid_bias,
                             extras=(_tile_extra(cs), _row_extra(cg1), _row_extra(hy_b_out[j])),
                             name="hyena_out_ctx")
        a1 = _norm_matmul(xs, norm_mlp_g[i], sh2, sc2, mlp_w1, layer=i, out_dtype=BF16, epilogue=_ep_relu2,
                          name="mlp_up")
        xs = _matmul(a1, mlp_w2, layer=i, out_dtype=F32, epilogue=_ep_resid, tm=BIG_TM, tn=256, tk=d_ff,
                     extras=(_tile_extra(xs), _row_extra(g2)), name="mlp_down")
        if ctx_updated:
            a1c = _norm_matmul(cs, norm_mlp_g[i], csh2, csc2, mlp_w1, layer=i, out_dtype=BF16,
                               epilogue=_ep_relu2, name="mlp_up_ctx")
            cs = _matmul(a1c, mlp_w2, layer=i, out_dtype=F32, epilogue=_ep_resid, tn=512, tk=d_ff,
                         extras=(_tile_extra(cs), _row_extra(cg2)), name="mlp_down_ctx")
    return _rms_norm(xs, final_g, out_dtype=F32)[None]
```

```python
import functools
import math

import numpy as np
import jax
import jax.numpy as jnp
from jax import lax
from jax.experimental import pallas as pl
from jax.experimental.pallas import tpu as pltpu

F32 = jnp.float32
BF16 = jnp.bfloat16

GRID_W = 64
HEAD_DIM = 128
A_Q_HEADS = 8
A_KV_HEADS = 2
B_Q_HEADS = 8
B_KV_HEADS = 2
GROUP = A_Q_HEADS // A_KV_HEADS
WINDOW = 128
ROPE_THETA = 10000.0
AXIS_DIM = HEAD_DIM // 2
Q_WIDTH = (A_Q_HEADS + B_Q_HEADS) * HEAD_DIM
KV_HEADS = A_KV_HEADS + B_KV_HEADS
KV_WIDTH = KV_HEADS * HEAD_DIM
HYENA_ORDER = 2
SHORT_CONV = 3
FILTER_EMB = 33
DECAY_TARGET = 1e-2
FAST_DECAY_PCT = 0.3
SLOW_DECAY_PCT = 1.5
MAX_DECAY = math.log(DECAY_TARGET) / FAST_DECAY_PCT
MIN_DECAY = math.log(DECAY_TARGET) / SLOW_DECAY_PCT
N_MOD = 6
EPS = 1e-6
LOG2_E = math.log2(math.e)
LN_2 = math.log(2.0)

VMEM_LIMIT_BYTES = 56 * 1024 * 1024
LANES = 128
SINGLE_STAGE_MAX_LEN = 512
BIG_TM = 1024


def _params(*sem):
    return pltpu.CompilerParams(dimension_semantics=sem, vmem_limit_bytes=VMEM_LIMIT_BYTES)


def _tile(n, pref, align):
    if n <= pref:
        return n
    t = (pref // align) * align
    while t >= align:
        if n % t == 0:
            return t
        t -= align
    return n


def _mod_body(a_ref, w_ref, b_ref, o_ref):
    a = a_ref[...]
    act = a * (1.0 / (1.0 + jnp.exp(-a)))
    o_ref[...] = jnp.dot(act.astype(BF16), w_ref[...].astype(BF16),
                         preferred_element_type=F32) + b_ref[...]


def _modulation(cond, mod_w, mod_b):
    depth, d, n = mod_w.shape
    r = cond.shape[0]
    tn = _tile(n, 1024, LANES)
    return pl.pallas_call(
        _mod_body,
        grid=(depth, n // tn),
        in_specs=[pl.BlockSpec((r, d), lambda l, j: (0, 0)),
                  pl.BlockSpec((None, d, tn), lambda l, j: (l, 0, j)),
                  pl.BlockSpec((None, 1, tn), lambda l, j: (l, 0, j))],
        out_specs=pl.BlockSpec((None, r, tn), lambda l, j: (l, 0, j)),
        out_shape=jax.ShapeDtypeStruct((depth, r, n), F32),
        compiler_params=_params("parallel", "parallel"),
        name="modulation",
    )(cond, mod_w, mod_b.reshape(depth, 1, n))


def _norm_body(x_ref, g_ref, *rest, modulated):
    x = x_ref[...]
    y = x * lax.rsqrt(jnp.mean(x * x, axis=-1, keepdims=True) + EPS) * g_ref[...]
    if modulated:
        sh_ref, sc_ref, o_ref = rest
        y = y * (1.0 + sc_ref[...]) + sh_ref[...]
    else:
        (o_ref,) = rest
    o_ref[...] = y.astype(o_ref.dtype)


def _rms_norm(x, g, shift=None, scale=None, out_dtype=None):
    t, d = x.shape
    out_dtype = BF16 if out_dtype is None else out_dtype
    tm = _tile(t, 512, 16)
    row = pl.BlockSpec((1, d), lambda i: (0, 0))
    vecs = [g.reshape(1, d)]
    if shift is not None:
        vecs += [shift.reshape(1, d), scale.reshape(1, d)]
    return pl.pallas_call(
        functools.partial(_norm_body, modulated=shift is not None),
        grid=(t // tm,),
        in_specs=[pl.BlockSpec((tm, d), lambda i: (i, 0))] + [row] * len(vecs),
        out_specs=pl.BlockSpec((tm, d), lambda i: (i, 0)),
        out_shape=jax.ShapeDtypeStruct((t, d), out_dtype),
        compiler_params=_params("parallel"),
        name="rms_norm",
    )(x, *vecs)


def _ep_none(acc):
    return acc


def _ep_relu2(acc):
    return jnp.square(jnp.maximum(acc, 0.0))


def _ep_resid(acc, x, g):
    return x + g * acc


def _ep_resid_bias(acc, x, g, b):
    return x + g * (acc + b)


def _ep_gate(acc, x, v, skip):
    return x * (acc + v * skip)


def _mm_body(*refs, nk, epilogue, n_extra):
    a_ref, w_ref = refs[0], refs[1]
    extra = refs[2:2 + n_extra]
    o_ref = refs[2 + n_extra]
    a = a_ref[...].astype(BF16)
    w = w_ref[...].astype(BF16)
    part = jnp.dot(a, w, preferred_element_type=F32)
    if nk == 1:
        o_ref[...] = epilogue(part, *[e[...] for e in extra]).astype(o_ref.dtype)
        return
    acc_ref = refs[3 + n_extra]
    k = pl.program_id(2)

    @pl.when(k == 0)
    def _():
        acc_ref[...] = part

    @pl.when(k > 0)
    def _():
        acc_ref[...] += part

    @pl.when(k == nk - 1)
    def _():
        o_ref[...] = epilogue(acc_ref[...], *[e[...] for e in extra]).astype(o_ref.dtype)


def _weight_spec(w, layer, tk, tn, index):
    if layer is None:
        return pl.BlockSpec((tk, tn), index)
    return pl.BlockSpec((None, tk, tn), lambda *ids: (layer,) + tuple(index(*ids)))


def _matmul(a, w, *, out_dtype, epilogue=_ep_none, extras=(), tm=512, tn=1024, tk=2048,
            w_cols=None, layer=None, name="matmul"):
    m, kdim = a.shape
    n = w.shape[-1] if w_cols is None else w_cols[0]
    tm = _tile(m, tm, 16)
    tn = _tile(n, tn, LANES)
    tk = _tile(kdim, tk, LANES)
    nk = kdim // tk
    wmap = (lambda j: j) if w_cols is None else w_cols[1](tn)
    in_specs = [pl.BlockSpec((tm, tk), lambda i, j, k: (i, k)),
                _weight_spec(w, layer, tk, tn, lambda i, j, k: (k, wmap(j)))]
    in_specs += [fn(tm, tn) for _, fn in extras]
    return pl.pallas_call(
        functools.partial(_mm_body, nk=nk, epilogue=epilogue, n_extra=len(extras)),
        grid=(m // tm, n // tn, nk),
        in_specs=in_specs,
        out_specs=pl.BlockSpec((tm, tn), lambda i, j, k: (i, j)),
        out_shape=jax.ShapeDtypeStruct((m, n), out_dtype),
        scratch_shapes=[pltpu.VMEM((tm, tn), F32)] if nk > 1 else [],
        compiler_params=_params("parallel", "parallel", "arbitrary"),
        name=name,
    )(a, w, *[arr for arr, _ in extras])


def _modulated_norm(x, g, shift, scale):
    gain = g * (1.0 + scale)
    return x * lax.rsqrt(jnp.mean(x * x, axis=-1, keepdims=True) + EPS) * gain + shift


def _nmm_body(x_ref, g_ref, sh_ref, sc_ref, w_ref, *rest, epilogue, n_extra):
    extra = rest[:n_extra]
    o_ref, h_ref = rest[n_extra], rest[n_extra + 1]

    @pl.when(pl.program_id(1) == 0)
    def _():
        h_ref[...] = _modulated_norm(x_ref[...], g_ref[...], sh_ref[...], sc_ref[...]).astype(h_ref.dtype)

    acc = jnp.dot(h_ref[...], w_ref[...].astype(BF16), preferred_element_type=F32)
    o_ref[...] = epilogue(acc, *[e[...] for e in extra]).astype(o_ref.dtype)


def _norm_matmul(x, g, shift, scale, w, *, out_dtype, epilogue=_ep_none, extras=(), tm=1024, tn=1024,
                 layer=None, name="norm_matmul"):
    m, d = x.shape
    n = w.shape[-1]
    tm = _tile(m, tm, 16)
    tn = _tile(n, tn, LANES)
    vec = pl.BlockSpec((1, d), lambda i, j: (0, 0))
    with_k = lambda spec_fn: (lambda bs: pl.BlockSpec(bs.block_shape, lambda i, j: bs.index_map(i, j, 0)))(
        spec_fn(tm, tn))
    return pl.pallas_call(
        functools.partial(_nmm_body, epilogue=epilogue, n_extra=len(extras)),
        grid=(m // tm, n // tn),
        in_specs=[pl.BlockSpec((tm, d), lambda i, j: (i, 0)), vec, vec, vec,
                  _weight_spec(w, layer, d, tn, lambda i, j: (0, j))] + [with_k(fn) for _, fn in extras],
        out_specs=pl.BlockSpec((tm, tn), lambda i, j: (i, j)),
        out_shape=jax.ShapeDtypeStruct((m, n), out_dtype),
        scratch_shapes=[pltpu.VMEM((tm, d), BF16)],
        compiler_params=_params("arbitrary", "arbitrary"),
        name=name,
    )(x, g.reshape(1, d), shift.reshape(1, d), scale.reshape(1, d), w, *[arr for arr, _ in extras])


HALO = 16


def _nmm_conv_body(x_ref, xp_ref, xn_ref, g_ref, sh_ref, sc_ref, w_ref, b_ref, cw_ref, cb_ref, o_ref, h_ref,
                   *, n_row_tiles):
    i = pl.program_id(0)
    tm = x_ref.shape[0]

    @pl.when(pl.program_id(1) == 0)
    def _():
        def norm(ref):
            return _modulated_norm(ref[...], g_ref[...], sh_ref[...], sc_ref[...]).astype(h_ref.dtype)

        h_ref[:HALO] = norm(xp_ref)
        h_ref[HALO:HALO + tm] = norm(x_ref)
        h_ref[HALO + tm:] = norm(xn_ref)

    u = jnp.dot(h_ref[...], w_ref[...].astype(BF16), preferred_element_type=F32) + b_ref[...]
    before = jnp.where(i > 0, u[:HALO], 0.0)
    after = jnp.where(i + 1 < n_row_tiles, u[HALO + tm:], 0.0)
    u = jnp.concatenate([before, u[HALO:HALO + tm], after], axis=0)
    rows = u.shape[0]
    y = (pltpu.roll(u, 1, 0) * cw_ref[0:1, :] + u * cw_ref[1:2, :]
         + pltpu.roll(u, rows - 1, 0) * cw_ref[2:3, :] + cb_ref[...])
    o_ref[...] = y[HALO:HALO + tm]


def _norm_matmul_conv(x, g, shift, scale, w, layer, b, conv_w, conv_b, *, tm=1024, tn=512, name):
    m, d = x.shape
    n = w.shape[-1]
    tm = _tile(m, tm, HALO)
    tn = _tile(n, tn, LANES)
    hb = tm // HALO
    n_row_tiles = m // tm
    vec = pl.BlockSpec((1, d), lambda i, j: (0, 0))
    col = lambda rows: pl.BlockSpec((rows, tn), lambda i, j: (0, j))
    return pl.pallas_call(
        functools.partial(_nmm_conv_body, n_row_tiles=n_row_tiles),
        grid=(n_row_tiles, n // tn),
        in_specs=[pl.BlockSpec((tm, d), lambda i, j: (i, 0)),
                  pl.BlockSpec((HALO, d), lambda i, j: (jnp.maximum(i * hb - 1, 0), 0)),
                  pl.BlockSpec((HALO, d), lambda i, j: (jnp.minimum((i + 1) * hb, m // HALO - 1), 0)),
                  vec, vec, vec, _weight_spec(w, layer, d, tn, lambda i, j: (0, j)),
                  col(1), col(SHORT_CONV), col(1)],
        out_specs=pl.BlockSpec((tm, tn), lambda i, j: (i, j)),
        out_shape=jax.ShapeDtypeStruct((m, n), F32),
        scratch_shapes=[pltpu.VMEM((tm + 2 * HALO, d), BF16)],
        compiler_params=_params("arbitrary", "arbitrary"),
        name=name,
    )(x, x, x, g.reshape(1, d), shift.reshape(1, d), scale.reshape(1, d), w, b.reshape(1, n), conv_w,
      conv_b.reshape(1, n))


def _row_extra(vec, period_blocks=None):
    vec = vec.reshape(1, -1)
    if period_blocks is None:
        return vec, lambda tm, tn: pl.BlockSpec((1, tn), lambda i, j, k: (0, j))
    return vec, lambda tm, tn: pl.BlockSpec((1, tn), lambda i, j, k: (0, j % period_blocks(tn)))


def _tile_extra(arr, colmap=None):
    cm = colmap if colmap is not None else (lambda tn: (lambda j: j))
    return arr, lambda tm, tn: pl.BlockSpec((tm, tn), lambda i, j, k: (i, cm(tn)(j)))


def _rope_tables(n_tok):
    rows = n_tok // GRID_W
    row = jnp.repeat(jnp.arange(rows, dtype=F32), GRID_W)
    col = jnp.tile(jnp.arange(GRID_W, dtype=F32), rows)
    inv = ROPE_THETA ** (-jnp.arange(0, AXIS_DIM, 2, dtype=F32) / AXIS_DIM)
    ang_r = row[:, None] * inv[None]
    ang_c = col[:, None] * inv[None]
    cos = jnp.concatenate([jnp.cos(ang_r)] * 2 + [jnp.cos(ang_c)] * 2, axis=-1)
    sin = jnp.concatenate([-jnp.sin(ang_r), jnp.sin(ang_r), -jnp.sin(ang_c), jnp.sin(ang_c)], axis=-1)
    return cos, sin


HEADS_PER_TILE = 8
ATTN_IN_WIDTH = Q_WIDTH + 2 * KV_WIDTH


def _attn_in_body(*refs, rope):
    if rope:
        x_ref, g_ref, sh_ref, sc_ref, w_ref, qn_ref, kn_ref, cos_ref, sin_ref, o_ref, h_ref = refs
        cos, sin = cos_ref[...], sin_ref[...]
        lane = lax.broadcasted_iota(jnp.int32, cos.shape, 1)
        first = (lane % (AXIS_DIM)) < (AXIS_DIM // 2)
    else:
        x_ref, g_ref, sh_ref, sc_ref, w_ref, qn_ref, kn_ref, o_ref, h_ref = refs
    j = pl.program_id(1)
    scale = HEAD_DIM ** -0.5
    quarter = AXIS_DIM // 2

    @pl.when(j == 0)
    def _():
        h_ref[...] = _modulated_norm(x_ref[...], g_ref[...], sh_ref[...], sc_ref[...]).astype(h_ref.dtype)

    acc = jnp.dot(h_ref[...], w_ref[...].astype(BF16), preferred_element_type=F32)

    def rot(x):
        if not rope:
            return x
        swapped = jnp.where(first, pltpu.roll(x, HEAD_DIM - quarter, 1), pltpu.roll(x, quarter, 1))
        return x * cos + swapped * sin

    def nrm(x, g_ref_):
        return x * lax.rsqrt(jnp.mean(x * x, axis=-1, keepdims=True) + EPS) * g_ref_[...]

    q_a = lambda x: rot(x) * scale
    q_b = lambda x: rot(nrm(x, qn_ref)) * (scale * LOG2_E)
    k_b = lambda x: rot(nrm(x, kn_ref))
    kinds = ([q_a] * A_Q_HEADS + [q_b] * B_Q_HEADS + [rot] * A_KV_HEADS + [k_b] * B_KV_HEADS
             + [lambda x: x] * KV_HEADS)

    def emit(tile_kinds):
        for hh, kind in enumerate(tile_kinds):
            sl = slice(hh * HEAD_DIM, (hh + 1) * HEAD_DIM)
            o_ref[:, sl] = kind(acc[:, sl]).astype(o_ref.dtype)

    for t in range(len(kinds) // HEADS_PER_TILE):
        pl.when(j == t)(functools.partial(emit, kinds[t * HEADS_PER_TILE:(t + 1) * HEADS_PER_TILE]))


def _attn_in(x, g, shift, scale, w, layer, q_norm, k_norm, rope):
    t, d = x.shape
    tm = _tile(t, 512, 16)
    tn = HEADS_PER_TILE * HEAD_DIM
    vec = lambda n: pl.BlockSpec((1, n), lambda i, j: (0, 0))
    in_specs = [pl.BlockSpec((tm, d), lambda i, j: (i, 0)), vec(d), vec(d), vec(d),
                _weight_spec(w, layer, d, tn, lambda i, j: (0, j)), vec(HEAD_DIM), vec(HEAD_DIM)]
    args = [x, g.reshape(1, d), shift.reshape(1, d), scale.reshape(1, d), w,
            q_norm.reshape(1, HEAD_DIM), k_norm.reshape(1, HEAD_DIM)]
    if rope is not None:
        in_specs += [pl.BlockSpec((tm, HEAD_DIM), lambda i, j: (i, 0))] * 2
        args += list(rope)
    return pl.pallas_call(
        functools.partial(_attn_in_body, rope=rope is not None),
        grid=(t // tm, ATTN_IN_WIDTH // tn),
        in_specs=in_specs,
        out_specs=[pl.BlockSpec((tm, tn), lambda i, j: (i, j)), pl.BlockSpec((tm, d), lambda i, j: (i, 0))],
        out_shape=[jax.ShapeDtypeStruct((t, ATTN_IN_WIDTH), BF16), jax.ShapeDtypeStruct((t, d), BF16)],
        compiler_params=_params("arbitrary", "arbitrary"),
        name="attn_in",
    )(*args)


def _stack_heads(q_ref):
    return jnp.concatenate([q_ref[:, g * HEAD_DIM:(g + 1) * HEAD_DIM] for g in range(GROUP)], axis=0)


def _unstack_heads(o_ref, out, t):
    for g in range(GROUP):
        o_ref[:, g * HEAD_DIM:(g + 1) * HEAD_DIM] = out[g * t:(g + 1) * t].astype(o_ref.dtype)


def _sink_column(sink_ref, first, t):
    head = lax.broadcasted_iota(jnp.int32, (GROUP * t, 1), 0) // t
    col = jnp.full((GROUP * t, 1), sink_ref[first], F32)
    for g in range(1, GROUP):
        col = jnp.where(head == g, sink_ref[first + g], col)
    return col


def _qk(q, k):
    return lax.dot_general(q, k, (((1,), (1,)), ((), ())), preferred_element_type=F32)


ONES_ROWS = 16


def _gattn_body(q_ref, kc_ref, vtc_ref, k_ref, vt_ref, o_ref, m_ref, acc_ref, sa_ref, sb_ref,
                *, tq, ck, n_chunks):
    q = _stack_heads(q_ref)
    m_ref[...] = jnp.full(m_ref.shape, -jnp.inf, F32)
    acc_ref[...] = jnp.zeros(acc_ref.shape, F32)

    def scores(c):
        return _qk(k_ref[pl.ds(pl.multiple_of(c * ck, ck), ck), :], q)

    def consume(st, vtb):
        m_old = m_ref[...]
        m_new = jnp.maximum(m_old, jnp.max(st, axis=0, keepdims=True))
        p = jnp.exp2(st - m_new).astype(BF16)
        lhs = jnp.concatenate([vtb, jnp.ones((ONES_ROWS, vtb.shape[1]), BF16)], axis=0)
        acc_ref[...] = jnp.exp2(m_old - m_new) * acc_ref[...] + jnp.dot(lhs, p, preferred_element_type=F32)
        m_ref[...] = m_new

    def values(c):
        return vt_ref[:, pl.ds(pl.multiple_of(c * ck, ck), ck)]

    sa_ref[...] = scores(0)
    consume(_qk(kc_ref[...], q), vtc_ref[...])

    def pair(c2, carry):
        c = 2 * c2
        sb_ref[...] = scores(c + 1)
        consume(sa_ref[...], values(c))
        sa_ref[...] = scores(c + 2)
        consume(sb_ref[...], values(c + 1))
        return carry

    lax.fori_loop(0, n_chunks // 2 - 1, pair, 0)
    last = n_chunks - 2
    sb_ref[...] = scores(last + 1)
    consume(sa_ref[...], values(last))
    consume(sb_ref[...], values(last + 1))
    _unstack_heads(o_ref, (acc_ref[:HEAD_DIM] / acc_ref[HEAD_DIM:HEAD_DIM + 1]).T, tq)


def _attn_out_body(oa_ref, ob_ref, w_ref, x_ref, g_ref, o_ref):
    ka = oa_ref.shape[1]
    w = w_ref[...].astype(BF16)
    acc = (jnp.dot(oa_ref[...], w[:ka], preferred_element_type=F32)
           + jnp.dot(ob_ref[...], w[ka:], preferred_element_type=F32))
    o_ref[...] = x_ref[...] + g_ref[...] * acc


def _attn_out(o_a, o_b, w, layer, x, gate):
    m, ka = o_a.shape
    kb = o_b.shape[1]
    n = w.shape[-1]
    tm = _tile(m, 512, 16)
    tn = _tile(n, 2048, LANES)
    return pl.pallas_call(
        _attn_out_body,
        grid=(m // tm, n // tn),
        in_specs=[pl.BlockSpec((tm, ka), lambda i, j: (i, 0)), pl.BlockSpec((tm, kb), lambda i, j: (i, 0)),
                  _weight_spec(w, layer, ka + kb, tn, lambda i, j: (0, j)),
                  pl.BlockSpec((tm, tn), lambda i, j: (i, j)), pl.BlockSpec((1, tn), lambda i, j: (0, j))],
        out_specs=pl.BlockSpec((tm, tn), lambda i, j: (i, j)),
        out_shape=jax.ShapeDtypeStruct((m, n), F32),
        compiler_params=_params("parallel", "parallel"),
        name="attn_out",
    )(o_a, o_b, w, x, gate.reshape(1, n))


K_COL0 = Q_WIDTH // HEAD_DIM


def _global_attention(qkv, vt, qkv_ctx, vt_ctx):
    s, n_ctx = qkv.shape[0], qkv_ctx.shape[0]
    tq = _tile(s, 1024, LANES)
    ck = _tile(s // 2, 512, LANES)
    n_chunks = s // ck
    assert n_chunks % 2 == 0
    rows = GROUP * tq
    gw = GROUP * HEAD_DIM
    kspec = lambda n: pl.BlockSpec((n, HEAD_DIM), lambda h, i: (0, K_COL0 + A_KV_HEADS + h))
    vspec = lambda n: pl.BlockSpec((HEAD_DIM, n), lambda h, i: (A_KV_HEADS + h, 0))
    return pl.pallas_call(
        functools.partial(_gattn_body, tq=tq, ck=ck, n_chunks=n_chunks),
        grid=(B_KV_HEADS, s // tq),
        in_specs=[pl.BlockSpec((tq, gw), lambda h, i: (i, A_KV_HEADS + h)),
                  kspec(n_ctx), vspec(n_ctx), kspec(s), vspec(s)],
        out_specs=pl.BlockSpec((tq, gw), lambda h, i: (i, h)),
        out_shape=jax.ShapeDtypeStruct((s, B_Q_HEADS * HEAD_DIM), BF16),
        scratch_shapes=[pltpu.VMEM((1, rows), F32), pltpu.VMEM((HEAD_DIM + ONES_ROWS, rows), F32),
                        pltpu.VMEM((ck, rows), F32), pltpu.VMEM((ck, rows), F32)],
        compiler_params=_params("parallel", "parallel"),
        name="global_attention",
    )(qkv, qkv_ctx, vt_ctx, qkv, vt)


def _project_t_body(w_ref, h_ref, o_ref, wt_ref):
    @pl.when(pl.program_id(0) == 0)
    def _():
        wt_ref[...] = w_ref[...].T.astype(wt_ref.dtype)

    o_ref[...] = _qk(wt_ref[...], h_ref[...]).astype(o_ref.dtype)


def _project_t(w, layer, col0, c, h):
    d = w.shape[1]
    t = h.shape[0]
    tt = _tile(t, 1024, LANES)
    return pl.pallas_call(
        _project_t_body,
        grid=(t // tt,),
        in_specs=[pl.BlockSpec((None, d, c), lambda i: (layer, 0, col0 // c)),
                  pl.BlockSpec((tt, d), lambda i: (i, 0))],
        out_specs=pl.BlockSpec((c, tt), lambda i: (0, i)),
        out_shape=jax.ShapeDtypeStruct((c, t), BF16),
        scratch_shapes=[pltpu.VMEM((c, d), BF16)],
        compiler_params=_params("arbitrary"),
        name="project_t",
    )(w, h)


WINDOW_BLOCKS = 2


def _wattn_body(sink_ref, q_ref, kc_ref, vtc_ref, kp_ref, k0_ref, kn_ref, vtp_ref, vt0_ref, vtn_ref, o_ref,
                *, n_tiles, n_ctx):
    h = pl.program_id(0)
    i = pl.program_id(1)
    w = WINDOW
    tq = q_ref.shape[0]
    q = _stack_heads(q_ref)
    kcat = jnp.concatenate([kc_ref[...], kp_ref[...], k0_ref[...], kn_ref[...]], axis=0)
    vtcat = jnp.concatenate([vtc_ref[...], vtp_ref[...], vt0_ref[...], vtn_ref[...]], axis=1)
    st = _qk(kcat, q)
    c = lax.broadcasted_iota(jnp.int32, st.shape, 0) - n_ctx
    r = lax.broadcasted_iota(jnp.int32, st.shape, 1) % tq
    lo = jnp.maximum(r, jnp.where(i >= 1, 0, w))
    hi = jnp.minimum(r + 2 * w, jnp.where(i + 1 < n_tiles, tq + 2 * w - 1, tq + w - 1))
    valid = (c < 0) | ((c >= lo) & (c <= hi))
    st = jnp.where(valid, st, -jnp.inf)
    head = lax.broadcasted_iota(jnp.int32, (1, GROUP * tq), 1) // tq
    sink = jnp.full((1, GROUP * tq), sink_ref[h * GROUP], F32)
    for g in range(1, GROUP):
        sink = jnp.where(head == g, sink_ref[h * GROUP + g], sink)
    m = jnp.maximum(jnp.max(st, axis=0, keepdims=True), sink)
    p = jnp.exp(st - m).astype(BF16)
    lhs = jnp.concatenate([vtcat, jnp.ones((ONES_ROWS, vtcat.shape[1]), BF16)], axis=0)
    acc = jnp.dot(lhs, p, preferred_element_type=F32)
    denom = acc[HEAD_DIM:HEAD_DIM + 1] + jnp.exp(sink - m)
    _unstack_heads(o_ref, (acc[:HEAD_DIM] / denom).T, tq)


def _window_attention(qkv, vt, qkv_ctx, vt_ctx, sink):
    s, n_ctx = qkv.shape[0], qkv_ctx.shape[0]
    w = WINDOW
    nb = s // w
    tb = WINDOW_BLOCKS if nb % WINDOW_BLOCKS == 0 else 1
    tq = tb * w
    n_tiles = nb // tb
    gw = GROUP * HEAD_DIM
    before = lambda i: jnp.maximum(i * tb - 1, 0)
    after = lambda i: jnp.minimum((i + 1) * tb, nb - 1)
    kblk = lambda pos: pl.BlockSpec((w, HEAD_DIM), lambda h, i: (pos(i), K_COL0 + h))
    vblk = lambda pos: pl.BlockSpec((HEAD_DIM, w), lambda h, i: (h, pos(i)))
    return pl.pallas_call(
        functools.partial(_wattn_body, n_tiles=n_tiles, n_ctx=n_ctx),
        grid=(A_KV_HEADS, n_tiles),
        in_specs=[pl.BlockSpec(memory_space=pltpu.SMEM),
                  pl.BlockSpec((tq, gw), lambda h, i: (i, h)),
                  pl.BlockSpec((n_ctx, HEAD_DIM), lambda h, i: (0, K_COL0 + h)),
                  pl.BlockSpec((HEAD_DIM, n_ctx), lambda h, i: (h, 0)),
                  kblk(before), pl.BlockSpec((tq, HEAD_DIM), lambda h, i: (i, K_COL0 + h)), kblk(after),
                  vblk(before), pl.BlockSpec((HEAD_DIM, tq), lambda h, i: (h, i)), vblk(after)],
        out_specs=pl.BlockSpec((tq, gw), lambda h, i: (i, h)),
        out_shape=jax.ShapeDtypeStruct((s, A_Q_HEADS * HEAD_DIM), BF16),
        compiler_params=_params("parallel", "parallel"),
        name="window_attention",
    )(sink, qkv, qkv_ctx, vt_ctx, qkv, qkv, qkv, vt, vt, vt)


def _cattn_body(sink_ref, q_ref, k_ref, v_ref, o_ref, *, n_ctx):
    h = pl.program_id(0)
    q = _stack_heads(q_ref)
    s = _qk(q, k_ref[...]) * jnp.where(h >= A_KV_HEADS, LN_2, 1.0)
    sink = _sink_column(sink_ref, h * GROUP, n_ctx)
    m = jnp.maximum(jnp.max(s, axis=-1, keepdims=True), sink)
    p = jnp.exp(s - m)
    denom = jnp.sum(p, axis=-1, keepdims=True) + jnp.exp(sink - m)
    out = jnp.dot(p.astype(BF16), v_ref[...], preferred_element_type=F32) / denom
    _unstack_heads(o_ref, out, n_ctx)


def _context_attention(qkv, sink):
    n_ctx = qkv.shape[0]
    gw = GROUP * HEAD_DIM
    sink_all = jnp.concatenate([sink.astype(F32), jnp.full((B_Q_HEADS,), -jnp.inf, F32)])
    kv = lambda col0: pl.BlockSpec((n_ctx, HEAD_DIM), lambda h: (0, col0 + h))
    return pl.pallas_call(
        functools.partial(_cattn_body, n_ctx=n_ctx),
        grid=(KV_HEADS,),
        in_specs=[pl.BlockSpec(memory_space=pltpu.SMEM),
                  pl.BlockSpec((n_ctx, gw), lambda h: (0, h)), kv(K_COL0), kv(K_COL0 + KV_HEADS)],
        out_specs=pl.BlockSpec((n_ctx, gw), lambda h: (0, h)),
        out_shape=jax.ShapeDtypeStruct((n_ctx, Q_WIDTH), BF16),
        compiler_params=_params("parallel"),
        name="context_attention",
    )(sink_all, qkv, qkv, qkv)


def _filter_body(feat_ref, w1_ref, b1_ref, f1_ref, w2_ref, b2_ref, f2_ref, w3_ref, dl_ref,
                 h_ref, ss_ref, *, n_tok):
    i = pl.program_id(0)
    tl = feat_ref.shape[0]
    mm = lambda a, b: jnp.dot(a.astype(BF16), b.astype(BF16), preferred_element_type=F32)
    hid = jnp.sin(f1_ref[...] * (mm(feat_ref[...], w1_ref[...]) + b1_ref[...]))
    hid = jnp.sin(f2_ref[...] * (mm(hid, w2_ref[...]) + b2_ref[...]))
    h = mm(hid, w3_ref[...])
    t = i * tl + lax.broadcasted_iota(jnp.int32, (tl, 1), 0)
    offs = jnp.abs(t - n_tok // 2).astype(F32) * (2.0 / n_tok)
    h = h * jnp.exp(-offs * dl_ref[...])
    h_ref[...] = h.astype(h_ref.dtype)

    @pl.when(i == 0)
    def _():
        ss_ref[...] = jnp.zeros(ss_ref.shape, F32)

    ss_ref[...] += jnp.sum(h * h, axis=0, keepdims=True)


def _hyena_filters(n_tok, d, w1, b1, fr1, w2, b2, fr2, w3):
    bands = (FILTER_EMB - 1) // 2
    t = jnp.linspace(0.0, 1.0, n_tok, dtype=F32)[:, None]
    wv = 2.0 * math.pi * jnp.arange(n_tok, dtype=F32)[:, None] / n_tok
    f = jnp.linspace(1e-4, bands - 1, bands, dtype=F32)[None]
    feats = jnp.concatenate([t, jnp.cos(f * wv), -jnp.sin(f * wv)], axis=-1)
    deltas = jnp.abs(jnp.linspace(MIN_DECAY, MAX_DECAY, d, dtype=F32))
    hidden = w1.shape[1]
    pad_e, pad_h = LANES - FILTER_EMB, LANES - hidden
    padv = lambda v: jnp.pad(v.reshape(1, hidden), ((0, 0), (0, pad_h)))
    c = HYENA_ORDER * d
    tl = _tile(n_tok, 256, 8)
    full = lambda shape: pl.BlockSpec(shape, lambda i: (0, 0))
    return pl.pallas_call(
        functools.partial(_filter_body, n_tok=n_tok),
        grid=(n_tok // tl,),
        in_specs=[pl.BlockSpec((tl, LANES), lambda i: (i, 0)),
                  full((LANES, LANES)), full((1, LANES)), full((1, LANES)),
                  full((LANES, LANES)), full((1, LANES)), full((1, LANES)),
                  full((LANES, c)), full((1, c))],
        out_specs=[pl.BlockSpec((tl, c), lambda i: (i, 0)), full((1, c))],
        out_shape=[jax.ShapeDtypeStruct((n_tok, c), BF16), jax.ShapeDtypeStruct((1, c), F32)],
        compiler_params=_params("arbitrary"),
        name="hyena_filter",
    )(jnp.pad(feats, ((0, 0), (0, pad_e))),
      jnp.pad(w1, ((0, pad_e), (0, pad_h))), padv(b1), padv(fr1),
      jnp.pad(w2, ((0, pad_h), (0, pad_h))), padv(b2), padv(fr2),
      jnp.pad(w3, ((0, pad_h), (0, 0))), jnp.tile(deltas, HYENA_ORDER).reshape(1, c))


def _lane_table(vals):
    return jnp.broadcast_to(jnp.asarray(vals)[..., None], vals.shape + (LANES,))


SUBLANES = 8
DFT_ROW_GROUP = 16


@functools.lru_cache(maxsize=None)
def _dft_plan(n_tok):
    n = 2 * n_tok
    log = n.bit_length() - 1
    assert 1 << log == n, "sequence length must be a power of two"
    n2 = 1 << (log // 2)
    n1 = n // n2
    assert n1 % 4 == 0 and n2 % DFT_ROW_GROUP == 0
    eye = np.eye(SUBLANES)
    nk1 = n1 // 2 + 1
    nk1_pad = -(-nk1 // SUBLANES) * SUBLANES
    a1 = 2.0 * np.pi * np.outer(np.arange(nk1), np.arange(n1 // 2)) / n1
    f1 = np.kron(np.concatenate([np.cos(a1), -np.sin(a1)], axis=0), eye)
    a2 = 2.0 * np.pi * np.outer(np.arange(n2), np.arange(n2)) / n2
    c2, s2 = np.cos(a2), np.sin(a2)
    g2 = np.block([[c2, s2], [-s2, c2]])
    g2i = np.block([[c2, -s2], [s2, c2]])
    rows = np.arange(n1 // 4, n1 // 4 + n1 // 2)
    a1i = 2.0 * np.pi * np.outer(rows, np.arange(nk1_pad)) / n1
    weight = np.where((np.arange(nk1_pad) == 0) | (np.arange(nk1_pad) == n1 // 2), 1.0, 2.0)
    weight = np.where(np.arange(nk1_pad) < nk1, weight, 0.0) / n
    f1i = np.kron(np.concatenate([np.cos(a1i) * weight, -np.sin(a1i) * weight], axis=1), eye)
    tw = 2.0 * np.pi * np.outer(np.arange(n2), np.arange(nk1)) / n
    tw1 = tw.reshape(n2 // SUBLANES, SUBLANES, nk1).transpose(0, 2, 1).reshape(n2 // SUBLANES, nk1 * SUBLANES)
    f32 = lambda m: m.astype(np.float32)
    return dict(n1=n1, n2=n2, nk1=nk1, nk1_pad=nk1_pad, f1=f32(f1), g2=f32(g2), g2i=f32(g2i), f1i=f32(f1i),
                tw1_cos=f32(np.cos(tw1)), tw1_sin=f32(np.sin(tw1)),
                tw2_cos=f32(np.cos(tw.T)), tw2_sin=f32(np.sin(tw.T)))


def _first_step():
    return (pl.program_id(0) == 0) & (pl.program_id(1) == 0)


def _stage1_body(f_ref, z_ref, c_ref, s_ref, o_ref, fb_ref):
    n1h, g, tc = z_ref.shape
    nk1 = o_ref.shape[1]
    half = nk1 * SUBLANES
    reps = tc // LANES

    @pl.when(_first_step())
    def _():
        fb_ref[...] = f_ref[...].astype(BF16)

    z = z_ref[...].astype(F32)
    re_parts, im_parts = [], []
    for a in range(g // SUBLANES):
        zz = z[:, a * SUBLANES:(a + 1) * SUBLANES, :].reshape(n1h * SUBLANES, tc)
        acc = jnp.dot(fb_ref[...], zz.astype(BF16), preferred_element_type=F32)
        re, im = acc[:half], acc[half:]
        c = jnp.tile(c_ref[a], (1, reps))
        s = jnp.tile(s_ref[a], (1, reps))
        re_parts.append((re * c + im * s).reshape(nk1, SUBLANES, tc))
        im_parts.append((im * c - re * s).reshape(nk1, SUBLANES, tc))
    o_ref[0] = jnp.concatenate(re_parts, axis=1).astype(o_ref.dtype)
    o_ref[1] = jnp.concatenate(im_parts, axis=1).astype(o_ref.dtype)


def _dft_stage1(plan, src, width, part=0):
    n1, n2, nk1 = plan["n1"], plan["n2"], plan["nk1"]
    g = DFT_ROW_GROUP
    tc = _tile(width, 1024, LANES)
    nc = width // tc
    sub = g // SUBLANES
    tw = lambda t: t.reshape(n2 // g, sub, nk1 * SUBLANES, LANES)
    tw_spec = pl.BlockSpec((None, sub, nk1 * SUBLANES, LANES), lambda i, j: (i, 0, 0, 0))
    f = jnp.asarray(plan["f1"])
    return pl.pallas_call(
        _stage1_body,
        grid=(n2 // g, nc),
        in_specs=[pl.BlockSpec(f.shape, lambda i, j: (0, 0)),
                  pl.BlockSpec((n1 // 2, g, tc), lambda i, j: (0, i, part * nc + j)),
                  tw_spec, tw_spec],
        out_specs=pl.BlockSpec((2, nk1, g, tc), lambda i, j: (0, 0, i, j)),
        out_shape=jax.ShapeDtypeStruct((2, nk1, n2, width), BF16),
        scratch_shapes=[pltpu.VMEM(f.shape, BF16)],
        compiler_params=_params("arbitrary", "arbitrary"),
        name="dft_stage1",
    )(f, src.reshape(n1 // 2, n2, src.shape[1]), tw(plan["tw1_cos_lanes"]), tw(plan["tw1_sin_lanes"]))


def _istage1_body(f_ref, q_ref, x_ref, v_ref, skip_ref, o_ref, fb_ref):
    _, nk1, g, tc = q_ref.shape
    n1h = o_ref.shape[0]
    nk1_pad = f_ref.shape[1] // (2 * SUBLANES)

    @pl.when(_first_step())
    def _():
        fb_ref[...] = f_ref[...].astype(BF16)

    q = q_ref[...].astype(F32)
    q = jnp.concatenate([q, jnp.zeros((2, nk1_pad - nk1, g, tc), F32)], axis=1)
    parts = []
    for a in range(g // SUBLANES):
        qq = q[:, :, a * SUBLANES:(a + 1) * SUBLANES, :].reshape(2 * nk1_pad * SUBLANES, tc)
        y = jnp.dot(fb_ref[...], qq.astype(BF16), preferred_element_type=F32)
        parts.append(y.reshape(n1h, SUBLANES, tc))
    y = jnp.concatenate(parts, axis=1)
    o_ref[...] = (x_ref[...] * (y + v_ref[...] * skip_ref[...])).astype(o_ref.dtype)


def _idft_gate(plan, q, x_src, x_part, v_src, v_part, skip, out_dtype):
    n1, n2, nk1 = plan["n1"], plan["n2"], plan["nk1"]
    d = q.shape[3]
    g = DFT_ROW_GROUP
    tc = _tile(d, 1024, LANES)
    nc = d // tc
    f = jnp.asarray(plan["f1i"])
    tok = lambda part: pl.BlockSpec((n1 // 2, g, tc), lambda i, j: (0, i, part * nc + j))
    view = lambda arr: arr.reshape(n1 // 2, n2, arr.shape[1])
    out = pl.pallas_call(
        _istage1_body,
        grid=(n2 // g, nc),
        in_specs=[pl.BlockSpec(f.shape, lambda i, j: (0, 0)),
                  pl.BlockSpec((2, nk1, g, tc), lambda i, j: (0, 0, i, j)),
                  tok(x_part), tok(v_part), pl.BlockSpec((1, tc), lambda i, j: (0, j))],
        out_specs=tok(0),
        out_shape=jax.ShapeDtypeStruct((n1 // 2, n2, d), out_dtype),
        scratch_shapes=[pltpu.VMEM(f.shape, BF16)],
        compiler_params=_params("arbitrary", "arbitrary"),
        name="idft_gate",
    )(f, q, view(x_src), view(v_src), skip.reshape(1, d))
    return out.reshape(n1 // 2 * n2, d)


SPECTRAL_ROWS_PER_STEP = 5


def _spectral_body(a_ref, ah_ref, ss_ref, g2_ref, g2i_ref, c_ref, s_ref, o_ref):
    _, kb, n2, td = a_ref.shape
    g2 = g2_ref[...].astype(BF16)
    g2i = g2i_ref[...].astype(BF16)
    scale = lax.rsqrt(ss_ref[...] + EPS)
    reps = td // LANES
    for b in range(kb):
        stack = lambda ref: jnp.concatenate([ref[0, b], ref[1, b]], axis=0)
        x = jnp.dot(g2, stack(a_ref), preferred_element_type=F32)
        hf = jnp.dot(g2, stack(ah_ref), preferred_element_type=F32)
        xr, xi, hr, hi = x[:n2], x[n2:], hf[:n2], hf[n2:]
        y = jnp.concatenate([(xr * hr - xi * hi) * scale, (xr * hi + xi * hr) * scale], axis=0)
        p = jnp.dot(g2i, y.astype(BF16), preferred_element_type=F32)
        pr, pi = p[:n2], p[n2:]
        c = jnp.tile(c_ref[b], (1, reps))
        s = jnp.tile(s_ref[b], (1, reps))
        o_ref[0, b] = (pr * c - pi * s).astype(o_ref.dtype)
        o_ref[1, b] = (pr * s + pi * c).astype(o_ref.dtype)


def _spectral_conv(plan, a, ah, ss, order):
    n1, n2, d = a.shape[1:]
    td = _tile(d, 2048, LANES)
    nd = d // td
    kb = max(b for b in range(1, SPECTRAL_ROWS_PER_STEP + 1) if n1 % b == 0)
    blk = lambda off: pl.BlockSpec((2, kb, n2, td), lambda k1, j: (0, k1, 0, off + j))
    const = pl.BlockSpec((2 * n2, 2 * n2), lambda k1, j: (0, 0))
    tw = pl.BlockSpec((kb, n2, LANES), lambda k1, j: (k1, 0, 0))
    return pl.pallas_call(
        _spectral_body,
        grid=(n1 // kb, nd),
        in_specs=[blk(0), blk(order * nd), pl.BlockSpec((1, td), lambda k1, j: (0, order * nd + j)),
                  const, const, tw, tw],
        out_specs=blk(0),
        out_shape=jax.ShapeDtypeStruct((2, n1, n2, d), BF16),
        compiler_params=_params("parallel", "parallel"),
        name="spectral_conv",
    )(a, ah, ss, jnp.asarray(plan["g2"]), jnp.asarray(plan["g2i"]),
      plan["tw2_cos_lanes"], plan["tw2_sin_lanes"])


@functools.lru_cache(maxsize=None)
def _dft_plan_single(n_tok):
    n = 2 * n_tok
    af = 2.0 * np.pi * np.outer(np.arange(n), np.arange(n_tok)) / n
    fwd = np.concatenate([np.cos(af), -np.sin(af)], axis=0)
    ai = 2.0 * np.pi * np.outer(np.arange(n_tok // 2, n_tok // 2 + n_tok), np.arange(n)) / n
    inv = np.concatenate([np.cos(ai), -np.sin(ai)], axis=1) / n
    return dict(n=n, fwd=fwd.astype(np.float32), inv=inv.astype(np.float32))


def _cmul_body(x_ref, h_ref, ss_ref, o_ref):
    n = x_ref.shape[0] // 2
    xr, xi, hr, hi = x_ref[:n], x_ref[n:], h_ref[:n], h_ref[n:]
    scale = lax.rsqrt(ss_ref[...] + EPS)
    o_ref[:n] = ((xr * hr - xi * hi) * scale).astype(o_ref.dtype)
    o_ref[n:] = ((xr * hi + xi * hr) * scale).astype(o_ref.dtype)


def _cmul(x, h, ss, d, order):
    rows = x.shape[0]
    td = _tile(d, 512, LANES)
    nd = d // td
    return pl.pallas_call(
        _cmul_body,
        grid=(nd,),
        in_specs=[pl.BlockSpec((rows, td), lambda j: (0, j)),
                  pl.BlockSpec((rows, td), lambda j: (0, order * nd + j)),
                  pl.BlockSpec((1, td), lambda j: (0, order * nd + j))],
        out_specs=pl.BlockSpec((rows, td), lambda j: (0, j)),
        out_shape=jax.ShapeDtypeStruct((rows, d), BF16),
        compiler_params=_params("parallel"),
        name="spectrum_product",
    )(x, h, ss)


def _hyena_mixer(x, norm, w_in, layer, b_in, conv_w, conv_b, fparams, skip, plans):
    n_tok, d = x.shape
    u = _norm_matmul_conv(x, *norm, w_in, layer, b_in, conv_w, conv_b, name="hyena_in")
    filt, ss = _hyena_filters(n_tok, d, *fparams)

    if n_tok <= SINGLE_STAGE_MAX_LEN:
        plan = _dft_plan_single(n_tok)
        fwd, inv = jnp.asarray(plan["fwd"]), jnp.asarray(plan["inv"])
        part_cols = lambda p: (lambda tn: (lambda j: p * (d // tn) + j))
        hf = _matmul(fwd, filt, out_dtype=F32, tm=2 * plan["n"], tn=512, name="dft_filter")
        z_src, z_cols = u, part_cols(HYENA_ORDER)
        for order in range(HYENA_ORDER):
            x = _matmul(fwd, z_src, out_dtype=F32, tm=2 * plan["n"], tn=512, w_cols=(d, z_cols),
                        name="dft_signal")
            y = _cmul(x, hf, ss, d, order)
            last = order == HYENA_ORDER - 1
            z_src = _matmul(inv, y, out_dtype=BF16 if last else F32, epilogue=_ep_gate, tn=512,
                            extras=(_tile_extra(u, part_cols(order)), _tile_extra(z_src, z_cols),
                                    _row_extra(skip[order])), name="idft_gate")
            z_cols = part_cols(0)
        return z_src

    if n_tok not in plans:
        base = _dft_plan(n_tok)
        plans[n_tok] = dict(base, **{name + "_lanes": _lane_table(base[name])
                                     for name in ("tw1_cos", "tw1_sin", "tw2_cos", "tw2_sin")})
    plan = plans[n_tok]
    ah = _dft_stage1(plan, filt, HYENA_ORDER * d)
    z_src, z_part = u, HYENA_ORDER
    for order in range(HYENA_ORDER):
        a = _dft_stage1(plan, z_src, d, z_part)
        q = _spectral_conv(plan, a, ah, ss, order)
        last = order == HYENA_ORDER - 1
        z_src = _idft_gate(plan, q, u, order, z_src, z_part, skip[order], BF16 if last else F32)
        z_part = 0
    return z_src


def kernel(x, c, ctx, c_ctx, mod_w, mod_b, norm_mix_g, norm_mlp_g, attn_w_in, attn_w_out, attn_sink, attn_q_norm, attn_k_norm, hy_w_in, hy_b_in, hy_conv_w, hy_conv_b, hy_f_w1, hy_f_b1, hy_f_freq1, hy_f_w2, hy_f_b2, hy_f_freq2, hy_f_w3, hy_skip, hy_w_out, hy_b_out, mlp_w1, mlp_w2, final_g):
    batch, n_lat, d = x.shape
    n_ctx = ctx.shape[1]
    depth = mod_w.shape[0]
    assert batch == 1 and n_lat % GRID_W == 0 and n_lat % WINDOW == 0
    last_ctx_layer = 2 * ((depth - 1) // 2)
    rope = _rope_tables(n_lat)
    d_ff = mlp_w1.shape[2]
    attn_w_out_bf, hy_w_out_bf = attn_w_out.astype(BF16), hy_w_out.astype(BF16)
    dft_plans = {}

    cond = jnp.zeros((16, d), F32).at[0].set(c[0]).at[1].set(c_ctx)
    mod = _modulation(cond, mod_w, mod_b)

    xs = x[0]
    cs = ctx[0]
    for i in range(depth):
        j = i // 2
        is_attn = i % 2 == 0
        ctx_updated = i < last_ctx_layer
        sh1, sc1, g1, sh2, sc2, g2 = jnp.split(mod[i, 0:1], N_MOD, axis=-1)
        csh1, csc1, cg1, csh2, csc2, cg2 = jnp.split(mod[i, 1:2], N_MOD, axis=-1)
        if is_attn:
            qkv, h_lat = _attn_in(xs, norm_mix_g[i], sh1, sc1, attn_w_in, j, attn_q_norm[j], attn_k_norm[j], rope)
            qkv_c, h_ctx = _attn_in(cs, norm_mix_g[i], csh1, csc1, attn_w_in, j, attn_q_norm[j], attn_k_norm[j],
                                    None)
            v_col0 = Q_WIDTH + KV_WIDTH
            vt = _project_t(attn_w_in, j, v_col0, KV_WIDTH, h_lat)
            vt_c = _project_t(attn_w_in, j, v_col0, KV_WIDTH, h_ctx)
            o_a = _window_attention(qkv, vt, qkv_c, vt_c, attn_sink[j])
            o_b = _global_attention(qkv, vt, qkv_c, vt_c)
            xs = _attn_out(o_a, o_b, attn_w_out_bf, j, xs, g1)
            if ctx_updated:
                o_c = _context_attention(qkv_c, attn_sink[j])
                cs = _matmul(o_c, attn_w_out_bf, layer=j, out_dtype=F32, epilogue=_ep_resid,
                             extras=(_tile_extra(cs), _row_extra(cg1)), name="attn_out_ctx")
        else:
            fparams = (hy_f_w1[j], hy_f_b1[j], hy_f_freq1[j], hy_f_w2[j], hy_f_b2[j], hy_f_freq2[j], hy_f_w3[j])
            z = _hyena_mixer(xs, (norm_mix_g[i], sh1, sc1), hy_w_in, j, hy_b_in[j], hy_conv_w[j], hy_conv_b[j],
                             fparams, hy_skip[j], dft_plans)
            xs = _matmul(z, hy_w_out_bf, layer=j, out_dtype=F32, epilogue=_ep_resid_bias, tm=512, tn=2048,
                         extras=(_tile_extra(xs), _row_extra(g1), _row_extra(hy_b_out[j])), name="hyena_out")
            if ctx_updated:
                z_c = _hyena_mixer(cs, (norm_mix_g[i], csh1, csc1), hy_w_in, j, hy_b_in[j], hy_conv_w[j],
                                   hy_conv_b[j], fparams, hy_skip[j], dft_plans)
                cs = _matmul(z_c, hy_w_out_bf, layer=j, out_dtype=F32, epilogue=_ep_resid_bias,
                             extras=(_tile_extra(cs), _row_extra(cg1), _row_extra(hy_b_out[j])),
                             name="hyena_out_ctx")
        a1 = _norm_matmul(xs, norm_mlp_g[i], sh2, sc2, mlp_w1, layer=i, out_dtype=BF16, epilogue=_ep_relu2,
                          name="mlp_up")
        xs = _matmul(a1, mlp_w2, layer=i, out_dtype=F32, epilogue=_ep_resid, tm=BIG_TM, tn=256, tk=d_ff,
                     extras=(_tile_extra(xs), _row_extra(g2)), name="mlp_down")
        if ctx_updated:
            a1c = _norm_matmul(cs, norm_mlp_g[i], csh2, csc2, mlp_w1, layer=i, out_dtype=BF16,
                               epilogue=_ep_relu2, name="mlp_up_ctx")
            cs = _matmul(a1c, mlp_w2, layer=i, out_dtype=F32, epilogue=_ep_resid, tn=512, tk=d_ff,
                         extras=(_tile_extra(cs), _row_extra(cg2)), name="mlp_down_ctx")
    return _rms_norm(xs, final_g, out_dtype=F32)[None]
```
